```python
import math
import jax, jax.numpy as jnp
from jax import lax
import numpy as np

D_MODEL = 2048
BATCH = 1
SEQ = 8192
DEPTH = 1

GRID_W = 64
CTX_LEN = 256
EPS = 1e-6
F32 = jnp.float32

SSD_D_INNER = 2048
SSD_HEADDIM = 64
SSD_HEADS = SSD_D_INNER // SSD_HEADDIM
SSD_GROUPS = 4
SSD_STATE = 128
SSD_CHUNK = 128
SSD_CONV = 5
SSD_GN = SSD_GROUPS * SSD_STATE
SSD_XBC = SSD_D_INNER + 2 * SSD_GN

HY_WIDTH = 2048
HY_ORDER = 2
HY_SHORT = 3
HY_EMB = 33
HY_BANDS = (HY_EMB - 1) // 2
HY_FILTER_HIDDEN = 64
HY_DECAY_TARGET = 1e-2
HY_FAST_DECAY_PCT = 0.3
HY_SLOW_DECAY_PCT = 1.5

N_BRANCH = 2
COL_Z = 0
COL_XBC = COL_Z + SSD_D_INNER
COL_DT = COL_XBC + SSD_XBC
COL_HY = COL_DT + 2 * SSD_HEADS
COL_GATE = COL_HY + (HY_ORDER + 1) * HY_WIDTH
IN_COLS = COL_GATE + N_BRANCH * D_MODEL

MOE_GROUPS = 4
MOE_PER_GROUP = 4
MOE_EXPERTS = MOE_GROUPS * MOE_PER_GROUP
MOE_HIDDEN = 512
MOE_TOPK = 2

kernel_name = 'hybrid_ssd_hyena_hmoe_dit_block'


def rmsnorm(x, g):
    xf = x.astype(F32)
    y = xf * lax.rsqrt(jnp.mean(xf * xf, axis=-1, keepdims=True) + EPS)
    return (y * g.astype(F32)).astype(x.dtype)


def modulate(h, shift, scale):
    return h * (1 + scale) + shift


def adaln(cvec, w, b, n):
    m = jax.nn.silu(cvec) @ w[:, :n * D_MODEL] + b[:n * D_MODEL]
    return jnp.split(m[..., None, :], n, axis=-1)


def flip(t):
    return jnp.flip(t, axis=1)


def short_conv(u, w, b, rows):
    bsz, L, C = u.shape
    K = w.shape[0]
    v = u if rows is None else u.reshape(bsz * rows, GRID_W, C)
    out = lax.conv_general_dilated(v, w[:, None, :].astype(u.dtype), window_strides=(1,),
                                   padding=[(K // 2, K // 2)],
                                   dimension_numbers=('NWC', 'WIO', 'NWC'),
                                   feature_group_count=C)
    return out.reshape(bsz, L, C) + b.astype(u.dtype)


def dt_pair(dt_raw, dt_bias):
    bsz, L, _ = dt_raw.shape
    return jax.nn.softplus(dt_raw.astype(F32).reshape(bsz, L, 2, SSD_HEADS) + dt_bias.astype(F32))


def ssd_chunked(xs, dt, a, bm, cm, h0):
    bsz, L, H, P = xs.shape
    G, N, Q = SSD_GROUPS, SSD_STATE, SSD_CHUNK
    R = H // G
    nc = L // Q
    xd = (xs.astype(F32) * dt[..., None]).reshape(bsz, nc, Q, G, R, P)
    ad = (dt * a).reshape(bsz, nc, Q, G, R).transpose(0, 3, 4, 1, 2)
    bc = bm.astype(F32).reshape(bsz, nc, Q, G, N)
    cc = cm.astype(F32).reshape(bsz, nc, Q, G, N)
    acs = jnp.cumsum(ad, axis=-1)
    lower = jnp.tril(jnp.ones((Q, Q), dtype=bool))
    decay_in = jnp.exp(jnp.where(lower, acs[..., :, None] - acs[..., None, :], -jnp.inf))
    cb = jnp.einsum('bclgn,bcsgn->bgcls', cc, bc)
    y_diag = jnp.einsum('bgcls,bgrcls,bcsgrp->bclgrp', cb, decay_in, xd)
    decay_to_end = jnp.exp(acs[..., -1:] - acs)
    chunk_states = jnp.einsum('bcsgn,bgrcs,bcsgrp->cbgrpn', bc, decay_to_end, xd)
    chunk_decay = jnp.exp(acs[..., -1]).transpose(3, 0, 1, 2)

    def step(h, inp):
        s, d = inp
        return d[..., None, None] * h + s, h

    h_final, h_in = lax.scan(step, h0.astype(F32), (chunk_states, chunk_decay))
    y_off = jnp.einsum('bclgn,cbgrpn,bgrcl->bclgrp', cc, h_in, jnp.exp(acs))
    return (y_diag + y_off).reshape(bsz, L, H, P), h_final


def ssd_final_state(xs, dt, a, bm):
    bsz, L, H, P = xs.shape
    G = SSD_GROUPS
    R = H // G
    acs = jnp.cumsum(dt * a, axis=1)
    w = (jnp.exp(acs[:, -1:] - acs) * dt).reshape(bsz, L, G, R)
    return jnp.einsum('blgn,blgr,blgrp->bgrpn', bm.astype(F32), w,
                      xs.astype(F32).reshape(bsz, L, G, R, P))


def gated_rmsnorm(y, z, g):
    bsz, L, _ = y.shape
    yz = (y * jax.nn.silu(z.astype(F32))).reshape(bsz, L, SSD_GROUPS, -1)
    yz = yz * lax.rsqrt(jnp.mean(yz * yz, axis=-1, keepdims=True) + EPS)
    return yz.reshape(bsz, L, SSD_D_INNER) * g.astype(F32)


def hyena_filters(L, lp):
    t = jnp.linspace(0.0, 1.0, L, dtype=F32)
    w = 2 * math.pi * jnp.arange(L, dtype=F32)[:, None] / L
    bands = jnp.linspace(1e-4, HY_BANDS - 1, HY_BANDS, dtype=F32)[None, :]
    feats = jnp.concatenate([t[:, None], jnp.cos(bands * w), -jnp.sin(bands * w)], axis=-1)
    h = jnp.sin(lp['hy_f_freq1'].astype(F32) * (feats @ lp['hy_f_w1'].astype(F32) + lp['hy_f_b1'].astype(F32)))
    h = jnp.sin(lp['hy_f_freq2'].astype(F32) * (h @ lp['hy_f_w2'].astype(F32) + lp['hy_f_b2'].astype(F32)))
    h = (h @ lp['hy_f_w3'].astype(F32)).reshape(L, HY_ORDER, 2, HY_WIDTH)
    max_decay = math.log(HY_DECAY_TARGET) / HY_FAST_DECAY_PCT
    min_decay = math.log(HY_DECAY_TARGET) / HY_SLOW_DECAY_PCT
    deltas = jnp.abs(jnp.linspace(min_decay, max_decay, HY_WIDTH, dtype=F32))
    h = h * jnp.exp(-t[:, None] * deltas[None, :])[:, None, None, :]
    k = jnp.concatenate([h[:, :, 0], jnp.zeros((1, HY_ORDER, HY_WIDTH), F32),
                         jnp.flip(h[1:, :, 1], axis=0)], axis=0)
    k = k / jnp.sum(jnp.abs(k), axis=0, keepdims=True)
    return jnp.fft.rfft(k, n=2 * L, axis=0)


def fft_long_conv(u, kf, bias):
    L = u.shape[1]
    uf = jnp.fft.rfft(u, n=2 * L, axis=1)
    y = jnp.fft.irfft(uf * kf[None], n=2 * L, axis=1)[:, :L]
    return y + u * bias.astype(F32)


def mixer(p, rows, h0f, h0b, lp):
    bsz, L, _ = p.shape
    dtype = p.dtype
    z = p[..., COL_Z:COL_XBC]
    xbc = jax.nn.silu(short_conv(p[..., COL_XBC:COL_DT], lp['ssd_conv_w'], lp['ssd_conv_b'], rows))
    xs = xbc[..., :SSD_D_INNER].reshape(bsz, L, SSD_HEADS, SSD_HEADDIM)
    bm = xbc[..., SSD_D_INNER:SSD_D_INNER + SSD_GN].reshape(bsz, L, SSD_GROUPS, SSD_STATE)
    cm = xbc[..., SSD_D_INNER + SSD_GN:].reshape(bsz, L, SSD_GROUPS, SSD_STATE)
    dt = dt_pair(p[..., COL_DT:COL_HY], lp['ssd_dt_bias'])
    a = -jnp.exp(lp['ssd_a_log'].astype(F32))
    yf, hf = ssd_chunked(xs, dt[:, :, 0], a[0], bm, cm, h0f)
    yb, hb = ssd_chunked(flip(xs), flip(dt[:, :, 1]), a[1], flip(bm), flip(cm), h0b)
    y = yf + flip(yb) + lp['ssd_d'].astype(F32)[:, None] * xs.astype(F32)
    y_ssd = gated_rmsnorm(y.reshape(bsz, L, SSD_D_INNER), z, lp['ssd_norm_g']).astype(dtype)
    kf = hyena_filters(L, lp)
    u3 = short_conv(p[..., COL_HY:COL_GATE], lp['hy_conv_w'], lp['hy_conv_b'], rows).astype(F32)
    v, x1, x2 = jnp.split(u3, 3, axis=-1)
    z2 = x1 * fft_long_conv(v, kf[:, 0], lp['hy_bias'][0])
    y_hy = (x2 * fft_long_conv(z2, kf[:, 1], lp['hy_bias'][1])).astype(dtype)
    gates = jax.nn.sigmoid((p[..., COL_GATE:] + lp['gate_b']).astype(F32)).astype(dtype)
    merged = gates[..., :D_MODEL] * (y_ssd @ lp['w_ssd_out']) + gates[..., D_MODEL:] * (y_hy @ lp['w_hy_out'])
    return merged @ lp['w_o'], hf, hb


def context_states(hc, w_in, conv_w, conv_b, dt_bias, a_log):
    xb_cols = SSD_D_INNER + SSD_GN
    xb = jax.nn.silu(short_conv(hc @ w_in[:, COL_XBC:COL_XBC + xb_cols], conv_w[:, :xb_cols],
                                conv_b[:xb_cols], None))
    bsz, L, _ = xb.shape
    xs = xb[..., :SSD_D_INNER].reshape(bsz, L, SSD_HEADS, SSD_HEADDIM)
    bm = xb[..., SSD_D_INNER:].reshape(bsz, L, SSD_GROUPS, SSD_STATE)
    dt = dt_pair(hc @ w_in[:, COL_DT:COL_HY], dt_bias)
    a = -jnp.exp(a_log.astype(F32))
    h_fwd = ssd_final_state(xs, dt[:, :, 0], a[0], bm)
    h_bwd = ssd_final_state(flip(xs), flip(dt[:, :, 1]), a[1], flip(bm))
    return h_fwd, h_bwd


def hier_moe(h, w_rg, b_rg, w_re, b_re, w_gate, w_up, w_down):
    g_logits = (h @ w_rg).astype(F32) + b_rg.astype(F32)
    g_prob = jax.nn.softmax(g_logits, axis=-1)
    _, g_idx = lax.top_k(g_logits, 1)
    g_w = jnp.take_along_axis(g_prob, g_idx, axis=-1)
    bsz, L, _ = h.shape
    e_logits = ((h @ w_re).astype(F32) + b_re.astype(F32)).reshape(bsz, L, MOE_GROUPS, MOE_PER_GROUP)
    e_in_group = jnp.einsum('blge,blg->ble', e_logits, jax.nn.one_hot(g_idx[..., 0], MOE_GROUPS, dtype=F32))
    e_val, e_idx = lax.top_k(e_in_group, MOE_TOPK)
    e_w = jax.nn.softmax(e_val, axis=-1) * g_w
    expert = g_idx * MOE_PER_GROUP + e_idx
    combine = jnp.sum(jax.nn.one_hot(expert, MOE_EXPERTS, dtype=F32) * e_w[..., None], axis=-2)
    hid = jax.nn.silu(jnp.einsum('bld,edf->blef', h, w_gate)) * jnp.einsum('bld,edf->blef', h, w_up)
    hid = hid * combine.astype(h.dtype)[..., None]
    return jnp.einsum('blef,efd->bld', hid, w_down)


def setup_inputs(seed: int = 0) -> dict:
    key = jax.random.key(seed)
    keys = list(jax.random.split(key, 40))

    def nrm(shape, scale):
        return scale * jax.random.normal(keys.pop(), shape, F32)

    def uni(shape, lo, hi):
        return jax.random.uniform(keys.pop(), shape, F32, lo, hi)

    def dt_bias_init(shape):
        dt0 = jnp.exp(uni(shape, math.log(1e-3), math.log(1e-1)))
        return dt0 + jnp.log(-jnp.expm1(-dt0))

    D = D_MODEL
    return {
        'x': nrm((BATCH, SEQ, D), 1.0),
        'c': nrm((BATCH, D), 1.0),
        'ctx': nrm((BATCH, CTX_LEN, D), 1.0),
        'c_ctx': nrm((D,), 1.0),
        'ada_w': nrm((DEPTH, D, 6 * D), 0.5 * D ** -0.5),
        'ada_b': nrm((DEPTH, 6 * D), 0.02),
        'norm1_g': 1.0 + nrm((DEPTH, D), 0.02),
        'w_in': nrm((DEPTH, D, IN_COLS), D ** -0.5),
        'ssd_conv_w': nrm((DEPTH, SSD_CONV, SSD_XBC), SSD_CONV ** -0.5),
        'ssd_conv_b': nrm((DEPTH, SSD_XBC), 0.02),
        'ssd_dt_bias': dt_bias_init((DEPTH, 2, SSD_HEADS)),
        'ssd_a_log': jnp.log(uni((DEPTH, 2, SSD_HEADS), 1.0, 16.0)),
        'ssd_d': 1.0 + nrm((DEPTH, SSD_HEADS), 0.1),
        'ssd_norm_g': 1.0 + nrm((DEPTH, SSD_D_INNER), 0.02),
        'w_ssd_out': nrm((DEPTH, SSD_D_INNER, D), SSD_D_INNER ** -0.5),
        'hy_conv_w': nrm((DEPTH, HY_SHORT, (HY_ORDER + 1) * HY_WIDTH), HY_SHORT ** -0.5),
        'hy_conv_b': nrm((DEPTH, (HY_ORDER + 1) * HY_WIDTH), 0.02),
        'hy_f_w1': nrm((DEPTH, HY_EMB, HY_FILTER_HIDDEN), HY_EMB ** -0.5),
        'hy_f_b1': nrm((DEPTH, HY_FILTER_HIDDEN), 0.1),
        'hy_f_freq1': 1.0 + nrm((DEPTH, HY_FILTER_HIDDEN), 0.05),
        'hy_f_w2': nrm((DEPTH, HY_FILTER_HIDDEN, HY_FILTER_HIDDEN), HY_FILTER_HIDDEN ** -0.5),
        'hy_f_b2': nrm((DEPTH, HY_FILTER_HIDDEN), 0.1),
        'hy_f_freq2': 1.0 + nrm((DEPTH, HY_FILTER_HIDDEN), 0.05),
        'hy_f_w3': nrm((DEPTH, HY_FILTER_HIDDEN, HY_ORDER * 2 * HY_WIDTH), HY_FILTER_HIDDEN ** -0.5),
        'hy_bias': nrm((DEPTH, HY_ORDER, HY_WIDTH), 0.5),
        'w_hy_out': nrm((DEPTH, HY_WIDTH, D), HY_WIDTH ** -0.5),
        'gate_b': nrm((DEPTH, N_BRANCH * D), 0.02),
        'w_o': nrm((DEPTH, D, D), D ** -0.5),
        'norm2_g': 1.0 + nrm((DEPTH, D), 0.02),
        'moe_w_rg': nrm((DEPTH, D, MOE_GROUPS), D ** -0.5),
        'moe_b_rg': nrm((DEPTH, MOE_GROUPS), 0.01),
        'moe_w_re': nrm((DEPTH, D, MOE_EXPERTS), D ** -0.5),
        'moe_b_re': nrm((DEPTH, MOE_EXPERTS), 0.01),
        'moe_w_gate': nrm((DEPTH, MOE_EXPERTS, D, MOE_HIDDEN), D ** -0.5),
        'moe_w_up': nrm((DEPTH, MOE_EXPERTS, D, MOE_HIDDEN), D ** -0.5),
        'moe_w_down': nrm((DEPTH, MOE_EXPERTS, MOE_HIDDEN, D), MOE_HIDDEN ** -0.5),
        'final_g': 1.0 + nrm((D,), 0.02),
    }


def reference(x, c, ctx, c_ctx, ada_w, ada_b, norm1_g, w_in, ssd_conv_w, ssd_conv_b, ssd_dt_bias,
              ssd_a_log, ssd_d, ssd_norm_g, w_ssd_out, hy_conv_w, hy_conv_b, hy_f_w1, hy_f_b1,
              hy_f_freq1, hy_f_w2, hy_f_b2, hy_f_freq2, hy_f_w3, hy_bias, w_hy_out, gate_b, w_o,
              norm2_g, moe_w_rg, moe_b_rg, moe_w_re, moe_b_re, moe_w_gate, moe_w_up, moe_w_down,
              final_g):
    rows = x.shape[1] // GRID_W
    for i in range(DEPTH):
        last = i == DEPTH - 1
        lp = dict(ssd_conv_w=ssd_conv_w[i], ssd_conv_b=ssd_conv_b[i], ssd_dt_bias=ssd_dt_bias[i],
                  ssd_a_log=ssd_a_log[i], ssd_d=ssd_d[i], ssd_norm_g=ssd_norm_g[i],
                  w_ssd_out=w_ssd_out[i], hy_conv_w=hy_conv_w[i], hy_conv_b=hy_conv_b[i],
                  hy_f_w1=hy_f_w1[i], hy_f_b1=hy_f_b1[i], hy_f_freq1=hy_f_freq1[i],
                  hy_f_w2=hy_f_w2[i], hy_f_b2=hy_f_b2[i], hy_f_freq2=hy_f_freq2[i],
                  hy_f_w3=hy_f_w3[i], hy_bias=hy_bias[i], w_hy_out=w_hy_out[i],
                  gate_b=gate_b[i], w_o=w_o[i])
        moe_args = (moe_w_rg[i], moe_b_rg[i], moe_w_re[i], moe_b_re[i], moe_w_gate[i], moe_w_up[i],
                    moe_w_down[i])
        sh1, sc1, gt1, sh2, sc2, gt2 = adaln(c, ada_w[i], ada_b[i], 6)
        hx = modulate(rmsnorm(x, norm1_g[i]), sh1, sc1)
        if last:
            csh1, csc1 = adaln(c_ctx, ada_w[i], ada_b[i], 2)
            hc = modulate(rmsnorm(ctx, norm1_g[i]), csh1, csc1)
            h0f, h0b = context_states(hc, w_in[i], ssd_conv_w[i], ssd_conv_b[i], ssd_dt_bias[i], ssd_a_log[i])
        else:
            csh1, csc1, cgt1, csh2, csc2, cgt2 = adaln(c_ctx, ada_w[i], ada_b[i], 6)
            hc = modulate(rmsnorm(ctx, norm1_g[i]), csh1, csc1)
            zero_state = jnp.zeros((ctx.shape[0], SSD_GROUPS, SSD_HEADS // SSD_GROUPS, SSD_HEADDIM, SSD_STATE), F32)
            out_c, h0f, h0b = mixer(hc @ w_in[i], None, zero_state, zero_state, lp)
            ctx = ctx + cgt1 * out_c
            ctx = ctx + cgt2 * hier_moe(modulate(rmsnorm(ctx, norm2_g[i]), csh2, csc2), *moe_args)
        out_x, _, _ = mixer(hx @ w_in[i], rows, h0f, h0b, lp)
        x = x + gt1 * out_x
        x = x + gt2 * hier_moe(modulate(rmsnorm(x, norm2_g[i]), sh2, sc2), *moe_args)
    return rmsnorm(x, final_g)
```

```python
import functools
import math

import jax
import jax.numpy as jnp
from jax import lax
from jax.experimental import pallas as pl
from jax.experimental.pallas import tpu as pltpu

F32 = jnp.float32
BF16 = jnp.bfloat16

EPS = 1e-6
GRID_W = 64
SSD_HEADS = 32
SSD_HEADDIM = 64
SSD_GROUPS = 4
SSD_STATE = 128
SSD_CHUNK = 128
SSD_D_INNER = SSD_HEADS * SSD_HEADDIM
SSD_GN = SSD_GROUPS * SSD_STATE
HY_BANDS = 16
HY_EMB = 33
HY_HIDDEN = 64
MOE_GROUPS = 4
MOE_PER_GROUP = 4
MOE_EXPERTS = 16

LANES = 128
DFT_B = 128
PITCH = 136
VMEM_LIMIT = 56 * 1024 * 1024


def _cparams(sem):
    return pltpu.CompilerParams(dimension_semantics=sem, vmem_limit_bytes=VMEM_LIMIT)


def _dot(a, b):
    return jnp.dot(a, b, preferred_element_type=F32)


def _sigmoid(x):
    return 1.0 / (1.0 + jnp.exp(-x))


def _silu(x):
    return x * _sigmoid(x)


def _split3(v):
    b1 = v.astype(BF16)
    r1 = v - b1.astype(F32)
    b2 = r1.astype(BF16)
    b3 = (r1 - b2.astype(F32)).astype(BF16)
    return b1, b2, b3


def _dot_sel_r(v, sel_b):
    b1, b2, b3 = _split3(v)
    return _dot(b1, sel_b) + _dot(b2, sel_b) + _dot(b3, sel_b)


def _dot_sel_l(sel_b, v):
    b1, b2, b3 = _split3(v)
    return _dot(sel_b, b1) + _dot(sel_b, b2) + _dot(sel_b, b3)


def _dot_hi(a, b):
    a1, a2, a3 = _split3(a)
    b1, b2, b3 = _split3(b)
    return (_dot(a1, b1) + (_dot(a1, b2) + _dot(a2, b1))
            + (_dot(a1, b3) + _dot(a2, b2) + _dot(a3, b1)))


def _adaln_kernel(c_ref, w_ref, b_ref, o_ref):
    s = _silu(c_ref[...]).astype(BF16)
    o_ref[...] = _dot(s, w_ref[...].astype(BF16)) + b_ref[...]


def _adaln(cc, w, b, tn=1024):
    m, d = cc.shape
    n = w.shape[1]
    return pl.pallas_call(
        _adaln_kernel,
        out_shape=jax.ShapeDtypeStruct((m, n), F32),
        grid=(n // tn,),
        in_specs=[pl.BlockSpec((m, d), lambda j: (0, 0)),
                  pl.BlockSpec((d, tn), lambda j: (0, j)),
                  pl.BlockSpec((1, tn), lambda j: (0, j))],
        out_specs=pl.BlockSpec((m, tn), lambda j: (0, j)),
        compiler_params=_cparams(("parallel",)),
        name="adaln",
    )(cc, w, b.reshape(1, n))


def _normmod_kernel(x_ref, g_ref, sh_ref, sc_ref, o_ref):
    x = x_ref[...]
    y = x * lax.rsqrt(jnp.mean(x * x, axis=-1, keepdims=True) + EPS) * g_ref[...]
    o_ref[...] = (y * (1.0 + sc_ref[...]) + sh_ref[...]).astype(o_ref.dtype)


def _normmod(x, g, sh, sc, tm):
    m, d = x.shape
    row = pl.BlockSpec((1, d), lambda i: (0, 0))
    return pl.pallas_call(
        _normmod_kernel,
        out_shape=jax.ShapeDtypeStruct((m, d), BF16),
        grid=(m // tm,),
        in_specs=[pl.BlockSpec((tm, d), lambda i: (i, 0)), row, row, row],
        out_specs=pl.BlockSpec((tm, d), lambda i: (i, 0)),
        compiler_params=_cparams(("parallel",)),
        name="normmod",
    )(x, g.reshape(1, d), sh, sc)


def _conv_rows(acc, cw_ref, group):
    tm, tn = acc.shape
    taps = cw_ref.shape[0]
    half = taps // 2
    pos = lax.broadcasted_iota(jnp.int32, (tm, tn), 0) & (group - 1)
    out = acc * cw_ref[half:half + 1, :]
    for k in range(taps):
        d = k - half
        if d == 0:
            continue
        shifted = pltpu.roll(acc, (tm - d) % tm, 0)
        valid = (pos + d >= 0) & (pos + d < group)
        out = out + jnp.where(valid, shifted, 0.0) * cw_ref[k:k + 1, :]
    return out


def _proj_kernel(a_ref, w_ref, *rest, epi, group):
    o_ref = rest[-1]
    acc = _dot(a_ref[...], w_ref[...])
    if epi == "none":
        out = acc
    elif epi == "conv":
        cw_ref, b_ref = rest[0], rest[1]
        out = _conv_rows(acc, cw_ref, group) + b_ref[...]
    elif epi == "conv_silu":
        cw_ref, b_ref = rest[0], rest[1]
        out = _silu(_conv_rows(acc, cw_ref, group) + b_ref[...])
    elif epi == "softplus":
        v = acc + rest[0][...]
        out = jnp.maximum(v, 0.0) + jnp.log(1.0 + jnp.exp(-jnp.abs(v)))
    else:
        raise ValueError(epi)
    o_ref[...] = out.astype(o_ref.dtype)


def _proj(a, w, epi, tm, tn, conv_w=None, bias=None, group=GRID_W, out_dtype=F32):
    m, k = a.shape
    n = w.shape[1]
    tn = min(tn, n)
    extra, extra_specs = [], []
    if conv_w is not None:
        extra.append(conv_w)
        extra_specs.append(pl.BlockSpec((conv_w.shape[0], tn), lambda i, j: (0, j)))
    if bias is not None:
        extra.append(bias.reshape(1, n))
        extra_specs.append(pl.BlockSpec((1, tn), lambda i, j: (0, j)))
    return pl.pallas_call(
        functools.partial(_proj_kernel, epi=epi, group=group),
        out_shape=jax.ShapeDtypeStruct((m, n), out_dtype),
        grid=(m // tm, n // tn),
        in_specs=[pl.BlockSpec((tm, k), lambda i, j: (i, 0)),
                  pl.BlockSpec((k, tn), lambda i, j: (0, j))] + extra_specs,
        out_specs=pl.BlockSpec((tm, tn), lambda i, j: (i, j)),
        compiler_params=_cparams(("parallel", "parallel")),
        name="proj_" + epi,
    )(a, w, *extra)


def _ssd_kernel(xs_ref, b_ref, c_ref, dt_ref, dtT_ref, alog_ref, alogT_ref, h0_ref, *rest,
                reverse, mode):
    if mode == "state":
        hfin_ref, st_ref = rest
    elif mode == "fwd":
        y_ref, st_ref = rest
    else:
        yf_ref, z_ref, dx_ref, ng_ref, y_ref, st_ref = rest
    q = xs_ref.shape[0]
    nh = SSD_HEADS
    rp = SSD_D_INNER // SSD_GROUPS
    step = pl.program_id(0)

    @pl.when(step == 0)
    def _():
        st_ref[...] = h0_ref[...]

    d = 1 if reverse else 0
    dtd = dt_ref[:, d * nh:(d + 1) * nh]
    dtdT = dtT_ref[d * nh:(d + 1) * nh, :]
    a_row = -jnp.exp(alog_ref[d:d + 1, :])
    a_col = -jnp.exp(alogT_ref[:, d:d + 1])
    ad = dtd * a_row
    adT = dtdT * a_col
    row = lax.broadcasted_iota(jnp.int32, (q, q), 0)
    col = lax.broadcasted_iota(jnp.int32, (q, q), 1)
    lower = col <= row
    upper = col >= row
    causal = upper if reverse else lower
    tri = jnp.where(causal, 1.0, 0.0).astype(BF16)
    triT = jnp.where(lower if reverse else upper, 1.0, 0.0).astype(BF16)
    acs = _dot_sel_l(tri, ad)
    acsT = _dot_sel_r(adT, triT)
    tot = jnp.sum(ad, axis=0, keepdims=True)
    dte = jnp.exp(tot - acs)
    eacs = jnp.exp(acs)
    cdec = jnp.exp(tot)
    hsel = lax.broadcasted_iota(jnp.int32, (nh, SSD_D_INNER), 0)
    lsel = lax.broadcasted_iota(jnp.int32, (nh, SSD_D_INNER), 1) // SSD_HEADDIM
    expand = jnp.where(hsel == lsel, 1.0, 0.0).astype(BF16)
    stacked = jnp.concatenate([dtd * dte, eacs, jnp.broadcast_to(cdec, (8, nh))], axis=0)
    exp_all = _dot_sel_r(stacked, expand)
    w_x = exp_all[0:q]
    eacs_x = exp_all[q:2 * q]
    cdec_x = exp_all[2 * q:2 * q + 1]

    xs = xs_ref[...]
    xs_b = xs.astype(BF16)
    xdw = (xs * w_x).astype(BF16)
    neg_inf = jnp.float32(-jnp.inf)
    y_groups = []
    for g in range(SSD_GROUPS):
        bg = b_ref[:, g * SSD_STATE:(g + 1) * SSD_STATE]
        st_old = st_ref[g]
        s_new = _dot(bg.T.astype(BF16), xdw[:, g * rp:(g + 1) * rp])
        if mode != "state":
            bg_b = bg.astype(BF16)
            cg_b = c_ref[:, g * SSD_STATE:(g + 1) * SSD_STATE].astype(BF16)
            cb = lax.dot_general(cg_b, bg_b, (((1,), (1,)), ((), ())),
                                 preferred_element_type=F32)
            y_off = _dot(cg_b, st_old.astype(BF16)) * eacs_x[:, g * rp:(g + 1) * rp]
            parts = []
            for r in range(SSD_HEADS // SSD_GROUPS):
                h = g * (SSD_HEADS // SSD_GROUPS) + r
                seg = acs[:, h:h + 1] - acsT[h:h + 1, :]
                dec = jnp.exp(jnp.where(causal, seg, neg_inf))
                mat = (cb * dec * dtdT[h:h + 1, :]).astype(BF16)
                parts.append(_dot(mat, xs_b[:, h * SSD_HEADDIM:(h + 1) * SSD_HEADDIM]))
            y_groups.append(jnp.concatenate(parts, axis=1) + y_off)
        st_ref[g] = st_old * cdec_x[:, g * rp:(g + 1) * rp] + s_new

    if mode == "state":
        @pl.when(step == pl.num_programs(0) - 1)
        def _():
            hfin_ref[...] = st_ref[...]
    elif mode == "fwd":
        for g in range(SSD_GROUPS):
            y_ref[:, g * rp:(g + 1) * rp] = y_groups[g]
    else:
        for g in range(SSD_GROUPS):
            sl = slice(g * rp, (g + 1) * rp)
            y = yf_ref[:, sl] + y_groups[g] + dx_ref[:, sl] * xs[:, sl]
            yz = y * _silu(z_ref[:, sl])
            yz = yz * lax.rsqrt(jnp.mean(yz * yz, axis=-1, keepdims=True) + EPS)
            y_ref[:, sl] = (yz * ng_ref[:, sl]).astype(y_ref.dtype)


def _ssd(xbc, dt, dtT, alog, alogT, h0, *, reverse, mode, yf=None, z=None, dx=None, ng=None):
    l = xbc.shape[0]
    q = SSD_CHUNK
    nc = l // q
    di = SSD_D_INNER
    nb = di // SSD_GN
    if reverse:
        cidx = lambda c: nc - 1 - c
    else:
        cidx = lambda c: c
    st_shape = (SSD_GROUPS, SSD_STATE, di // SSD_GROUPS)
    full3 = pl.BlockSpec(st_shape, lambda c: (0, 0, 0))
    in_specs = [pl.BlockSpec((q, di), lambda c: (cidx(c), 0)),
                pl.BlockSpec((q, SSD_GN), lambda c: (cidx(c), nb)),
                pl.BlockSpec((q, SSD_GN), lambda c: (cidx(c), nb + 1)),
                pl.BlockSpec((q, 2 * SSD_HEADS), lambda c: (cidx(c), 0)),
                pl.BlockSpec((2 * SSD_HEADS, q), lambda c: (0, cidx(c))),
                pl.BlockSpec((2, SSD_HEADS), lambda c: (0, 0)),
                pl.BlockSpec((SSD_HEADS, 2), lambda c: (0, 0)),
                full3]
    args = [xbc, xbc, xbc, dt, dtT, alog, alogT, h0]
    wide = pl.BlockSpec((q, di), lambda c: (cidx(c), 0))
    rowspec = pl.BlockSpec((1, di), lambda c: (0, 0))
    if mode == "state":
        out_shape = jax.ShapeDtypeStruct(st_shape, F32)
        out_specs = full3
    elif mode == "fwd":
        out_shape = jax.ShapeDtypeStruct((l, di), F32)
        out_specs = wide
    else:
        in_specs += [wide, wide, rowspec, rowspec]
        args += [yf, z, dx, ng]
        out_shape = jax.ShapeDtypeStruct((l, di), BF16)
        out_specs = wide
    return pl.pallas_call(
        functools.partial(_ssd_kernel, reverse=reverse, mode=mode),
        out_shape=out_shape,
        grid=(nc,),
        in_specs=in_specs,
        out_specs=out_specs,
        scratch_shapes=[pltpu.VMEM(st_shape, F32)],
        compiler_params=_cparams(("arbitrary",)),
        name="ssd_%s_%s" % (mode, "rev" if reverse else "fwd"),
    )(*args)


def _dft_tables(l):
    n = 2 * l
    hh = l // DFT_B
    ka = jnp.arange(hh, dtype=jnp.int32)
    n1 = jnp.arange(hh, dtype=jnp.int32)
    n2 = jnp.arange(DFT_B, dtype=jnp.int32)
    tt = DFT_B * n1[None, None, :] + n2[:, None, None]
    ph = ((2 * ka[None, :, None] + 1) * tt) % (2 * n)
    ang = ph.astype(F32) * (math.pi / n)
    cs, sn = jnp.cos(ang), jnp.sin(ang)
    g1 = jnp.concatenate([cs, -sn], axis=1).astype(BF16)
    scale = 2.0 / n
    h2 = jnp.concatenate([jnp.swapaxes(cs, 1, 2), -jnp.swapaxes(sn, 1, 2)], axis=2) * scale
    kb = jnp.arange(DFT_B, dtype=jnp.int32)
    ph2 = (kb[:, None] * n2[None, :]) % DFT_B
    ang2 = ph2.astype(F32) * (2.0 * math.pi / DFT_B)
    fr, fi = jnp.cos(ang2), -jnp.sin(ang2)
    f2 = jnp.concatenate([jnp.concatenate([fr, -fi], axis=1),
                          jnp.concatenate([fi, fr], axis=1)], axis=0).astype(BF16)
    f2i = jnp.concatenate([jnp.concatenate([fr, fi], axis=1),
                           jnp.concatenate([-fi, fr], axis=1)], axis=0).astype(BF16)
    return g1, f2, f2i, h2.astype(BF16)


def _filter_mlp_kernel(bands_ref, w1_ref, b1_ref, f1_ref, w2_ref, b2_ref, f2_ref, o_ref, *, l):
    tm = o_ref.shape[0]
    base = pl.program_id(0) * tm
    idx = (lax.broadcasted_iota(jnp.int32, (tm, LANES), 0) + base).astype(F32)
    lane = lax.broadcasted_iota(jnp.int32, (tm, LANES), 1)
    t = idx * (1.0 / (l - 1))
    w = idx * (2.0 * math.pi / l)
    arg = bands_ref[...] * w
    feats = jnp.where(lane == 0, t,
                      jnp.where(lane <= HY_BANDS, jnp.cos(arg),
                                jnp.where(lane < HY_EMB, -jnp.sin(arg), 0.0)))
    h = jnp.sin(f1_ref[...] * (_dot_hi(feats, w1_ref[...]) + b1_ref[...]))
    h = jnp.sin(f2_ref[...] * (_dot_hi(h, w2_ref[...]) + b2_ref[...]))
    o_ref[...] = h.astype(o_ref.dtype)


def _filter_mlp(l, w1, b1, f1, w2, b2, f2, tm=1024):
    tm = min(tm, l)
    bands = jnp.linspace(1e-4, HY_BANDS - 1, HY_BANDS, dtype=F32)
    bands_row = jnp.zeros((1, LANES), F32).at[0, 1:1 + HY_BANDS].set(bands)
    bands_row = bands_row.at[0, 1 + HY_BANDS:HY_EMB].set(bands)
    w1p = jnp.zeros((LANES, HY_HIDDEN), F32).at[:HY_EMB].set(w1)
    full = lambda a: pl.BlockSpec(a.shape, lambda i: (0,) * a.ndim)
    args = [bands_row, w1p, b1.reshape(1, -1), f1.reshape(1, -1), w2, b2.reshape(1, -1),
            f2.reshape(1, -1)]
    return pl.pallas_call(
        functools.partial(_filter_mlp_kernel, l=l),
        out_shape=jax.ShapeDtypeStruct((l, HY_HIDDEN), BF16),
        grid=(l // tm,),
        in_specs=[full(a) for a in args],
        out_specs=pl.BlockSpec((tm, HY_HIDDEN), lambda i: (i, 0)),
        compiler_params=_cparams(("parallel",)),
        name="hyena_filter_mlp",
    )(*args)


def _fwd_stage1(src_ref, g1_ref, pr_ref, pi_ref, hh):
    def body(n2, carry):
        rows = src_ref[pl.ds(n2, hh, stride=DFT_B), :]
        p = _dot(g1_ref[n2], rows.astype(BF16))
        pr_ref[pl.ds(n2, hh, stride=PITCH), :] = p[:hh]
        pi_ref[pl.ds(n2, hh, stride=PITCH), :] = p[hh:]
        return carry
    lax.fori_loop(0, DFT_B, body, 0)


def _filter_spec_kernel(hid_ref, wf_ref, wb_ref, dl_ref, g1_ref, f2_ref, kr_ref, ki_ref,
                        s_ref, d_ref, pr_ref, pi_ref, *, l):
    hh = l // DFT_B
    t = lax.broadcasted_iota(jnp.int32, (l, LANES), 0).astype(F32) * (1.0 / (l - 1))
    dec = jnp.exp(-t * dl_ref[...])
    hid = hid_ref[...]
    hf = _dot(hid, wf_ref[...].astype(BF16)) * dec
    hb = _dot(hid, wb_ref[...].astype(BF16)) * dec
    first = lax.broadcasted_iota(jnp.int32, (l, LANES), 0) == 0
    hb = jnp.where(first, 0.0, hb)
    inv = 1.0 / (jnp.sum(jnp.abs(hf), axis=0, keepdims=True)
                 + jnp.sum(jnp.abs(hb), axis=0, keepdims=True))
    s_ref[...] = hf + hb
    d_ref[...] = hf - hb
    f2 = f2_ref[...]
    for src_ref, out_ref, lo in ((s_ref, kr_ref, 0), (d_ref, ki_ref, DFT_B)):
        _fwd_stage1(src_ref, g1_ref, pr_ref, pi_ref, hh)

        def body(ka, carry, out_ref=out_ref, lo=lo):
            off = pl.multiple_of(ka * PITCH, 8)
            pp = jnp.concatenate([pr_ref[pl.ds(off, DFT_B), :], pi_ref[pl.ds(off, DFT_B), :]],
                                 axis=0).astype(BF16)
            x = _dot(f2[lo:lo + DFT_B, :], pp)
            out_ref[pl.ds(pl.multiple_of(ka * DFT_B, DFT_B), DFT_B), :] = (x * inv).astype(
                out_ref.dtype)
            return carry
        lax.fori_loop(0, hh, body, 0)


def _filter_spec(hid, w3, deltas, g1, f2, l, c):
    hh = l // DFT_B
    nct = c // LANES
    orders = w3.shape[1] // (2 * c)
    spec = pl.BlockSpec((None, l, LANES), lambda o, j: (o, 0, j))
    out_sd = jax.ShapeDtypeStruct((orders, l, c), BF16)
    return pl.pallas_call(
        functools.partial(_filter_spec_kernel, l=l),
        out_shape=(out_sd, out_sd),
        grid=(orders, nct),
        in_specs=[pl.BlockSpec((l, HY_HIDDEN), lambda o, j: (0, 0)),
                  pl.BlockSpec((HY_HIDDEN, LANES), lambda o, j: (0, (2 * o) * nct + j)),
                  pl.BlockSpec((HY_HIDDEN, LANES), lambda o, j: (0, (2 * o + 1) * nct + j)),
                  pl.BlockSpec((1, LANES), lambda o, j: (0, j)),
                  pl.BlockSpec(g1.shape, lambda o, j: (0, 0, 0)),
                  pl.BlockSpec(f2.shape, lambda o, j: (0, 0))],
        out_specs=(spec, spec),
        scratch_shapes=[pltpu.VMEM((l, LANES), F32), pltpu.VMEM((l, LANES), F32),
                        pltpu.VMEM((hh * PITCH, LANES), F32), pltpu.VMEM((hh * PITCH, LANES), F32)],
        compiler_params=_cparams(("parallel", "parallel")),
        name="hyena_filter_spec",
    )(hid, w3, w3, deltas, g1, f2)


def _long_conv_kernel(u_ref, m_ref, kr_ref, ki_ref, bias_ref, g1_ref, f2_ref, f2i_ref, h2_ref,
                      o_ref, pr_ref, pi_ref, *, l):
    hh = l // DFT_B
    _fwd_stage1(u_ref, g1_ref, pr_ref, pi_ref, hh)
    f2 = f2_ref[...]
    f2i = f2i_ref[...]

    def mid(ka, carry):
        off = pl.multiple_of(ka * PITCH, 8)
        koff = pl.multiple_of(ka * DFT_B, DFT_B)
        pp = jnp.concatenate([pr_ref[pl.ds(off, DFT_B), :], pi_ref[pl.ds(off, DFT_B), :]],
                             axis=0).astype(BF16)
        x = _dot(f2, pp)
        xr, xi = x[:DFT_B], x[DFT_B:]
        kr = kr_ref[pl.ds(koff, DFT_B), :].astype(F32)
        ki = ki_ref[pl.ds(koff, DFT_B), :].astype(F32)
        yy = jnp.concatenate([xr * kr - xi * ki, xr * ki + xi * kr], axis=0).astype(BF16)
        qq = _dot(f2i, yy)
        pr_ref[pl.ds(off, DFT_B), :] = qq[:DFT_B]
        pi_ref[pl.ds(off, DFT_B), :] = qq[DFT_B:]
        return carry
    lax.fori_loop(0, hh, mid, 0)

    bias = bias_ref[...]

    def last(n2, carry):
        qq = jnp.concatenate([pr_ref[pl.ds(n2, hh, stride=PITCH), :],
                              pi_ref[pl.ds(n2, hh, stride=PITCH), :]], axis=0).astype(BF16)
        y = _dot(h2_ref[n2], qq)
        u = u_ref[pl.ds(n2, hh, stride=DFT_B), :]
        m = m_ref[pl.ds(n2, hh, stride=DFT_B), :]
        o_ref[pl.ds(n2, hh, stride=DFT_B), :] = (m * (y + bias * u)).astype(o_ref.dtype)
        return carry
    lax.fori_loop(0, DFT_B, last, 0)


def _long_conv(u_arr, u_blk, m_arr, m_blk, kr, ki, order, bias, tables, l, c):
    g1, f2, f2i, h2 = tables
    hh = l // DFT_B
    nct = c // LANES
    kspec = pl.BlockSpec((None, l, LANES), lambda j: (order, 0, j))
    return pl.pallas_call(
        functools.partial(_long_conv_kernel, l=l),
        out_shape=jax.ShapeDtypeStruct((l, c), F32),
        grid=(nct,),
        in_specs=[pl.BlockSpec((l, LANES), lambda j: (0, u_blk * nct + j)),
                  pl.BlockSpec((l, LANES), lambda j: (0, m_blk * nct + j)),
                  kspec, kspec,
                  pl.BlockSpec((1, LANES), lambda j: (0, j)),
                  pl.BlockSpec(g1.shape, lambda j: (0, 0, 0)),
                  pl.BlockSpec(f2.shape, lambda j: (0, 0)),
                  pl.BlockSpec(f2i.shape, lambda j: (0, 0)),
                  pl.BlockSpec(h2.shape, lambda j: (0, 0, 0))],
        out_specs=pl.BlockSpec((l, LANES), lambda j: (0, j)),
        scratch_shapes=[pltpu.VMEM((hh * PITCH, LANES), F32), pltpu.VMEM((hh * PITCH, LANES), F32)],
        compiler_params=_cparams(("parallel",)),
        name="hyena_long_conv",
    )(u_arr, m_arr, kr, ki, bias.reshape(1, c), g1, f2, f2i, h2)


def _merge_kernel(hx_ref, ys_ref, yh_ref, wg1_ref, wg2_ref, w1_ref, w2_ref, gb1_ref, gb2_ref, o_ref):
    hx = hx_ref[...]
    g1 = _sigmoid(_dot(hx, wg1_ref[...]) + gb1_ref[...])
    g2 = _sigmoid(_dot(hx, wg2_ref[...]) + gb2_ref[...])
    out = g1 * _dot(ys_ref[...], w1_ref[...]) + g2 * _dot(yh_ref[...], w2_ref[...])
    o_ref[...] = out.astype(o_ref.dtype)


def _merge(hx, ys, yh, wg, w1, w2, gate_b, tm=512, tn=512):
    m, d = hx.shape
    nt = d // tn
    a_spec = pl.BlockSpec((tm, d), lambda i, j: (i, 0))
    w_spec = pl.BlockSpec((d, tn), lambda i, j: (0, j))
    gb = gate_b.reshape(1, 2 * d)
    return pl.pallas_call(
        _merge_kernel,
        out_shape=jax.ShapeDtypeStruct((m, d), BF16),
        grid=(m // tm, nt),
        in_specs=[a_spec, a_spec, a_spec,
                  w_spec, pl.BlockSpec((d, tn), lambda i, j: (0, nt + j)), w_spec, w_spec,
                  pl.BlockSpec((1, tn), lambda i, j: (0, j)),
                  pl.BlockSpec((1, tn), lambda i, j: (0, nt + j))],
        out_specs=pl.BlockSpec((tm, tn), lambda i, j: (i, j)),
        compiler_params=_cparams(("parallel", "parallel")),
        name="merge",
    )(hx, ys, yh, wg, wg, w1, w2, gb, gb)


def _oproj_kernel(a_ref, w_ref, x_ref, gt_ref, o_ref):
    o_ref[...] = x_ref[...] + gt_ref[...] * _dot(a_ref[...], w_ref[...])


def _oproj(a, w, x, gt, tm=1024, tn=512):
    m, d = a.shape
    n = w.shape[1]
    return pl.pallas_call(
        _oproj_kernel,
        out_shape=jax.ShapeDtypeStruct((m, n), F32),
        grid=(m // tm, n // tn),
        in_specs=[pl.BlockSpec((tm, d), lambda i, j: (i, 0)),
                  pl.BlockSpec((d, tn), lambda i, j: (0, j)),
                  pl.BlockSpec((tm, tn), lambda i, j: (i, j)),
                  pl.BlockSpec((1, tn), lambda i, j: (0, j))],
        out_specs=pl.BlockSpec((tm, tn), lambda i, j: (i, j)),
        compiler_params=_cparams(("parallel", "parallel")),
        name="oproj",
    )(a, w, x, gt)


def _router_kernel(x_ref, g_ref, sh_ref, sc_ref, wr_ref, br_ref, h_ref, cmb_ref):
    x = x_ref[...]
    y = x * lax.rsqrt(jnp.mean(x * x, axis=-1, keepdims=True) + EPS) * g_ref[...]
    h = y * (1.0 + sc_ref[...]) + sh_ref[...]
    h_ref[...] = h.astype(h_ref.dtype)
    logits = _dot_hi(h, wr_ref[...]) + br_ref[...]
    tm = x.shape[0]
    lane = lax.broadcasted_iota(jnp.int32, (tm, LANES), 1)
    neg = jnp.float32(-jnp.inf)
    big = jnp.int32(LANES)
    is_grp = (lane >= MOE_EXPERTS) & (lane < MOE_EXPERTS + MOE_GROUPS)
    gl = jnp.where(is_grp, logits, neg)
    gmax = jnp.max(gl, axis=-1, keepdims=True)
    gidx = jnp.min(jnp.where(gl == gmax, lane, big), axis=-1, keepdims=True) - MOE_EXPERTS
    gw = 1.0 / jnp.sum(jnp.where(is_grp, jnp.exp(logits - gmax), 0.0), axis=-1, keepdims=True)
    in_grp = (lane < MOE_EXPERTS) & ((lane // MOE_PER_GROUP) == gidx)
    el = jnp.where(in_grp, logits, neg)
    m1 = jnp.max(el, axis=-1, keepdims=True)
    i1 = jnp.min(jnp.where(el == m1, lane, big), axis=-1, keepdims=True)
    el2 = jnp.where(lane == i1, neg, el)
    m2 = jnp.max(el2, axis=-1, keepdims=True)
    i2 = jnp.min(jnp.where(el2 == m2, lane, big), axis=-1, keepdims=True)
    e21 = jnp.exp(m2 - m1)
    w1 = gw / (1.0 + e21)
    w2 = gw * e21 / (1.0 + e21)
    cmb_ref[...] = jnp.where(lane == i1, w1, jnp.where(lane == i2, w2, 0.0))


def _router(x1, g, sh, sc, w_rg, b_rg, w_re, b_re, tm=512):
    m, d = x1.shape
    wr = jnp.zeros((d, LANES), F32).at[:, :MOE_EXPERTS].set(w_re)
    wr = wr.at[:, MOE_EXPERTS:MOE_EXPERTS + MOE_GROUPS].set(w_rg)
    br = jnp.zeros((1, LANES), F32).at[0, :MOE_EXPERTS].set(b_re)
    br = br.at[0, MOE_EXPERTS:MOE_EXPERTS + MOE_GROUPS].set(b_rg)
    row = pl.BlockSpec((1, d), lambda i: (0, 0))
    return pl.pallas_call(
        _router_kernel,
        out_shape=(jax.ShapeDtypeStruct((m, d), BF16), jax.ShapeDtypeStruct((m, LANES), F32)),
        grid=(m // tm,),
        in_specs=[pl.BlockSpec((tm, d), lambda i: (i, 0)), row, row, row,
                  pl.BlockSpec((d, LANES), lambda i: (0, 0)),
                  pl.BlockSpec((1, LANES), lambda i: (0, 0))],
        out_specs=(pl.BlockSpec((tm, d), lambda i: (i, 0)),
                   pl.BlockSpec((tm, LANES), lambda i: (i, 0))),
        compiler_params=_cparams(("parallel",)),
        name="router",
    )(x1, g.reshape(1, d), sh, sc, wr, br)


def _moe_kernel(h_ref, cmb_ref, wg_ref, wu_ref, wd_ref, x_ref, gt_ref, fg_ref, o_ref, acc_ref):
    e = pl.program_id(1)

    @pl.when(e == 0)
    def _():
        acc_ref[...] = jnp.zeros_like(acc_ref)

    h = h_ref[...]
    lane = lax.broadcasted_iota(jnp.int32, cmb_ref.shape, 1)
    cw = jnp.sum(jnp.where(lane == e, cmb_ref[...], 0.0), axis=-1, keepdims=True)
    hid = _silu(_dot(h, wg_ref[...])) * _dot(h, wu_ref[...]) * cw
    acc_ref[...] += _dot(hid.astype(BF16), wd_ref[...])

    @pl.when(e == pl.num_programs(1) - 1)
    def _():
        x2 = x_ref[...] + gt_ref[...] * acc_ref[...]
        o_ref[...] = (x2 * lax.rsqrt(jnp.mean(x2 * x2, axis=-1, keepdims=True) + EPS)
                      * fg_ref[...])


def _moe(h, cmb, wg, wu, wd, x1, gt, fg, tm=512):
    m, d = h.shape
    ne, _, f = wg.shape
    row = pl.BlockSpec((1, d), lambda i, e: (0, 0))
    return pl.pallas_call(
        _moe_kernel,
        out_shape=jax.ShapeDtypeStruct((m, d), F32),
        grid=(m // tm, ne),
        in_specs=[pl.BlockSpec((tm, d), lambda i, e: (i, 0)),
                  pl.BlockSpec((tm, LANES), lambda i, e: (i, 0)),
                  pl.BlockSpec((None, d, f), lambda i, e: (e, 0, 0)),
                  pl.BlockSpec((None, d, f), lambda i, e: (e, 0, 0)),
                  pl.BlockSpec((None, f, d), lambda i, e: (e, 0, 0)),
                  pl.BlockSpec((tm, d), lambda i, e: (i, 0)),
                  row, row],
        out_specs=pl.BlockSpec((tm, d), lambda i, e: (i, 0)),
        scratch_shapes=[pltpu.VMEM((tm, d), F32)],
        compiler_params=_cparams(("parallel", "arbitrary")),
        name="moe",
    )(h, cmb, wg, wu, wd, x1, gt, fg.reshape(1, d))


def kernel(x, c, ctx, c_ctx, ada_w, ada_b, norm1_g, w_in, ssd_conv_w, ssd_conv_b, ssd_dt_bias,
           ssd_a_log, ssd_d, ssd_norm_g, w_ssd_out, hy_conv_w, hy_conv_b, hy_f_w1, hy_f_b1,
           hy_f_freq1, hy_f_w2, hy_f_b2, hy_f_freq2, hy_f_w3, hy_bias, w_hy_out, gate_b, w_o,
           norm2_g, moe_w_rg, moe_b_rg, moe_w_re, moe_b_re, moe_w_gate, moe_w_up, moe_w_down,
           final_g):
    bsz, l, d = x.shape
    assert bsz == 1 and ada_w.shape[0] == 1
    lc = ctx.shape[1]
    di = SSD_D_INNER
    xbc_cols = di + 2 * SSD_GN
    col_xbc = di
    col_dt = col_xbc + xbc_cols
    col_hy = col_dt + 2 * SSD_HEADS
    hyw = hy_bias.shape[-1]
    col_gate = col_hy + 3 * hyw

    x2d = x[0]
    ctx2d = ctx[0]

    cc = jnp.zeros((8, d), F32).at[0].set(c[0]).at[1].set(c_ctx)
    mods = _adaln(cc, ada_w[0], ada_b[0])
    sh1, sc1, gt1, sh2, sc2, gt2 = [mods[0:1, i * d:(i + 1) * d] for i in range(6)]
    csh1, csc1 = mods[1:2, 0:d], mods[1:2, d:2 * d]

    hx = _normmod(x2d, norm1_g[0], sh1, sc1, tm=512)
    hc = _normmod(ctx2d, norm1_g[0], csh1, csc1, tm=lc)

    w_in_b = w_in[0].astype(BF16)
    w_z = w_in_b[:, 0:col_xbc]
    w_xbc = w_in_b[:, col_xbc:col_dt]
    w_dt = w_in_b[:, col_dt:col_hy]
    w_hy = w_in_b[:, col_hy:col_gate]
    w_gate = w_in_b[:, col_gate:]
    dt_bias = ssd_dt_bias[0].reshape(-1)

    xbc_c = _proj(hc, w_xbc, "conv_silu", tm=lc, tn=512, conv_w=ssd_conv_w[0], bias=ssd_conv_b[0],
                  group=lc)
    dt_c = _proj(hc, w_dt, "softplus", tm=lc, tn=2 * SSD_HEADS, bias=dt_bias)
    alog = ssd_a_log[0]
    alogT = alog.T
    zero_state = jnp.zeros((SSD_GROUPS, SSD_STATE, di // SSD_GROUPS), F32)
    h0f = _ssd(xbc_c, dt_c, dt_c.T, alog, alogT, zero_state, reverse=False, mode="state")
    h0b = _ssd(xbc_c, dt_c, dt_c.T, alog, alogT, zero_state, reverse=True, mode="state")

    z = _proj(hx, w_z, "none", tm=1024, tn=512)
    xbc = _proj(hx, w_xbc, "conv_silu", tm=1024, tn=512, conv_w=ssd_conv_w[0], bias=ssd_conv_b[0])
    dt = _proj(hx, w_dt, "softplus", tm=1024, tn=2 * SSD_HEADS, bias=dt_bias)
    u3 = _proj(hx, w_hy, "conv", tm=1024, tn=512, conv_w=hy_conv_w[0], bias=hy_conv_b[0])

    dtT = dt.T
    yf = _ssd(xbc, dt, dtT, alog, alogT, h0f, reverse=False, mode="fwd")
    dx = jnp.repeat(ssd_d[0], SSD_HEADDIM).reshape(1, di)
    y_ssd = _ssd(xbc, dt, dtT, alog, alogT, h0b, reverse=True, mode="bwd", yf=yf, z=z, dx=dx,
                 ng=ssd_norm_g[0].reshape(1, di))

    tables = _dft_tables(l)
    hid = _filter_mlp(l, hy_f_w1[0], hy_f_b1[0], hy_f_freq1[0], hy_f_w2[0], hy_f_b2[0],
                      hy_f_freq2[0])
    max_decay = math.log(1e-2) / 0.3
    min_decay = math.log(1e-2) / 1.5
    deltas = jnp.abs(jnp.linspace(min_decay, max_decay, hyw, dtype=F32)).reshape(1, hyw)
    kr, ki = _filter_spec(hid, hy_f_w3[0], deltas, tables[0], tables[1], l, hyw)
    z2 = _long_conv(u3, 0, u3, 1, kr, ki, 0, hy_bias[0, 0], tables, l, hyw)
    y_hy = _long_conv(z2, 0, u3, 2, kr, ki, 1, hy_bias[0, 1], tables, l, hyw)

    merged = _merge(hx, y_ssd, y_hy.astype(BF16), w_gate, w_ssd_out[0].astype(BF16),
                    w_hy_out[0].astype(BF16), gate_b[0])
    x1 = _oproj(merged, w_o[0].astype(BF16), x2d, gt1)

    h2, cmb = _router(x1, norm2_g[0], sh2, sc2, moe_w_rg[0], moe_b_rg[0], moe_w_re[0], moe_b_re[0])
    out = _moe(h2, cmb, moe_w_gate[0].astype(BF16), moe_w_up[0].astype(BF16),
               moe_w_down[0].astype(BF16), x1, gt2, final_g)
    return out[None]
```

```python
import functools
import math

import jax
import jax.numpy as jnp
from jax import lax
from jax.experimental import pallas as pl
from jax.experimental.pallas import tpu as pltpu

F32 = jnp.float32
BF16 = jnp.bfloat16

EPS = 1e-6
GRID_W = 64
SSD_HEADS = 32
SSD_HEADDIM = 64
SSD_GROUPS = 4
SSD_STATE = 128
SSD_CHUNK = 128
SSD_D_INNER = SSD_HEADS * SSD_HEADDIM
SSD_GN = SSD_GROUPS * SSD_STATE
HY_BANDS = 16
HY_EMB = 33
HY_HIDDEN = 64
MOE_GROUPS = 4
MOE_PER_GROUP = 4
MOE_EXPERTS = 16

LANES = 128
DFT_B = 128
PITCH = 136
VMEM_LIMIT = 56 * 1024 * 1024
UNROLL_OUTER = 8
UNROLL_MID = 4


def _cparams(sem):
    return pltpu.CompilerParams(dimension_semantics=sem, vmem_limit_bytes=VMEM_LIMIT)


def _dot(a, b):
    return jnp.dot(a, b, preferred_element_type=F32)


def _sigmoid(x):
    return 1.0 / (1.0 + jnp.exp(-x))


def _silu(x):
    return x * _sigmoid(x)


def _split3(v):
    b1 = v.astype(BF16)
    r1 = v - b1.astype(F32)
    b2 = r1.astype(BF16)
    b3 = (r1 - b2.astype(F32)).astype(BF16)
    return b1, b2, b3


def _dot_sel_r(v, sel_b):
    b1, b2, b3 = _split3(v)
    return _dot(b1, sel_b) + _dot(b2, sel_b) + _dot(b3, sel_b)


def _dot_sel_l(sel_b, v):
    b1, b2, b3 = _split3(v)
    return _dot(sel_b, b1) + _dot(sel_b, b2) + _dot(sel_b, b3)


def _dot_hi(a, b):
    a1, a2, a3 = _split3(a)
    b1, b2, b3 = _split3(b)
    return (_dot(a1, b1) + (_dot(a1, b2) + _dot(a2, b1))
            + (_dot(a1, b3) + _dot(a2, b2) + _dot(a3, b1)))


def _adaln_kernel(c_ref, w_ref, b_ref, o_ref):
    s = _silu(c_ref[...]).astype(BF16)
    o_ref[...] = _dot(s, w_ref[...].astype(BF16)) + b_ref[...]


def _adaln(cc, w, b, tn=1024):
    m, d = cc.shape
    n = w.shape[1]
    return pl.pallas_call(
        _adaln_kernel,
        out_shape=jax.ShapeDtypeStruct((m, n), F32),
        grid=(n // tn,),
        in_specs=[pl.BlockSpec((m, d), lambda j: (0, 0)),
                  pl.BlockSpec((d, tn), lambda j: (0, j)),
                  pl.BlockSpec((1, tn), lambda j: (0, j))],
        out_specs=pl.BlockSpec((m, tn), lambda j: (0, j)),
        compiler_params=_cparams(("parallel",)),
        name="adaln",
    )(cc, w, b.reshape(1, n))


def _normmod_kernel(x_ref, g_ref, sh_ref, sc_ref, o_ref):
    x = x_ref[...]
    y = x * lax.rsqrt(jnp.mean(x * x, axis=-1, keepdims=True) + EPS) * g_ref[...]
    o_ref[...] = (y * (1.0 + sc_ref[...]) + sh_ref[...]).astype(o_ref.dtype)


def _normmod(x, g, sh, sc, tm):
    m, d = x.shape
    row = pl.BlockSpec((1, d), lambda i: (0, 0))
    return pl.pallas_call(
        _normmod_kernel,
        out_shape=jax.ShapeDtypeStruct((m, d), BF16),
        grid=(m // tm,),
        in_specs=[pl.BlockSpec((tm, d), lambda i: (i, 0)), row, row, row],
        out_specs=pl.BlockSpec((tm, d), lambda i: (i, 0)),
        compiler_params=_cparams(("parallel",)),
        name="normmod",
    )(x, g.reshape(1, d), sh, sc)


def _conv_rows(acc, cw_ref, group):
    tm, tn = acc.shape
    taps = cw_ref.shape[0]
    half = taps // 2
    pos = lax.broadcasted_iota(jnp.int32, (tm, tn), 0) & (group - 1)
    out = acc * cw_ref[half:half + 1, :]
    for k in range(taps):
        d = k - half
        if d == 0:
            continue
        shifted = pltpu.roll(acc, (tm - d) % tm, 0)
        valid = (pos + d >= 0) & (pos + d < group)
        out = out + jnp.where(valid, shifted, 0.0) * cw_ref[k:k + 1, :]
    return out


def _proj_kernel(a_ref, w_ref, *rest, epi, group):
    o_ref = rest[-1]
    acc = _dot(a_ref[...], w_ref[...])
    if epi == "none":
        out = acc
    elif epi == "conv":
        cw_ref, b_ref = rest[0], rest[1]
        out = _conv_rows(acc, cw_ref, group) + b_ref[...]
    elif epi == "conv_silu":
        cw_ref, b_ref = rest[0], rest[1]
        out = _silu(_conv_rows(acc, cw_ref, group) + b_ref[...])
    elif epi == "softplus":
        v = acc + rest[0][...]
        out = jnp.maximum(v, 0.0) + jnp.log(1.0 + jnp.exp(-jnp.abs(v)))
    else:
        raise ValueError(epi)
    if len(o_ref.shape) == 3:
        nb, _, tn = o_ref.shape
        o_ref[:, 0:DFT_B, :] = out.reshape(nb, DFT_B, tn).astype(o_ref.dtype)
        o_ref[:, DFT_B:PITCH, :] = jnp.zeros((nb, PITCH - DFT_B, tn), o_ref.dtype)
    else:
        o_ref[...] = out.astype(o_ref.dtype)


def _proj(a, w, epi, tm, tn, conv_w=None, bias=None, group=GRID_W, out_dtype=F32, pitched=False):
    m, k = a.shape
    n = w.shape[1]
    tn = min(tn, n)
    if pitched:
        out_shape = jax.ShapeDtypeStruct((m // DFT_B, PITCH, n), out_dtype)
        out_spec = pl.BlockSpec((tm // DFT_B, PITCH, tn), lambda i, j: (i, 0, j))
    else:
        out_shape = jax.ShapeDtypeStruct((m, n), out_dtype)
        out_spec = pl.BlockSpec((tm, tn), lambda i, j: (i, j))
    extra, extra_specs = [], []
    if conv_w is not None:
        extra.append(conv_w)
        extra_specs.append(pl.BlockSpec((conv_w.shape[0], tn), lambda i, j: (0, j)))
    if bias is not None:
        extra.append(bias.reshape(1, n))
        extra_specs.append(pl.BlockSpec((1, tn), lambda i, j: (0, j)))
    return pl.pallas_call(
        functools.partial(_proj_kernel, epi=epi, group=group),
        out_shape=out_shape,
        grid=(m // tm, n // tn),
        in_specs=[pl.BlockSpec((tm, k), lambda i, j: (i, 0)),
                  pl.BlockSpec((k, tn), lambda i, j: (0, j))] + extra_specs,
        out_specs=out_spec,
        compiler_params=_cparams(("parallel", "parallel")),
        name="proj_" + epi,
    )(a, w, *extra)


def _ssd_kernel(xs_ref, b_ref, c_ref, dt_ref, dtT_ref, alog_ref, alogT_ref, h0_ref, *rest,
                reverse, mode):
    if mode == "state":
        hfin_ref, st_ref = rest
    elif mode == "fwd":
        y_ref, st_ref = rest
    else:
        yf_ref, z_ref, dx_ref, ng_ref, y_ref, st_ref = rest
    q = xs_ref.shape[0]
    nh = SSD_HEADS
    rp = SSD_D_INNER // SSD_GROUPS
    step = pl.program_id(0)

    @pl.when(step == 0)
    def _():
        st_ref[...] = h0_ref[...]

    d = 1 if reverse else 0
    dtd = dt_ref[:, d * nh:(d + 1) * nh]
    dtdT = dtT_ref[d * nh:(d + 1) * nh, :]
    a_row = -jnp.exp(alog_ref[d:d + 1, :])
    a_col = -jnp.exp(alogT_ref[:, d:d + 1])
    ad = dtd * a_row
    adT = dtdT * a_col
    row = lax.broadcasted_iota(jnp.int32, (q, q), 0)
    col = lax.broadcasted_iota(jnp.int32, (q, q), 1)
    lower = col <= row
    upper = col >= row
    causal = upper if reverse else lower
    tri = jnp.where(causal, 1.0, 0.0).astype(BF16)
    triT = jnp.where(lower if reverse else upper, 1.0, 0.0).astype(BF16)
    acs = _dot_sel_l(tri, ad)
    acsT = _dot_sel_r(adT, triT)
    tot = jnp.sum(ad, axis=0, keepdims=True)
    dte = jnp.exp(tot - acs)
    eacs = jnp.exp(acs)
    cdec = jnp.exp(tot)
    hsel = lax.broadcasted_iota(jnp.int32, (nh, SSD_D_INNER), 0)
    lsel = lax.broadcasted_iota(jnp.int32, (nh, SSD_D_INNER), 1) // SSD_HEADDIM
    expand = jnp.where(hsel == lsel, 1.0, 0.0).astype(BF16)
    stacked = jnp.concatenate([dtd * dte, eacs, jnp.broadcast_to(cdec, (8, nh))], axis=0)
    exp_all = _dot_sel_r(stacked, expand)
    w_x = exp_all[0:q]
    eacs_x = exp_all[q:2 * q]
    cdec_x = exp_all[2 * q:2 * q + 1]

    xs = xs_ref[...]
    xs_b = xs.astype(BF16)
    xdw = (xs * w_x).astype(BF16)
    neg_inf = jnp.float32(-jnp.inf)
    y_groups = []
    for g in range(SSD_GROUPS):
        bg = b_ref[:, g * SSD_STATE:(g + 1) * SSD_STATE]
        st_old = st_ref[g]
        s_new = _dot(bg.T.astype(BF16), xdw[:, g * rp:(g + 1) * rp])
        if mode != "state":
            bg_b = bg.astype(BF16)
            cg_b = c_ref[:, g * SSD_STATE:(g + 1) * SSD_STATE].astype(BF16)
            cb = lax.dot_general(cg_b, bg_b, (((1,), (1,)), ((), ())),
                                 preferred_element_type=F32)
            y_off = _dot(cg_b, st_old.astype(BF16)) * eacs_x[:, g * rp:(g + 1) * rp]
            parts = []
            for r in range(SSD_HEADS // SSD_GROUPS):
                h = g * (SSD_HEADS // SSD_GROUPS) + r
                seg = acs[:, h:h + 1] - acsT[h:h + 1, :]
                dec = jnp.exp(jnp.where(causal, seg, neg_inf))
                mat = (cb * dec * dtdT[h:h + 1, :]).astype(BF16)
                parts.append(_dot(mat, xs_b[:, h * SSD_HEADDIM:(h + 1) * SSD_HEADDIM]))
            y_groups.append(jnp.concatenate(parts, axis=1) + y_off)
        st_ref[g] = st_old * cdec_x[:, g * rp:(g + 1) * rp] + s_new

    if mode == "state":
        @pl.when(step == pl.num_programs(0) - 1)
        def _():
            hfin_ref[...] = st_ref[...]
    elif mode == "fwd":
        for g in range(SSD_GROUPS):
            y_ref[:, g * rp:(g + 1) * rp] = y_groups[g]
    else:
        for g in range(SSD_GROUPS):
            sl = slice(g * rp, (g + 1) * rp)
            y = yf_ref[:, sl] + y_groups[g] + dx_ref[:, sl] * xs[:, sl]
            yz = y * _silu(z_ref[:, sl])
            yz = yz * lax.rsqrt(jnp.mean(yz * yz, axis=-1, keepdims=True) + EPS)
            y_ref[:, sl] = (yz * ng_ref[:, sl]).astype(y_ref.dtype)


def _ssd(xbc, dt, dtT, alog, alogT, h0, *, reverse, mode, yf=None, z=None, dx=None, ng=None):
    l = xbc.shape[0]
    q = SSD_CHUNK
    nc = l // q
    di = SSD_D_INNER
    nb = di // SSD_GN
    if reverse:
        cidx = lambda c: nc - 1 - c
    else:
        cidx = lambda c: c
    st_shape = (SSD_GROUPS, SSD_STATE, di // SSD_GROUPS)
    full3 = pl.BlockSpec(st_shape, lambda c: (0, 0, 0))
    in_specs = [pl.BlockSpec((q, di), lambda c: (cidx(c), 0)),
                pl.BlockSpec((q, SSD_GN), lambda c: (cidx(c), nb)),
                pl.BlockSpec((q, SSD_GN), lambda c: (cidx(c), nb + 1)),
                pl.BlockSpec((q, 2 * SSD_HEADS), lambda c: (cidx(c), 0)),
                pl.BlockSpec((2 * SSD_HEADS, q), lambda c: (0, cidx(c))),
                pl.BlockSpec((2, SSD_HEADS), lambda c: (0, 0)),
                pl.BlockSpec((SSD_HEADS, 2), lambda c: (0, 0)),
                full3]
    args = [xbc, xbc, xbc, dt, dtT, alog, alogT, h0]
    wide = pl.BlockSpec((q, di), lambda c: (cidx(c), 0))
    rowspec = pl.BlockSpec((1, di), lambda c: (0, 0))
    if mode == "state":
        out_shape = jax.ShapeDtypeStruct(st_shape, F32)
        out_specs = full3
    elif mode == "fwd":
        out_shape = jax.ShapeDtypeStruct((l, di), F32)
        out_specs = wide
    else:
        in_specs += [wide, wide, rowspec, rowspec]
        args += [yf, z, dx, ng]
        out_shape = jax.ShapeDtypeStruct((l, di), BF16)
        out_specs = wide
    return pl.pallas_call(
        functools.partial(_ssd_kernel, reverse=reverse, mode=mode),
        out_shape=out_shape,
        grid=(nc,),
        in_specs=in_specs,
        out_specs=out_specs,
        scratch_shapes=[pltpu.VMEM(st_shape, F32)],
        compiler_params=_cparams(("arbitrary",)),
        name="ssd_%s_%s" % (mode, "rev" if reverse else "fwd"),
    )(*args)


def _dft_tables(l):
    n = 2 * l
    hh = l // DFT_B
    ka = jnp.arange(hh, dtype=jnp.int32)
    n1 = jnp.arange(hh, dtype=jnp.int32)
    n2 = jnp.arange(DFT_B, dtype=jnp.int32)
    tt = DFT_B * n1[None, None, :] + n2[:, None, None]
    ph = ((2 * ka[None, :, None] + 1) * tt) % (2 * n)
    ang = ph.astype(F32) * (math.pi / n)
    cs, sn = jnp.cos(ang), jnp.sin(ang)
    g1 = jnp.concatenate([cs, -sn], axis=1).astype(BF16)
    scale = 2.0 / n
    h2 = jnp.concatenate([jnp.swapaxes(cs, 1, 2), -jnp.swapaxes(sn, 1, 2)], axis=2) * scale
    kb = jnp.arange(DFT_B, dtype=jnp.int32)
    ph2 = (kb[:, None] * n2[None, :]) % DFT_B
    ang2 = ph2.astype(F32) * (2.0 * math.pi / DFT_B)
    fr, fi = jnp.cos(ang2), -jnp.sin(ang2)
    f2 = jnp.concatenate([jnp.concatenate([fr, -fi], axis=1),
                          jnp.concatenate([fi, fr], axis=1)], axis=0).astype(BF16)
    f2i = jnp.concatenate([jnp.concatenate([fr, fi], axis=1),
                           jnp.concatenate([-fi, fr], axis=1)], axis=0).astype(BF16)
    return g1, f2, f2i, h2.astype(BF16)


def _filter_mlp_kernel(bands_ref, w1_ref, b1_ref, f1_ref, w2_ref, b2_ref, f2_ref, o_ref, *, l):
    tm = o_ref.shape[0]
    base = pl.program_id(0) * tm
    idx = (lax.broadcasted_iota(jnp.int32, (tm, LANES), 0) + base).astype(F32)
    lane = lax.broadcasted_iota(jnp.int32, (tm, LANES), 1)
    t = idx * (1.0 / (l - 1))
    w = idx * (2.0 * math.pi / l)
    arg = bands_ref[...] * w
    feats = jnp.where(lane == 0, t,
                      jnp.where(lane <= HY_BANDS, jnp.cos(arg),
                                jnp.where(lane < HY_EMB, -jnp.sin(arg), 0.0)))
    h = jnp.sin(f1_ref[...] * (_dot_hi(feats, w1_ref[...]) + b1_ref[...]))
    h = jnp.sin(f2_ref[...] * (_dot_hi(h, w2_ref[...]) + b2_ref[...]))
    o_ref[...] = h.astype(o_ref.dtype)


def _filter_mlp(l, w1, b1, f1, w2, b2, f2, tm=1024):
    tm = min(tm, l)
    bands = jnp.linspace(1e-4, HY_BANDS - 1, HY_BANDS, dtype=F32)
    bands_row = jnp.zeros((1, LANES), F32).at[0, 1:1 + HY_BANDS].set(bands)
    bands_row = bands_row.at[0, 1 + HY_BANDS:HY_EMB].set(bands)
    w1p = jnp.zeros((LANES, HY_HIDDEN), F32).at[:HY_EMB].set(w1)
    full = lambda a: pl.BlockSpec(a.shape, lambda i: (0,) * a.ndim)
    args = [bands_row, w1p, b1.reshape(1, -1), f1.reshape(1, -1), w2, b2.reshape(1, -1),
            f2.reshape(1, -1)]
    return pl.pallas_call(
        functools.partial(_filter_mlp_kernel, l=l),
        out_shape=jax.ShapeDtypeStruct((l, HY_HIDDEN), BF16),
        grid=(l // tm,),
        in_specs=[full(a) for a in args],
        out_specs=pl.BlockSpec((tm, HY_HIDDEN), lambda i: (i, 0)),
        compiler_params=_cparams(("parallel",)),
        name="hyena_filter_mlp",
    )(*args)


def _pq_pitch(hh):
    return 2 * hh + 8


def _fwd_stage1(src_ref, g1_ref, pq_ref, hh):
    pqp = _pq_pitch(hh)

    def body(n2, carry):
        rows = src_ref[pl.ds(n2, hh, stride=PITCH), :]
        p = _dot(g1_ref[n2], rows.astype(BF16))
        pq_ref[pl.ds(pl.multiple_of(n2 * pqp, 8), 2 * hh), :] = p
        return carry
    lax.fori_loop(0, DFT_B, body, 0, unroll=UNROLL_OUTER)


def _load_spectrum_rows(pq_ref, ka, hh):
    pqp = _pq_pitch(hh)
    return jnp.concatenate([pq_ref[pl.ds(ka, DFT_B, stride=pqp), :],
                            pq_ref[pl.ds(hh + ka, DFT_B, stride=pqp), :]], axis=0)


def _filter_spec_kernel(hid_ref, wf_ref, wb_ref, dl_ref, g1_ref, f2_ref, kr_ref, ki_ref,
                        s_ref, d_ref, pq_ref, *, l):
    hh = l // DFT_B
    t = lax.broadcasted_iota(jnp.int32, (l, LANES), 0).astype(F32) * (1.0 / (l - 1))
    dec = jnp.exp(-t * dl_ref[...])
    hid = hid_ref[...]
    hf = _dot(hid, wf_ref[...].astype(BF16)) * dec
    hb = _dot(hid, wb_ref[...].astype(BF16)) * dec
    first = lax.broadcasted_iota(jnp.int32, (l, LANES), 0) == 0
    hb = jnp.where(first, 0.0, hb)
    inv = 1.0 / (jnp.sum(jnp.abs(hf), axis=0, keepdims=True)
                 + jnp.sum(jnp.abs(hb), axis=0, keepdims=True))
    hs = hf + hb
    hd = hf - hb
    for n1 in range(hh):
        s_ref[n1 * PITCH:n1 * PITCH + DFT_B, :] = hs[n1 * DFT_B:(n1 + 1) * DFT_B]
        d_ref[n1 * PITCH:n1 * PITCH + DFT_B, :] = hd[n1 * DFT_B:(n1 + 1) * DFT_B]
    f2 = f2_ref[...]
    for src_ref, out_ref, lo in ((s_ref, kr_ref, 0), (d_ref, ki_ref, DFT_B)):
        _fwd_stage1(src_ref, g1_ref, pq_ref, hh)

        def body(ka, carry, out_ref=out_ref, lo=lo):
            pp = _load_spectrum_rows(pq_ref, ka, hh).astype(BF16)
            x = _dot(f2[lo:lo + DFT_B, :], pp)
            out_ref[pl.ds(pl.multiple_of(ka * DFT_B, DFT_B), DFT_B), :] = (x * inv).astype(
                out_ref.dtype)
            return carry
        lax.fori_loop(0, hh, body, 0, unroll=UNROLL_MID)


def _filter_spec(hid, w3, deltas, g1, f2, l, c):
    hh = l // DFT_B
    nct = c // LANES
    orders = w3.shape[1] // (2 * c)
    spec = pl.BlockSpec((None, l, LANES), lambda o, j: (o, 0, j))
    out_sd = jax.ShapeDtypeStruct((orders, l, c), BF16)
    return pl.pallas_call(
        functools.partial(_filter_spec_kernel, l=l),
        out_shape=(out_sd, out_sd),
        grid=(orders, nct),
        in_specs=[pl.BlockSpec((l, HY_HIDDEN), lambda o, j: (0, 0)),
                  pl.BlockSpec((HY_HIDDEN, LANES), lambda o, j: (0, (2 * o) * nct + j)),
                  pl.BlockSpec((HY_HIDDEN, LANES), lambda o, j: (0, (2 * o + 1) * nct + j)),
                  pl.BlockSpec((1, LANES), lambda o, j: (0, j)),
                  pl.BlockSpec(g1.shape, lambda o, j: (0, 0, 0)),
                  pl.BlockSpec(f2.shape, lambda o, j: (0, 0))],
        out_specs=(spec, spec),
        scratch_shapes=[pltpu.VMEM((hh * PITCH, LANES), F32)] * 2
        + [pltpu.VMEM((DFT_B * _pq_pitch(hh), LANES), F32)],
        compiler_params=_cparams(("parallel", "parallel")),
        name="hyena_filter_spec",
    )(hid, w3, w3, deltas, g1, f2)


def _long_conv_kernel(u_ref, m_ref, kr_ref, ki_ref, bias_ref, g1_ref, f2_ref, f2i_ref, h2_ref,
                      o_ref, pq_ref, *, l):
    hh = l // DFT_B
    pqp = _pq_pitch(hh)
    _fwd_stage1(u_ref, g1_ref, pq_ref, hh)
    f2 = f2_ref[...]
    f2i = f2i_ref[...]

    def mid(ka, carry):
        koff = pl.multiple_of(ka * DFT_B, DFT_B)
        x = _dot(f2, _load_spectrum_rows(pq_ref, ka, hh).astype(BF16))
        xr, xi = x[:DFT_B], x[DFT_B:]
        kr = kr_ref[pl.ds(koff, DFT_B), :].astype(F32)
        ki = ki_ref[pl.ds(koff, DFT_B), :].astype(F32)
        yy = jnp.concatenate([xr * kr - xi * ki, xr * ki + xi * kr], axis=0).astype(BF16)
        qq = _dot(f2i, yy)
        pq_ref[pl.ds(ka, DFT_B, stride=pqp), :] = qq[:DFT_B]
        pq_ref[pl.ds(hh + ka, DFT_B, stride=pqp), :] = qq[DFT_B:]
        return carry
    lax.fori_loop(0, hh, mid, 0, unroll=UNROLL_MID)

    bias = bias_ref[...]

    def last(n2, carry):
        qq = pq_ref[pl.ds(pl.multiple_of(n2 * pqp, 8), 2 * hh), :].astype(BF16)
        y = _dot(h2_ref[n2], qq)
        u = u_ref[pl.ds(n2, hh, stride=PITCH), :]
        m = m_ref[pl.ds(n2, hh, stride=PITCH), :]
        o_ref[pl.ds(n2, hh, stride=PITCH), :] = (m * (y + bias * u)).astype(o_ref.dtype)
        return carry
    lax.fori_loop(0, DFT_B, last, 0, unroll=UNROLL_OUTER)
    for n1 in range(hh):
        o_ref[n1 * PITCH + DFT_B:(n1 + 1) * PITCH, :] = jnp.zeros((PITCH - DFT_B, LANES), o_ref.dtype)


def _long_conv(u_arr, u_blk, m_arr, m_blk, kr, ki, order, bias, tables, l, c):
    g1, f2, f2i, h2 = tables
    hh = l // DFT_B
    nct = c // LANES
    kspec = pl.BlockSpec((None, l, LANES), lambda j: (order, 0, j))
    return pl.pallas_call(
        functools.partial(_long_conv_kernel, l=l),
        out_shape=jax.ShapeDtypeStruct((hh * PITCH, c), F32),
        grid=(nct,),
        in_specs=[pl.BlockSpec((hh * PITCH, LANES), lambda j: (0, u_blk * nct + j)),
                  pl.BlockSpec((hh * PITCH, LANES), lambda j: (0, m_blk * nct + j)),
                  kspec, kspec,
                  pl.BlockSpec((1, LANES), lambda j: (0, j)),
                  pl.BlockSpec(g1.shape, lambda j: (0, 0, 0)),
                  pl.BlockSpec(f2.shape, lambda j: (0, 0)),
                  pl.BlockSpec(f2i.shape, lambda j: (0, 0)),
                  pl.BlockSpec(h2.shape, lambda j: (0, 0, 0))],
        out_specs=pl.BlockSpec((hh * PITCH, LANES), lambda j: (0, j)),
        scratch_shapes=[pltpu.VMEM((DFT_B * _pq_pitch(hh), LANES), F32)],
        compiler_params=_cparams(("parallel",)),
        name="hyena_long_conv",
    )(u_arr, m_arr, kr, ki, bias.reshape(1, c), g1, f2, f2i, h2)


def _merge_kernel(hx_ref, ys_ref, yh_ref, wg1_ref, wg2_ref, w1_ref, w2_ref, gb1_ref, gb2_ref, o_ref):
    hx = hx_ref[...]
    g1 = _sigmoid(_dot(hx, wg1_ref[...]) + gb1_ref[...])
    g2 = _sigmoid(_dot(hx, wg2_ref[...]) + gb2_ref[...])
    yh = yh_ref[:, 0:DFT_B, :].reshape(hx.shape).astype(BF16)
    out = g1 * _dot(ys_ref[...], w1_ref[...]) + g2 * _dot(yh, w2_ref[...])
    o_ref[...] = out.astype(o_ref.dtype)


def _merge(hx, ys, yh, wg, w1, w2, gate_b, tm=512, tn=512):
    m, d = hx.shape
    nt = d // tn
    a_spec = pl.BlockSpec((tm, d), lambda i, j: (i, 0))
    w_spec = pl.BlockSpec((d, tn), lambda i, j: (0, j))
    gb = gate_b.reshape(1, 2 * d)
    return pl.pallas_call(
        _merge_kernel,
        out_shape=jax.ShapeDtypeStruct((m, d), BF16),
        grid=(m // tm, nt),
        in_specs=[a_spec, a_spec,
                  pl.BlockSpec((tm // DFT_B, PITCH, d), lambda i, j: (i, 0, 0)),
                  w_spec, pl.BlockSpec((d, tn), lambda i, j: (0, nt + j)), w_spec, w_spec,
                  pl.BlockSpec((1, tn), lambda i, j: (0, j)),
                  pl.BlockSpec((1, tn), lambda i, j: (0, nt + j))],
        out_specs=pl.BlockSpec((tm, tn), lambda i, j: (i, j)),
        compiler_params=_cparams(("parallel", "parallel")),
        name="merge",
    )(hx, ys, yh, wg, wg, w1, w2, gb, gb)


def _oproj_kernel(a_ref, w_ref, x_ref, gt_ref, o_ref):
    o_ref[...] = x_ref[...] + gt_ref[...] * _dot(a_ref[...], w_ref[...])


def _oproj(a, w, x, gt, tm=1024, tn=512):
    m, d = a.shape
    n = w.shape[1]
    return pl.pallas_call(
        _oproj_kernel,
        out_shape=jax.ShapeDtypeStruct((m, n), F32),
        grid=(m // tm, n // tn),
        in_specs=[pl.BlockSpec((tm, d), lambda i, j: (i, 0)),
                  pl.BlockSpec((d, tn), lambda i, j: (0, j)),
                  pl.BlockSpec((tm, tn), lambda i, j: (i, j)),
                  pl.BlockSpec((1, tn), lambda i, j: (0, j))],
        out_specs=pl.BlockSpec((tm, tn), lambda i, j: (i, j)),
        compiler_params=_cparams(("parallel", "parallel")),
        name="oproj",
    )(a, w, x, gt)


def _router_kernel(x_ref, g_ref, sh_ref, sc_ref, wr_ref, br_ref, h_ref, cmb_ref):
    x = x_ref[...]
    y = x * lax.rsqrt(jnp.mean(x * x, axis=-1, keepdims=True) + EPS) * g_ref[...]
    h = y * (1.0 + sc_ref[...]) + sh_ref[...]
    h_ref[...] = h.astype(h_ref.dtype)
    logits = _dot_hi(h, wr_ref[...]) + br_ref[...]
    tm = x.shape[0]
    lane = lax.broadcasted_iota(jnp.int32, (tm, LANES), 1)
    neg = jnp.float32(-jnp.inf)
    big = jnp.int32(LANES)
    is_grp = (lane >= MOE_EXPERTS) & (lane < MOE_EXPERTS + MOE_GROUPS)
    gl = jnp.where(is_grp, logits, neg)
    gmax = jnp.max(gl, axis=-1, keepdims=True)
    gidx = jnp.min(jnp.where(gl == gmax, lane, big), axis=-1, keepdims=True) - MOE_EXPERTS
    gw = 1.0 / jnp.sum(jnp.where(is_grp, jnp.exp(logits - gmax), 0.0), axis=-1, keepdims=True)
    in_grp = (lane < MOE_EXPERTS) & ((lane // MOE_PER_GROUP) == gidx)
    el = jnp.where(in_grp, logits, neg)
    m1 = jnp.max(el, axis=-1, keepdims=True)
    i1 = jnp.min(jnp.where(el == m1, lane, big), axis=-1, keepdims=True)
    el2 = jnp.where(lane == i1, neg, el)
    m2 = jnp.max(el2, axis=-1, keepdims=True)
    i2 = jnp.min(jnp.where(el2 == m2, lane, big), axis=-1, keepdims=True)
    e21 = jnp.exp(m2 - m1)
    w1 = gw / (1.0 + e21)
    w2 = gw * e21 / (1.0 + e21)
    cmb_ref[...] = jnp.where(lane == i1, w1, jnp.where(lane == i2, w2, 0.0))


def _router(x1, g, sh, sc, w_rg, b_rg, w_re, b_re, tm=512):
    m, d = x1.shape
    wr = jnp.zeros((d, LANES), F32).at[:, :MOE_EXPERTS].set(w_re)
    wr = wr.at[:, MOE_EXPERTS:MOE_EXPERTS + MOE_GROUPS].set(w_rg)
    br = jnp.zeros((1, LANES), F32).at[0, :MOE_EXPERTS].set(b_re)
    br = br.at[0, MOE_EXPERTS:MOE_EXPERTS + MOE_GROUPS].set(b_rg)
    row = pl.BlockSpec((1, d), lambda i: (0, 0))
    return pl.pallas_call(
        _router_kernel,
        out_shape=(jax.ShapeDtypeStruct((m, d), BF16), jax.ShapeDtypeStruct((m, LANES), F32)),
        grid=(m // tm,),
        in_specs=[pl.BlockSpec((tm, d), lambda i: (i, 0)), row, row, row,
                  pl.BlockSpec((d, LANES), lambda i: (0, 0)),
                  pl.BlockSpec((1, LANES), lambda i: (0, 0))],
        out_specs=(pl.BlockSpec((tm, d), lambda i: (i, 0)),
                   pl.BlockSpec((tm, LANES), lambda i: (i, 0))),
        compiler_params=_cparams(("parallel",)),
        name="router",
    )(x1, g.reshape(1, d), sh, sc, wr, br)


def _moe_kernel(h_ref, cmb_ref, wg_ref, wu_ref, wd_ref, x_ref, gt_ref, fg_ref, o_ref, acc_ref):
    e = pl.program_id(1)

    @pl.when(e == 0)
    def _():
        acc_ref[...] = jnp.zeros_like(acc_ref)

    h = h_ref[...]
    lane = lax.broadcasted_iota(jnp.int32, cmb_ref.shape, 1)
    cw = jnp.sum(jnp.where(lane == e, cmb_ref[...], 0.0), axis=-1, keepdims=True)
    hid = _silu(_dot(h, wg_ref[...])) * _dot(h, wu_ref[...]) * cw
    acc_ref[...] += _dot(hid.astype(BF16), wd_ref[...])

    @pl.when(e == pl.num_programs(1) - 1)
    def _():
        x2 = x_ref[...] + gt_ref[...] * acc_ref[...]
        o_ref[...] = (x2 * lax.rsqrt(jnp.mean(x2 * x2, axis=-1, keepdims=True) + EPS)
                      * fg_ref[...])


def _moe(h, cmb, wg, wu, wd, x1, gt, fg, tm=512):
    m, d = h.shape
    ne, _, f = wg.shape
    row = pl.BlockSpec((1, d), lambda i, e: (0, 0))
    return pl.pallas_call(
        _moe_kernel,
        out_shape=jax.ShapeDtypeStruct((m, d), F32),
        grid=(m // tm, ne),
        in_specs=[pl.BlockSpec((tm, d), lambda i, e: (i, 0)),
                  pl.BlockSpec((tm, LANES), lambda i, e: (i, 0)),
                  pl.BlockSpec((None, d, f), lambda i, e: (e, 0, 0)),
                  pl.BlockSpec((None, d, f), lambda i, e: (e, 0, 0)),
                  pl.BlockSpec((None, f, d), lambda i, e: (e, 0, 0)),
                  pl.BlockSpec((tm, d), lambda i, e: (i, 0)),
                  row, row],
        out_specs=pl.BlockSpec((tm, d), lambda i, e: (i, 0)),
        scratch_shapes=[pltpu.VMEM((tm, d), F32)],
        compiler_params=_cparams(("parallel", "arbitrary")),
        name="moe",
    )(h, cmb, wg, wu, wd, x1, gt, fg.reshape(1, d))


def kernel(x, c, ctx, c_ctx, ada_w, ada_b, norm1_g, w_in, ssd_conv_w, ssd_conv_b, ssd_dt_bias,
           ssd_a_log, ssd_d, ssd_norm_g, w_ssd_out, hy_conv_w, hy_conv_b, hy_f_w1, hy_f_b1,
           hy_f_freq1, hy_f_w2, hy_f_b2, hy_f_freq2, hy_f_w3, hy_bias, w_hy_out, gate_b, w_o,
           norm2_g, moe_w_rg, moe_b_rg, moe_w_re, moe_b_re, moe_w_gate, moe_w_up, moe_w_down,
           final_g):
    bsz, l, d = x.shape
    assert bsz == 1 and ada_w.shape[0] == 1
    lc = ctx.shape[1]
    di = SSD_D_INNER
    xbc_cols = di + 2 * SSD_GN
    col_xbc = di
    col_dt = col_xbc + xbc_cols
    col_hy = col_dt + 2 * SSD_HEADS
    hyw = hy_bias.shape[-1]
    col_gate = col_hy + 3 * hyw

    x2d = x[0]
    ctx2d = ctx[0]

    cc = jnp.zeros((8, d), F32).at[0].set(c[0]).at[1].set(c_ctx)
    mods = _adaln(cc, ada_w[0], ada_b[0])
    sh1, sc1, gt1, sh2, sc2, gt2 = [mods[0:1, i * d:(i + 1) * d] for i in range(6)]
    csh1, csc1 = mods[1:2, 0:d], mods[1:2, d:2 * d]

    hx = _normmod(x2d, norm1_g[0], sh1, sc1, tm=512)
    hc = _normmod(ctx2d, norm1_g[0], csh1, csc1, tm=lc)

    w_in_b = w_in[0].astype(BF16)
    w_z = w_in_b[:, 0:col_xbc]
    w_xbc = w_in_b[:, col_xbc:col_dt]
    w_dt = w_in_b[:, col_dt:col_hy]
    w_hy = w_in_b[:, col_hy:col_gate]
    w_gate = w_in_b[:, col_gate:]
    dt_bias = ssd_dt_bias[0].reshape(-1)

    xbc_c = _proj(hc, w_xbc, "conv_silu", tm=lc, tn=512, conv_w=ssd_conv_w[0], bias=ssd_conv_b[0],
                  group=lc)
    dt_c = _proj(hc, w_dt, "softplus", tm=lc, tn=2 * SSD_HEADS, bias=dt_bias)
    alog = ssd_a_log[0]
    alogT = alog.T
    zero_state = jnp.zeros((SSD_GROUPS, SSD_STATE, di // SSD_GROUPS), F32)
    h0f = _ssd(xbc_c, dt_c, dt_c.T, alog, alogT, zero_state, reverse=False, mode="state")
    h0b = _ssd(xbc_c, dt_c, dt_c.T, alog, alogT, zero_state, reverse=True, mode="state")

    z = _proj(hx, w_z, "none", tm=1024, tn=512)
    xbc = _proj(hx, w_xbc, "conv_silu", tm=1024, tn=512, conv_w=ssd_conv_w[0], bias=ssd_conv_b[0])
    dt = _proj(hx, w_dt, "softplus", tm=1024, tn=2 * SSD_HEADS, bias=dt_bias)
    u3 = _proj(hx, w_hy, "conv", tm=1024, tn=512, conv_w=hy_conv_w[0], bias=hy_conv_b[0],
               pitched=True).reshape(l // DFT_B * PITCH, 3 * hyw)

    dtT = dt.T
    yf = _ssd(xbc, dt, dtT, alog, alogT, h0f, reverse=False, mode="fwd")
    dx = jnp.repeat(ssd_d[0], SSD_HEADDIM).reshape(1, di)
    y_ssd = _ssd(xbc, dt, dtT, alog, alogT, h0b, reverse=True, mode="bwd", yf=yf, z=z, dx=dx,
                 ng=ssd_norm_g[0].reshape(1, di))

    tables = _dft_tables(l)
    hid = _filter_mlp(l, hy_f_w1[0], hy_f_b1[0], hy_f_freq1[0], hy_f_w2[0], hy_f_b2[0],
                      hy_f_freq2[0])
    max_decay = math.log(1e-2) / 0.3
    min_decay = math.log(1e-2) / 1.5
    deltas = jnp.abs(jnp.linspace(min_decay, max_decay, hyw, dtype=F32)).reshape(1, hyw)
    kr, ki = _filter_spec(hid, hy_f_w3[0], deltas, tables[0], tables[1], l, hyw)
    z2 = _long_conv(u3, 0, u3, 1, kr, ki, 0, hy_bias[0, 0], tables, l, hyw)
    y_hy = _long_conv(z2, 0, u3, 2, kr, ki, 1, hy_bias[0, 1], tables, l, hyw)

    merged = _merge(hx, y_ssd, y_hy.reshape(l // DFT_B, PITCH, hyw), w_gate, w_ssd_out[0].astype(BF16),
                    w_hy_out[0].astype(BF16), gate_b[0])
    x1 = _oproj(merged, w_o[0].astype(BF16), x2d, gt1)

    h2, cmb = _router(x1, norm2_g[0], sh2, sc2, moe_w_rg[0], moe_b_rg[0], moe_w_re[0], moe_b_re[0])
    out = _moe(h2, cmb, moe_w_gate[0].astype(BF16), moe_w_up[0].astype(BF16),
               moe_w_down[0].astype(BF16), x1, gt2, final_g)
    return out[None]
```

```python
import functools
import math

import jax
import jax.numpy as jnp
from jax import lax
from jax.experimental import pallas as pl
from jax.experimental.pallas import tpu as pltpu

F32 = jnp.float32
BF16 = jnp.bfloat16

EPS = 1e-6
GRID_W = 64
SSD_HEADS = 32
SSD_HEADDIM = 64
SSD_GROUPS = 4
SSD_STATE = 128
SSD_CHUNK = 128
SSD_D_INNER = SSD_HEADS * SSD_HEADDIM
SSD_GN = SSD_GROUPS * SSD_STATE
HY_BANDS = 16
HY_EMB = 33
HY_HIDDEN = 64
MOE_GROUPS = 4
MOE_PER_GROUP = 4
MOE_EXPERTS = 16

LANES = 128
DFT_B = 128
PITCH = 136
VMEM_LIMIT = 56 * 1024 * 1024
UNROLL_OUTER = 16
UNROLL_MID = 8
PROJ_SUB_ROWS = 256
MOE_TILE = 256


def _cparams(sem):
    return pltpu.CompilerParams(dimension_semantics=sem, vmem_limit_bytes=VMEM_LIMIT)


def _dot(a, b):
    return jnp.dot(a, b, preferred_element_type=F32)


def _sigmoid(x):
    return 1.0 / (1.0 + jnp.exp(-x))


def _silu(x):
    return x * _sigmoid(x)


def _split3(v):
    b1 = v.astype(BF16)
    r1 = v - b1.astype(F32)
    b2 = r1.astype(BF16)
    b3 = (r1 - b2.astype(F32)).astype(BF16)
    return b1, b2, b3


def _dot_sel_r(v, sel_b):
    b1, b2, b3 = _split3(v)
    return _dot(b1, sel_b) + _dot(b2, sel_b) + _dot(b3, sel_b)


def _dot_sel_l(sel_b, v):
    b1, b2, b3 = _split3(v)
    return _dot(sel_b, b1) + _dot(sel_b, b2) + _dot(sel_b, b3)


def _dot_hi(a, b):
    a1, a2, a3 = _split3(a)
    b1, b2, b3 = _split3(b)
    return (_dot(a1, b1) + (_dot(a1, b2) + _dot(a2, b1))
            + (_dot(a1, b3) + _dot(a2, b2) + _dot(a3, b1)))


def _adaln_kernel(c_ref, w_ref, b_ref, o_ref):
    s = _silu(c_ref[...]).astype(BF16)
    o_ref[...] = _dot(s, w_ref[...].astype(BF16)) + b_ref[...]


def _adaln(cc, w, b, tn=1024):
    m, d = cc.shape
    n = w.shape[1]
    return pl.pallas_call(
        _adaln_kernel,
        out_shape=jax.ShapeDtypeStruct((m, n), F32),
        grid=(n // tn,),
        in_specs=[pl.BlockSpec((m, d), lambda j: (0, 0)),
                  pl.BlockSpec((d, tn), lambda j: (0, j)),
                  pl.BlockSpec((1, tn), lambda j: (0, j))],
        out_specs=pl.BlockSpec((m, tn), lambda j: (0, j)),
        compiler_params=_cparams(("parallel",)),
        name="adaln",
    )(cc, w, b.reshape(1, n))


def _normmod_kernel(x_ref, g_ref, sh_ref, sc_ref, o_ref):
    x = x_ref[...]
    y = x * lax.rsqrt(jnp.mean(x * x, axis=-1, keepdims=True) + EPS) * g_ref[...]
    o_ref[...] = (y * (1.0 + sc_ref[...]) + sh_ref[...]).astype(o_ref.dtype)


def _normmod(x, g, sh, sc, tm):
    m, d = x.shape
    row = pl.BlockSpec((1, d), lambda i: (0, 0))
    return pl.pallas_call(
        _normmod_kernel,
        out_shape=jax.ShapeDtypeStruct((m, d), BF16),
        grid=(m // tm,),
        in_specs=[pl.BlockSpec((tm, d), lambda i: (i, 0)), row, row, row],
        out_specs=pl.BlockSpec((tm, d), lambda i: (i, 0)),
        compiler_params=_cparams(("parallel",)),
        name="normmod",
    )(x, g.reshape(1, d), sh, sc)


def _conv_rows(acc, cw_ref, group):
    tm, tn = acc.shape
    taps = cw_ref.shape[0]
    half = taps // 2
    pos = lax.broadcasted_iota(jnp.int32, (tm, tn), 0) & (group - 1)
    out = acc * cw_ref[half:half + 1, :]
    for k in range(taps):
        d = k - half
        if d == 0:
            continue
        shifted = pltpu.roll(acc, (tm - d) % tm, 0)
        valid = (pos + d >= 0) & (pos + d < group)
        out = out + jnp.where(valid, shifted, 0.0) * cw_ref[k:k + 1, :]
    return out


def _proj_kernel(a_ref, w_ref, *rest, epi, group):
    o_ref = rest[-1]
    tm = a_ref.shape[0]
    sub = max(group, min(tm, PROJ_SUB_ROWS))
    for s in range(tm // sub):
        acc = _dot(a_ref[s * sub:(s + 1) * sub, :], w_ref[...])
        if epi == "none":
            out = acc
        elif epi == "conv":
            cw_ref, b_ref = rest[0], rest[1]
            out = _conv_rows(acc, cw_ref, group) + b_ref[...]
        elif epi == "conv_silu":
            cw_ref, b_ref = rest[0], rest[1]
            out = _silu(_conv_rows(acc, cw_ref, group) + b_ref[...])
        elif epi == "softplus":
            v = acc + rest[0][...]
            out = jnp.maximum(v, 0.0) + jnp.log(1.0 + jnp.exp(-jnp.abs(v)))
        else:
            raise ValueError(epi)
        if len(o_ref.shape) == 3:
            nb, tn = sub // DFT_B, o_ref.shape[2]
            o_ref[s * nb:(s + 1) * nb, 0:DFT_B, :] = out.reshape(nb, DFT_B, tn).astype(o_ref.dtype)
            o_ref[s * nb:(s + 1) * nb, DFT_B:PITCH, :] = jnp.zeros((nb, PITCH - DFT_B, tn), o_ref.dtype)
        else:
            o_ref[s * sub:(s + 1) * sub, :] = out.astype(o_ref.dtype)


def _proj(a, w, col0, n, epi, tm, tn, conv_w=None, bias=None, group=GRID_W, out_dtype=F32,
          pitched=False):
    m, k = a.shape
    tn = min(tn, n)
    assert col0 % tn == 0 and n % tn == 0
    jb = col0 // tn
    if pitched:
        out_shape = jax.ShapeDtypeStruct((m // DFT_B, PITCH, n), out_dtype)
        out_spec = pl.BlockSpec((tm // DFT_B, PITCH, tn), lambda i, j: (i, 0, j))
    else:
        out_shape = jax.ShapeDtypeStruct((m, n), out_dtype)
        out_spec = pl.BlockSpec((tm, tn), lambda i, j: (i, j))
    extra, extra_specs = [], []
    if conv_w is not None:
        extra.append(conv_w)
        extra_specs.append(pl.BlockSpec((conv_w.shape[0], tn), lambda i, j: (0, j)))
    if bias is not None:
        extra.append(bias.reshape(1, n))
        extra_specs.append(pl.BlockSpec((1, tn), lambda i, j: (0, j)))
    return pl.pallas_call(
        functools.partial(_proj_kernel, epi=epi, group=group),
        out_shape=out_shape,
        grid=(m // tm, n // tn),
        in_specs=[pl.BlockSpec((tm, k), lambda i, j: (i, 0)),
                  pl.BlockSpec((k, tn), lambda i, j: (0, jb + j))] + extra_specs,
        out_specs=out_spec,
        compiler_params=_cparams(("parallel", "parallel")),
        name="proj_" + epi,
    )(a, w, *extra)


def _ssd_kernel(xs_ref, b_ref, c_ref, dt_ref, dtT_ref, alog_ref, alogT_ref, h0_ref, *rest,
                reverse, mode):
    if mode == "state":
        hfin_ref, st_ref = rest
    elif mode == "fwd":
        y_ref, st_ref = rest
    else:
        yf_ref, z_ref, dx_ref, ng_ref, y_ref, st_ref = rest
    q = xs_ref.shape[0]
    nh = SSD_HEADS
    rp = SSD_D_INNER // SSD_GROUPS
    step = pl.program_id(0)

    @pl.when(step == 0)
    def _():
        st_ref[...] = h0_ref[...]

    d = 1 if reverse else 0
    dtd = dt_ref[:, d * nh:(d + 1) * nh]
    dtdT = dtT_ref[d * nh:(d + 1) * nh, :]
    a_row = -jnp.exp(alog_ref[d:d + 1, :])
    a_col = -jnp.exp(alogT_ref[:, d:d + 1])
    ad = dtd * a_row
    adT = dtdT * a_col
    row = lax.broadcasted_iota(jnp.int32, (q, q), 0)
    col = lax.broadcasted_iota(jnp.int32, (q, q), 1)
    lower = col <= row
    upper = col >= row
    causal = upper if reverse else lower
    tri = jnp.where(causal, 1.0, 0.0).astype(BF16)
    triT = jnp.where(lower if reverse else upper, 1.0, 0.0).astype(BF16)
    acs = _dot_sel_l(tri, ad)
    acsT = _dot_sel_r(adT, triT)
    tot = jnp.sum(ad, axis=0, keepdims=True)
    dte = jnp.exp(tot - acs)
    eacs = jnp.exp(acs)
    cdec = jnp.exp(tot)
    hsel = lax.broadcasted_iota(jnp.int32, (nh, SSD_D_INNER), 0)
    lsel = lax.broadcasted_iota(jnp.int32, (nh, SSD_D_INNER), 1) // SSD_HEADDIM
    expand = jnp.where(hsel == lsel, 1.0, 0.0).astype(BF16)
    stacked = jnp.concatenate([dtd * dte, eacs, jnp.broadcast_to(cdec, (8, nh))], axis=0)
    exp_all = _dot_sel_r(stacked, expand)
    w_x = exp_all[0:q]
    eacs_x = exp_all[q:2 * q]
    cdec_x = exp_all[2 * q:2 * q + 1]

    xs = xs_ref[...]
    xs_b = xs.astype(BF16)
    xdw = (xs * w_x).astype(BF16)
    neg_inf = jnp.float32(-jnp.inf)
    y_groups = []
    for g in range(SSD_GROUPS):
        bg = b_ref[:, g * SSD_STATE:(g + 1) * SSD_STATE]
        st_old = st_ref[g]
        s_new = _dot(bg.T.astype(BF16), xdw[:, g * rp:(g + 1) * rp])
        if mode != "state":
            bg_b = bg.astype(BF16)
            cg_b = c_ref[:, g * SSD_STATE:(g + 1) * SSD_STATE].astype(BF16)
            cb = lax.dot_general(cg_b, bg_b, (((1,), (1,)), ((), ())),
                                 preferred_element_type=F32)
            y_off = _dot(cg_b, st_old.astype(BF16)) * eacs_x[:, g * rp:(g + 1) * rp]
            parts = []
            for r in range(SSD_HEADS // SSD_GROUPS):
                h = g * (SSD_HEADS // SSD_GROUPS) + r
                seg = acs[:, h:h + 1] - acsT[h:h + 1, :]
                dec = jnp.exp(jnp.where(causal, seg, neg_inf))
                mat = (cb * dec * dtdT[h:h + 1, :]).astype(BF16)
                parts.append(_dot(mat, xs_b[:, h * SSD_HEADDIM:(h + 1) * SSD_HEADDIM]))
            y_groups.append(jnp.concatenate(parts, axis=1) + y_off)
        st_ref[g] = st_old * cdec_x[:, g * rp:(g + 1) * rp] + s_new

    if mode == "state":
        @pl.when(step == pl.num_programs(0) - 1)
        def _():
            hfin_ref[...] = st_ref[...]
    elif mode == "fwd":
        for g in range(SSD_GROUPS):
            y_ref[:, g * rp:(g + 1) * rp] = y_groups[g]
    else:
        for g in range(SSD_GROUPS):
            sl = slice(g * rp, (g + 1) * rp)
            y = yf_ref[:, sl] + y_groups[g] + dx_ref[:, sl] * xs[:, sl]
            yz = y * _silu(z_ref[:, sl])
            yz = yz * lax.rsqrt(jnp.mean(yz * yz, axis=-1, keepdims=True) + EPS)
            y_ref[:, sl] = (yz * ng_ref[:, sl]).astype(y_ref.dtype)


def _ssd(xbc, dt, dtT, alog, alogT, h0, *, reverse, mode, yf=None, z=None, dx=None, ng=None):
    l = xbc.shape[0]
    q = SSD_CHUNK
    nc = l // q
    di = SSD_D_INNER
    nb = di // SSD_GN
    if reverse:
        cidx = lambda c: nc - 1 - c
    else:
        cidx = lambda c: c
    st_shape = (SSD_GROUPS, SSD_STATE, di // SSD_GROUPS)
    full3 = pl.BlockSpec(st_shape, lambda c: (0, 0, 0))
    in_specs = [pl.BlockSpec((q, di), lambda c: (cidx(c), 0)),
                pl.BlockSpec((q, SSD_GN), lambda c: (cidx(c), nb)),
                pl.BlockSpec((q, SSD_GN), lambda c: (cidx(c), nb + 1)),
                pl.BlockSpec((q, dt.shape[1]), lambda c: (cidx(c), 0)),
                pl.BlockSpec((dtT.shape[0], q), lambda c: (0, cidx(c))),
                pl.BlockSpec((2, SSD_HEADS), lambda c: (0, 0)),
                pl.BlockSpec((SSD_HEADS, 2), lambda c: (0, 0)),
                full3]
    args = [xbc, xbc, xbc, dt, dtT, alog, alogT, h0]
    wide = pl.BlockSpec((q, di), lambda c: (cidx(c), 0))
    rowspec = pl.BlockSpec((1, di), lambda c: (0, 0))
    if mode == "state":
        out_shape = jax.ShapeDtypeStruct(st_shape, F32)
        out_specs = full3
    elif mode == "fwd":
        out_shape = jax.ShapeDtypeStruct((l, di), F32)
        out_specs = wide
    else:
        in_specs += [wide, wide, rowspec, rowspec]
        args += [yf, z, dx, ng]
        out_shape = jax.ShapeDtypeStruct((l, di), BF16)
        out_specs = wide
    return pl.pallas_call(
        functools.partial(_ssd_kernel, reverse=reverse, mode=mode),
        out_shape=out_shape,
        grid=(nc,),
        in_specs=in_specs,
        out_specs=out_specs,
        scratch_shapes=[pltpu.VMEM(st_shape, F32)],
        compiler_params=_cparams(("arbitrary",)),
        name="ssd_%s_%s" % (mode, "rev" if reverse else "fwd"),
    )(*args)


def _dft_tables(l):
    n = 2 * l
    hh = l // DFT_B
    ka = jnp.arange(hh, dtype=jnp.int32)
    n1 = jnp.arange(hh, dtype=jnp.int32)
    n2 = jnp.arange(DFT_B, dtype=jnp.int32)
    tt = DFT_B * n1[None, None, :] + n2[:, None, None]
    ph = ((2 * ka[None, :, None] + 1) * tt) % (2 * n)
    ang = ph.astype(F32) * (math.pi / n)
    cs, sn = jnp.cos(ang), jnp.sin(ang)
    g1 = jnp.concatenate([cs, -sn], axis=1).astype(BF16)
    scale = 2.0 / n
    h2 = jnp.concatenate([jnp.swapaxes(cs, 1, 2), -jnp.swapaxes(sn, 1, 2)], axis=2) * scale
    kb = jnp.arange(DFT_B, dtype=jnp.int32)
    ph2 = (kb[:, None] * n2[None, :]) % DFT_B
    ang2 = ph2.astype(F32) * (2.0 * math.pi / DFT_B)
    fr, fi = jnp.cos(ang2), -jnp.sin(ang2)
    f2 = jnp.concatenate([jnp.concatenate([fr, -fi], axis=1),
                          jnp.concatenate([fi, fr], axis=1)], axis=0).astype(BF16)
    f2i = jnp.concatenate([jnp.concatenate([fr, fi], axis=1),
                           jnp.concatenate([-fi, fr], axis=1)], axis=0).astype(BF16)
    return g1, f2, f2i, h2.astype(BF16)


def _filter_mlp_kernel(bands_ref, w1_ref, b1_ref, f1_ref, w2_ref, b2_ref, f2_ref, o_ref, *, l):
    tm = o_ref.shape[0]
    base = pl.program_id(0) * tm
    idx = (lax.broadcasted_iota(jnp.int32, (tm, LANES), 0) + base).astype(F32)
    lane = lax.broadcasted_iota(jnp.int32, (tm, LANES), 1)
    t = idx * (1.0 / (l - 1))
    w = idx * (2.0 * math.pi / l)
    arg = bands_ref[...] * w
    feats = jnp.where(lane == 0, t,
                      jnp.where(lane <= HY_BANDS, jnp.cos(arg),
                                jnp.where(lane < HY_EMB, -jnp.sin(arg), 0.0)))
    h = jnp.sin(f1_ref[...] * (_dot_hi(feats, w1_ref[...]) + b1_ref[...]))
    h = jnp.sin(f2_ref[...] * (_dot_hi(h, w2_ref[...]) + b2_ref[...]))
    o_ref[...] = h.astype(o_ref.dtype)


def _filter_mlp(l, w1, b1, f1, w2, b2, f2, tm=1024):
    tm = min(tm, l)
    bands = jnp.linspace(1e-4, HY_BANDS - 1, HY_BANDS, dtype=F32)
    bands_row = jnp.zeros((1, LANES), F32).at[0, 1:1 + HY_BANDS].set(bands)
    bands_row = bands_row.at[0, 1 + HY_BANDS:HY_EMB].set(bands)
    w1p = jnp.zeros((LANES, HY_HIDDEN), F32).at[:HY_EMB].set(w1)
    full = lambda a: pl.BlockSpec(a.shape, lambda i: (0,) * a.ndim)
    args = [bands_row, w1p, b1.reshape(1, -1), f1.reshape(1, -1), w2, b2.reshape(1, -1),
            f2.reshape(1, -1)]
    return pl.pallas_call(
        functools.partial(_filter_mlp_kernel, l=l),
        out_shape=jax.ShapeDtypeStruct((l, HY_HIDDEN), BF16),
        grid=(l // tm,),
        in_specs=[full(a) for a in args],
        out_specs=pl.BlockSpec((tm, HY_HIDDEN), lambda i: (i, 0)),
        compiler_params=_cparams(("parallel",)),
        name="hyena_filter_mlp",
    )(*args)


def _pq_pitch(hh):
    return 2 * hh + 8


def _fwd_stage1(src_ref, g1_ref, pq_ref, hh):
    pqp = _pq_pitch(hh)

    def body(n2, carry):
        rows = src_ref[pl.ds(n2, hh, stride=PITCH), :]
        p = _dot(g1_ref[n2], rows.astype(BF16))
        pq_ref[pl.ds(pl.multiple_of(n2 * pqp, 8), 2 * hh), :] = p
        return carry
    lax.fori_loop(0, DFT_B, body, 0, unroll=UNROLL_OUTER)


def _load_spectrum_rows(pq_ref, ka, hh):
    pqp = _pq_pitch(hh)
    return jnp.concatenate([pq_ref[pl.ds(ka, DFT_B, stride=pqp), :],
                            pq_ref[pl.ds(hh + ka, DFT_B, stride=pqp), :]], axis=0)


def _filter_spec_kernel(hid_ref, wf_ref, wb_ref, dl_ref, g1_ref, f2_ref, kr_ref, ki_ref,
                        s_ref, d_ref, pq_ref, *, l):
    hh = l // DFT_B
    t = lax.broadcasted_iota(jnp.int32, (l, LANES), 0).astype(F32) * (1.0 / (l - 1))
    dec = jnp.exp(-t * dl_ref[...])
    hid = hid_ref[...]
    hf = _dot(hid, wf_ref[...].astype(BF16)) * dec
    hb = _dot(hid, wb_ref[...].astype(BF16)) * dec
    first = lax.broadcasted_iota(jnp.int32, (l, LANES), 0) == 0
    hb = jnp.where(first, 0.0, hb)
    inv = 1.0 / (jnp.sum(jnp.abs(hf), axis=0, keepdims=True)
                 + jnp.sum(jnp.abs(hb), axis=0, keepdims=True))
    hs = hf + hb
    hd = hf - hb
    for n1 in range(hh):
        s_ref[n1 * PITCH:n1 * PITCH + DFT_B, :] = hs[n1 * DFT_B:(n1 + 1) * DFT_B]
        d_ref[n1 * PITCH:n1 * PITCH + DFT_B, :] = hd[n1 * DFT_B:(n1 + 1) * DFT_B]
    f2 = f2_ref[...]
    for src_ref, out_ref, lo in ((s_ref, kr_ref, 0), (d_ref, ki_ref, DFT_B)):
        _fwd_stage1(src_ref, g1_ref, pq_ref, hh)

        def body(ka, carry, out_ref=out_ref, lo=lo):
            pp = _load_spectrum_rows(pq_ref, ka, hh).astype(BF16)
            x = _dot(f2[lo:lo + DFT_B, :], pp)
            out_ref[pl.ds(pl.multiple_of(ka * DFT_B, DFT_B), DFT_B), :] = (x * inv).astype(
                out_ref.dtype)
            return carry
        lax.fori_loop(0, hh, body, 0, unroll=UNROLL_MID)


def _filter_spec(hid, w3, deltas, g1, f2, l, c):
    hh = l // DFT_B
    nct = c // LANES
    orders = w3.shape[1] // (2 * c)
    spec = pl.BlockSpec((None, l, LANES), lambda o, j: (o, 0, j))
    out_sd = jax.ShapeDtypeStruct((orders, l, c), BF16)
    return pl.pallas_call(
        functools.partial(_filter_spec_kernel, l=l),
        out_shape=(out_sd, out_sd),
        grid=(orders, nct),
        in_specs=[pl.BlockSpec((l, HY_HIDDEN), lambda o, j: (0, 0)),
                  pl.BlockSpec((HY_HIDDEN, LANES), lambda o, j: (0, (2 * o) * nct + j)),
                  pl.BlockSpec((HY_HIDDEN, LANES), lambda o, j: (0, (2 * o + 1) * nct + j)),
                  pl.BlockSpec((1, LANES), lambda o, j: (0, j)),
                  pl.BlockSpec(g1.shape, lambda o, j: (0, 0, 0)),
                  pl.BlockSpec(f2.shape, lambda o, j: (0, 0))],
        out_specs=(spec, spec),
        scratch_shapes=[pltpu.VMEM((hh * PITCH, LANES), F32)] * 2
        + [pltpu.VMEM((DFT_B * _pq_pitch(hh), LANES), F32)],
        compiler_params=_cparams(("parallel", "parallel")),
        name="hyena_filter_spec",
    )(hid, w3, w3, deltas, g1, f2)


def _long_conv_kernel(u_ref, m_ref, kr_ref, ki_ref, bias_ref, g1_ref, f2_ref, f2i_ref, h2_ref,
                      o_ref, pq_ref, *, l):
    hh = l // DFT_B
    pqp = _pq_pitch(hh)
    _fwd_stage1(u_ref, g1_ref, pq_ref, hh)
    f2 = f2_ref[...]
    f2i = f2i_ref[...]

    def mid(ka, carry):
        koff = pl.multiple_of(ka * DFT_B, DFT_B)
        x = _dot(f2, _load_spectrum_rows(pq_ref, ka, hh).astype(BF16))
        xr, xi = x[:DFT_B], x[DFT_B:]
        kr = kr_ref[pl.ds(koff, DFT_B), :].astype(F32)
        ki = ki_ref[pl.ds(koff, DFT_B), :].astype(F32)
        yy = jnp.concatenate([xr * kr - xi * ki, xr * ki + xi * kr], axis=0).astype(BF16)
        qq = _dot(f2i, yy)
        pq_ref[pl.ds(ka, DFT_B, stride=pqp), :] = qq[:DFT_B]
        pq_ref[pl.ds(hh + ka, DFT_B, stride=pqp), :] = qq[DFT_B:]
        return carry
    lax.fori_loop(0, hh, mid, 0, unroll=UNROLL_MID)

    bias = bias_ref[...]

    def last(n2, carry):
        qq = pq_ref[pl.ds(pl.multiple_of(n2 * pqp, 8), 2 * hh), :].astype(BF16)
        y = _dot(h2_ref[n2], qq)
        u = u_ref[pl.ds(n2, hh, stride=PITCH), :]
        m = m_ref[pl.ds(n2, hh, stride=PITCH), :]
        o_ref[pl.ds(n2, hh, stride=PITCH), :] = (m * (y + bias * u)).astype(o_ref.dtype)
        return carry
    lax.fori_loop(0, DFT_B, last, 0, unroll=UNROLL_OUTER)
    for n1 in range(hh):
        o_ref[n1 * PITCH + DFT_B:(n1 + 1) * PITCH, :] = jnp.zeros((PITCH - DFT_B, LANES), o_ref.dtype)


def _long_conv(u_arr, u_blk, m_arr, m_blk, kr, ki, order, bias, tables, l, c):
    g1, f2, f2i, h2 = tables
    hh = l // DFT_B
    nct = c // LANES
    kspec = pl.BlockSpec((None, l, LANES), lambda j: (order, 0, j))
    return pl.pallas_call(
        functools.partial(_long_conv_kernel, l=l),
        out_shape=jax.ShapeDtypeStruct((hh * PITCH, c), F32),
        grid=(nct,),
        in_specs=[pl.BlockSpec((hh * PITCH, LANES), lambda j: (0, u_blk * nct + j)),
                  pl.BlockSpec((hh * PITCH, LANES), lambda j: (0, m_blk * nct + j)),
                  kspec, kspec,
                  pl.BlockSpec((1, LANES), lambda j: (0, j)),
                  pl.BlockSpec(g1.shape, lambda j: (0, 0, 0)),
                  pl.BlockSpec(f2.shape, lambda j: (0, 0)),
                  pl.BlockSpec(f2i.shape, lambda j: (0, 0)),
                  pl.BlockSpec(h2.shape, lambda j: (0, 0, 0))],
        out_specs=pl.BlockSpec((hh * PITCH, LANES), lambda j: (0, j)),
        scratch_shapes=[pltpu.VMEM((DFT_B * _pq_pitch(hh), LANES), F32)],
        compiler_params=_cparams(("parallel",)),
        name="hyena_long_conv",
    )(u_arr, m_arr, kr, ki, bias.reshape(1, c), g1, f2, f2i, h2)


def _merge_kernel(hx_ref, ys_ref, yh_ref, wg1_ref, wg2_ref, w1_ref, w2_ref, gb1_ref, gb2_ref, o_ref):
    hx = hx_ref[...]
    g1 = _sigmoid(_dot(hx, wg1_ref[...]) + gb1_ref[...])
    g2 = _sigmoid(_dot(hx, wg2_ref[...]) + gb2_ref[...])
    yh = yh_ref[:, 0:DFT_B, :].reshape(hx.shape).astype(BF16)
    out = g1 * _dot(ys_ref[...], w1_ref[...]) + g2 * _dot(yh, w2_ref[...])
    o_ref[...] = out.astype(o_ref.dtype)


def _merge(hx, ys, yh, wg, gate_col0, w1, w2, gate_b, tm=512, tn=512):
    m, d = hx.shape
    nt = d // tn
    assert gate_col0 % tn == 0
    gj = gate_col0 // tn
    a_spec = pl.BlockSpec((tm, d), lambda i, j: (i, 0))
    w_spec = pl.BlockSpec((d, tn), lambda i, j: (0, j))
    gb = gate_b.reshape(1, 2 * d)
    return pl.pallas_call(
        _merge_kernel,
        out_shape=jax.ShapeDtypeStruct((m, d), BF16),
        grid=(m // tm, nt),
        in_specs=[a_spec, a_spec,
                  pl.BlockSpec((tm // DFT_B, PITCH, d), lambda i, j: (i, 0, 0)),
                  pl.BlockSpec((d, tn), lambda i, j: (0, gj + j)),
                  pl.BlockSpec((d, tn), lambda i, j: (0, gj + nt + j)), w_spec, w_spec,
                  pl.BlockSpec((1, tn), lambda i, j: (0, j)),
                  pl.BlockSpec((1, tn), lambda i, j: (0, nt + j))],
        out_specs=pl.BlockSpec((tm, tn), lambda i, j: (i, j)),
        compiler_params=_cparams(("parallel", "parallel")),
        name="merge",
    )(hx, ys, yh, wg, wg, w1, w2, gb, gb)


def _oproj_kernel(a_ref, w_ref, x_ref, gt_ref, o_ref):
    o_ref[...] = x_ref[...] + gt_ref[...] * _dot(a_ref[...], w_ref[...])


def _oproj(a, w, x, gt, tm=1024, tn=512):
    m, d = a.shape
    n = w.shape[1]
    return pl.pallas_call(
        _oproj_kernel,
        out_shape=jax.ShapeDtypeStruct((m, n), F32),
        grid=(m // tm, n // tn),
        in_specs=[pl.BlockSpec((tm, d), lambda i, j: (i, 0)),
                  pl.BlockSpec((d, tn), lambda i, j: (0, j)),
                  pl.BlockSpec((tm, tn), lambda i, j: (i, j)),
                  pl.BlockSpec((1, tn), lambda i, j: (0, j))],
        out_specs=pl.BlockSpec((tm, tn), lambda i, j: (i, j)),
        compiler_params=_cparams(("parallel", "parallel")),
        name="oproj",
    )(a, w, x, gt)


def _router_kernel(x_ref, g_ref, sh_ref, sc_ref, wr_ref, br_ref, h_ref, ids_ref, wts_ref):
    x = x_ref[...]
    y = x * lax.rsqrt(jnp.mean(x * x, axis=-1, keepdims=True) + EPS) * g_ref[...]
    h = y * (1.0 + sc_ref[...]) + sh_ref[...]
    h_ref[...] = h.astype(h_ref.dtype)
    logits = _dot_hi(h, wr_ref[...]) + br_ref[...]
    tm = x.shape[0]
    lane = lax.broadcasted_iota(jnp.int32, (tm, LANES), 1)
    neg = jnp.float32(-jnp.inf)
    big = jnp.int32(LANES)
    is_grp = (lane >= MOE_EXPERTS) & (lane < MOE_EXPERTS + MOE_GROUPS)
    gl = jnp.where(is_grp, logits, neg)
    gmax = jnp.max(gl, axis=-1, keepdims=True)
    gidx = jnp.min(jnp.where(gl == gmax, lane, big), axis=-1, keepdims=True) - MOE_EXPERTS
    gw = 1.0 / jnp.sum(jnp.where(is_grp, jnp.exp(logits - gmax), 0.0), axis=-1, keepdims=True)
    in_grp = (lane < MOE_EXPERTS) & ((lane // MOE_PER_GROUP) == gidx)
    el = jnp.where(in_grp, logits, neg)
    m1 = jnp.max(el, axis=-1, keepdims=True)
    i1 = jnp.min(jnp.where(el == m1, lane, big), axis=-1, keepdims=True)
    el2 = jnp.where(lane == i1, neg, el)
    m2 = jnp.max(el2, axis=-1, keepdims=True)
    i2 = jnp.min(jnp.where(el2 == m2, lane, big), axis=-1, keepdims=True)
    e21 = jnp.exp(m2 - m1)
    w1 = gw / (1.0 + e21)
    w2 = gw * e21 / (1.0 + e21)
    ids_ref[...] = jnp.where(lane == 0, i1, jnp.where(lane == 1, i2, -1))
    wts_ref[...] = jnp.where(lane == 0, w1, jnp.where(lane == 1, w2, 0.0))


def _router(x1, g, sh, sc, w_rg, b_rg, w_re, b_re, tm=512):
    m, d = x1.shape
    wr = jnp.zeros((d, LANES), F32).at[:, :MOE_EXPERTS].set(w_re)
    wr = wr.at[:, MOE_EXPERTS:MOE_EXPERTS + MOE_GROUPS].set(w_rg)
    br = jnp.zeros((1, LANES), F32).at[0, :MOE_EXPERTS].set(b_re)
    br = br.at[0, MOE_EXPERTS:MOE_EXPERTS + MOE_GROUPS].set(b_rg)
    row = pl.BlockSpec((1, d), lambda i: (0, 0))
    return pl.pallas_call(
        _router_kernel,
        out_shape=(jax.ShapeDtypeStruct((m, d), BF16), jax.ShapeDtypeStruct((m, LANES), jnp.int32),
                   jax.ShapeDtypeStruct((m, LANES), F32)),
        grid=(m // tm,),
        in_specs=[pl.BlockSpec((tm, d), lambda i: (i, 0)), row, row, row,
                  pl.BlockSpec((d, LANES), lambda i: (0, 0)),
                  pl.BlockSpec((1, LANES), lambda i: (0, 0))],
        out_specs=(pl.BlockSpec((tm, d), lambda i: (i, 0)),
                   pl.BlockSpec((tm, LANES), lambda i: (i, 0)),
                   pl.BlockSpec((tm, LANES), lambda i: (i, 0))),
        compiler_params=_cparams(("parallel",)),
        name="router",
    )(x1, g.reshape(1, d), sh, sc, wr, br)


def _moe_positions_kernel(ids_ref, pos_ref, meta_ref):
    n = ids_ref.shape[0]
    t = MOE_TILE
    lane = lax.broadcasted_iota(jnp.int32, (t, LANES), 1)

    def onehot(k):
        idt = ids_ref[pl.ds(pl.multiple_of(k * t, t), t), :]
        i1, i2 = idt[:, 0:1], idt[:, 1:2]
        return i1, i2, jnp.where((lane == i1) | (lane == i2), 1.0, 0.0)

    def count(k, acc):
        return acc + jnp.sum(onehot(k)[2], axis=0, keepdims=True)
    total = lax.fori_loop(0, n // t, count, jnp.zeros((1, LANES), F32))
    padded = (((total.astype(jnp.int32) + (t - 1)) // t) * t).astype(F32)
    r128 = lax.broadcasted_iota(jnp.int32, (LANES, LANES), 0)
    c128 = lax.broadcasted_iota(jnp.int32, (LANES, LANES), 1)
    before = jnp.where(r128 < c128, 1.0, 0.0).astype(BF16)
    off = _dot_sel_r(jnp.broadcast_to(padded, (8, LANES)), before)[0:1]
    row = lax.broadcasted_iota(jnp.int32, (t, t), 0)
    col = lax.broadcasted_iota(jnp.int32, (t, t), 1)
    tri = jnp.where(col <= row, 1.0, 0.0).astype(BF16)

    def place(k, seen):
        i1, i2, oh = onehot(k)
        base = off + seen + _dot(tri, oh.astype(BF16)) - oh
        p1 = jnp.sum(jnp.where(lane == i1, base, 0.0), axis=1, keepdims=True)
        p2 = jnp.sum(jnp.where(lane == i2, base, 0.0), axis=1, keepdims=True)
        pos_ref[pl.ds(pl.multiple_of(k * t, t), t), :] = jnp.where(
            lane == 0, p1, jnp.where(lane == 1, p2, 0.0)).astype(jnp.int32)
        return seen + jnp.sum(oh, axis=0, keepdims=True)
    lax.fori_loop(0, n // t, place, jnp.zeros((1, LANES), F32))

    ends = off + padded
    start = (r128 * t).astype(F32)
    done = jnp.where((jnp.broadcast_to(ends, (LANES, LANES)) <= start) & (c128 < MOE_EXPERTS), 1.0, 0.0)
    tile_expert = jnp.minimum(jnp.sum(done, axis=1, keepdims=True), MOE_EXPERTS - 1.0)
    used = jnp.sum(jnp.where(c128[0:1] == MOE_EXPERTS - 1, ends, 0.0), axis=1, keepdims=True) / t
    meta_ref[...] = jnp.where(c128 == 0, tile_expert, jnp.where(c128 == 1, used, 0.0)).astype(jnp.int32)


def _moe_positions(ids):
    m = ids.shape[0]
    return pl.pallas_call(
        _moe_positions_kernel,
        out_shape=(jax.ShapeDtypeStruct((m, LANES), jnp.int32),
                   jax.ShapeDtypeStruct((LANES, LANES), jnp.int32)),
        grid=(1,),
        in_specs=[pl.BlockSpec((m, LANES), lambda i: (0, 0))],
        out_specs=(pl.BlockSpec((m, LANES), lambda i: (0, 0)),
                   pl.BlockSpec((LANES, LANES), lambda i: (0, 0))),
        compiler_params=_cparams(("arbitrary",)),
        name="moe_positions",
    )(ids)


def _moe_dispatch_kernel(p1_ref, p2_ref, h_ref, xs_in_ref, xs_ref, sem):
    del xs_in_ref
    t = h_ref.shape[0]
    base = pl.program_id(0) * t

    def issue(r, carry):
        pltpu.make_async_copy(h_ref.at[r], xs_ref.at[p1_ref[base + r]], sem.at[0]).start()
        pltpu.make_async_copy(h_ref.at[r], xs_ref.at[p2_ref[base + r]], sem.at[1]).start()
        return carry
    lax.fori_loop(0, t, issue, 0, unroll=8)
    pltpu.make_async_copy(h_ref, xs_ref.at[pl.ds(0, t)], sem.at[0]).wait()
    pltpu.make_async_copy(h_ref, xs_ref.at[pl.ds(0, t)], sem.at[1]).wait()


def _moe_dispatch(h3, pos1, pos2, n_sorted):
    m, s, _ = h3.shape
    t = MOE_TILE
    zeros = jnp.zeros((n_sorted, s, LANES), h3.dtype)
    return pl.pallas_call(
        _moe_dispatch_kernel,
        out_shape=jax.ShapeDtypeStruct((n_sorted, s, LANES), h3.dtype),
        grid_spec=pltpu.PrefetchScalarGridSpec(
            num_scalar_prefetch=2,
            grid=(m // t,),
            in_specs=[pl.BlockSpec((t, s, LANES), lambda i, p1, p2: (i, 0, 0)),
                      pl.BlockSpec(memory_space=pl.ANY)],
            out_specs=pl.BlockSpec(memory_space=pl.ANY),
            scratch_shapes=[pltpu.SemaphoreType.DMA((2,))]),
        input_output_aliases={3: 0},
        compiler_params=_cparams(("arbitrary",)),
        name="moe_dispatch",
    )(pos1, pos2, h3, zeros)


def _moe_experts_kernel(te_ref, nu_ref, x_ref, wg_ref, wu_ref, wd_ref, y_ref):
    del te_ref
    live = pl.program_id(0) < nu_ref[0]

    @pl.when(live)
    def _():
        x = x_ref[...]
        hid = _silu(_dot(x, wg_ref[...])) * _dot(x, wu_ref[...])
        y_ref[...] = _dot(hid.astype(BF16), wd_ref[...])

    @pl.when(jnp.logical_not(live))
    def _():
        y_ref[...] = jnp.zeros_like(y_ref)


def _moe_experts(xs, tile_expert, n_used, wg, wu, wd):
    ns, d = xs.shape
    ne, _, f = wg.shape
    t = MOE_TILE
    return pl.pallas_call(
        _moe_experts_kernel,
        out_shape=jax.ShapeDtypeStruct((ns, d), F32),
        grid_spec=pltpu.PrefetchScalarGridSpec(
            num_scalar_prefetch=2,
            grid=(ns // t,),
            in_specs=[pl.BlockSpec((t, d), lambda i, te, nu: (i, 0)),
                      pl.BlockSpec((None, d, f), lambda i, te, nu: (te[i], 0, 0)),
                      pl.BlockSpec((None, d, f), lambda i, te, nu: (te[i], 0, 0)),
                      pl.BlockSpec((None, f, d), lambda i, te, nu: (te[i], 0, 0))],
            out_specs=pl.BlockSpec((t, d), lambda i, te, nu: (i, 0))),
        compiler_params=_cparams(("arbitrary",)),
        name="moe_experts",
    )(tile_expert, n_used, xs, wg, wu, wd)


def _moe_combine_kernel(p1_ref, p2_ref, x_ref, wts_ref, gt_ref, fg_ref, y_ref, o_ref, ya, yb, sem):
    t = x_ref.shape[0]
    i = pl.program_id(0)
    n = pl.num_programs(0)

    def row_copy(src_row, buf, slot, r, which):
        return pltpu.make_async_copy(y_ref.at[pl.ds(src_row, 1), :], buf.at[slot, pl.ds(r, 1), :],
                                     sem.at[which, slot])

    def issue(tile, slot):
        def body(r, carry):
            row_copy(p1_ref[tile * t + r], ya, slot, r, 0).start()
            row_copy(p2_ref[tile * t + r], yb, slot, r, 1).start()
            return carry
        lax.fori_loop(0, t, body, 0, unroll=8)

    @pl.when(i == 0)
    def _():
        issue(0, 0)

    @pl.when(i + 1 < n)
    def _():
        issue(i + 1, (i + 1) % 2)

    slot = i % 2
    pltpu.make_async_copy(y_ref.at[pl.ds(0, t), :], ya.at[slot], sem.at[0, slot]).wait()
    pltpu.make_async_copy(y_ref.at[pl.ds(0, t), :], yb.at[slot], sem.at[1, slot]).wait()
    w = wts_ref[...]
    moe = w[:, 0:1] * ya[slot] + w[:, 1:2] * yb[slot]
    x2 = x_ref[...] + gt_ref[...] * moe
    o_ref[...] = x2 * lax.rsqrt(jnp.mean(x2 * x2, axis=-1, keepdims=True) + EPS) * fg_ref[...]


def _moe_combine(x1, wts, gt, fg, y_sorted, pos1, pos2):
    m, d = x1.shape
    t = MOE_TILE
    row = pl.BlockSpec((1, d), lambda i, p1, p2: (0, 0))
    return pl.pallas_call(
        _moe_combine_kernel,
        out_shape=jax.ShapeDtypeStruct((m, d), F32),
        grid_spec=pltpu.PrefetchScalarGridSpec(
            num_scalar_prefetch=2,
            grid=(m // t,),
            in_specs=[pl.BlockSpec((t, d), lambda i, p1, p2: (i, 0)),
                      pl.BlockSpec((t, LANES), lambda i, p1, p2: (i, 0)),
                      row, row,
                      pl.BlockSpec(memory_space=pl.ANY)],
            out_specs=pl.BlockSpec((t, d), lambda i, p1, p2: (i, 0)),
            scratch_shapes=[pltpu.VMEM((2, t, d), F32), pltpu.VMEM((2, t, d), F32),
                            pltpu.SemaphoreType.DMA((2, 2))]),
        compiler_params=_cparams(("arbitrary",)),
        name="moe_combine",
    )(pos1, pos2, x1, wts, gt, fg.reshape(1, d), y_sorted)


def kernel(x, c, ctx, c_ctx, ada_w, ada_b, norm1_g, w_in, ssd_conv_w, ssd_conv_b, ssd_dt_bias,
           ssd_a_log, ssd_d, ssd_norm_g, w_ssd_out, hy_conv_w, hy_conv_b, hy_f_w1, hy_f_b1,
           hy_f_freq1, hy_f_w2, hy_f_b2, hy_f_freq2, hy_f_w3, hy_bias, w_hy_out, gate_b, w_o,
           norm2_g, moe_w_rg, moe_b_rg, moe_w_re, moe_b_re, moe_w_gate, moe_w_up, moe_w_down,
           final_g):
    bsz, l, d = x.shape
    assert bsz == 1 and ada_w.shape[0] == 1
    lc = ctx.shape[1]
    di = SSD_D_INNER
    xbc_cols = di + 2 * SSD_GN
    col_xbc = di
    col_dt = col_xbc + xbc_cols
    col_hy = col_dt + 2 * SSD_HEADS
    hyw = hy_bias.shape[-1]

    x2d = x[0]
    ctx2d = ctx[0]

    cc = jnp.zeros((8, d), F32).at[0].set(c[0]).at[1].set(c_ctx)
    mods = _adaln(cc, ada_w[0], ada_b[0])
    sh1, sc1, gt1, sh2, sc2, gt2 = [mods[0:1, i * d:(i + 1) * d] for i in range(6)]
    csh1, csc1 = mods[1:2, 0:d], mods[1:2, d:2 * d]

    hx = _normmod(x2d, norm1_g[0], sh1, sc1, tm=512)
    hc = _normmod(ctx2d, norm1_g[0], csh1, csc1, tm=lc)

    n_dt = LANES
    dt_pad = jnp.zeros((d, n_dt - 2 * SSD_HEADS), F32)
    w_in_b = jnp.concatenate([w_in[0, :, :col_dt], w_in[0, :, col_hy:], w_in[0, :, col_dt:col_hy],
                              dt_pad], axis=1).astype(BF16)
    c_z, c_xbc, c_hy = 0, col_xbc, col_dt
    c_gate = c_hy + 3 * hyw
    c_dt = c_gate + 2 * d
    dt_bias = jnp.concatenate([ssd_dt_bias[0].reshape(-1), jnp.zeros((n_dt - 2 * SSD_HEADS,), F32)])

    xbc_c = _proj(hc, w_in_b, c_xbc, xbc_cols, "conv_silu", tm=lc, tn=512, conv_w=ssd_conv_w[0],
                  bias=ssd_conv_b[0], group=lc)
    dt_c = _proj(hc, w_in_b, c_dt, n_dt, "softplus", tm=lc, tn=n_dt, bias=dt_bias)
    alog = ssd_a_log[0]
    alogT = alog.T
    zero_state = jnp.zeros((SSD_GROUPS, SSD_STATE, di // SSD_GROUPS), F32)
    h0f = _ssd(xbc_c, dt_c, dt_c.T, alog, alogT, zero_state, reverse=False, mode="state")
    h0b = _ssd(xbc_c, dt_c, dt_c.T, alog, alogT, zero_state, reverse=True, mode="state")

    z = _proj(hx, w_in_b, c_z, di, "none", tm=1024, tn=512)
    xbc = _proj(hx, w_in_b, c_xbc, xbc_cols, "conv_silu", tm=1024, tn=512, conv_w=ssd_conv_w[0],
                bias=ssd_conv_b[0])
    dt = _proj(hx, w_in_b, c_dt, n_dt, "softplus", tm=1024, tn=n_dt, bias=dt_bias)
    u3 = _proj(hx, w_in_b, c_hy, 3 * hyw, "conv", tm=1024, tn=512, conv_w=hy_conv_w[0],
               bias=hy_conv_b[0], pitched=True).reshape(l // DFT_B * PITCH, 3 * hyw)

    dtT = dt.T
    yf = _ssd(xbc, dt, dtT, alog, alogT, h0f, reverse=False, mode="fwd")
    dx = jnp.repeat(ssd_d[0], SSD_HEADDIM).reshape(1, di)
    y_ssd = _ssd(xbc, dt, dtT, alog, alogT, h0b, reverse=True, mode="bwd", yf=yf, z=z, dx=dx,
                 ng=ssd_norm_g[0].reshape(1, di))

    tables = _dft_tables(l)
    hid = _filter_mlp(l, hy_f_w1[0], hy_f_b1[0], hy_f_freq1[0], hy_f_w2[0], hy_f_b2[0],
                      hy_f_freq2[0])
    max_decay = math.log(1e-2) / 0.3
    min_decay = math.log(1e-2) / 1.5
    deltas = jnp.abs(jnp.linspace(min_decay, max_decay, hyw, dtype=F32)).reshape(1, hyw)
    kr, ki = _filter_spec(hid, hy_f_w3[0], deltas, tables[0], tables[1], l, hyw)
    z2 = _long_conv(u3, 0, u3, 1, kr, ki, 0, hy_bias[0, 0], tables, l, hyw)
    y_hy = _long_conv(z2, 0, u3, 2, kr, ki, 1, hy_bias[0, 1], tables, l, hyw)

    merged = _merge(hx, y_ssd, y_hy.reshape(l // DFT_B, PITCH, hyw), w_in_b, c_gate,
                    w_ssd_out[0].astype(BF16),
                    w_hy_out[0].astype(BF16), gate_b[0])
    x1 = _oproj(merged, w_o[0].astype(BF16), x2d, gt1)

    h2, ids, wts = _router(x1, norm2_g[0], sh2, sc2, moe_w_rg[0], moe_b_rg[0], moe_w_re[0],
                           moe_b_re[0])
    pos, meta = _moe_positions(ids)
    pos1, pos2 = pos[:, 0], pos[:, 1]
    n_sorted = 2 * l + MOE_EXPERTS * MOE_TILE
    n_tiles = n_sorted // MOE_TILE
    xs = _moe_dispatch(h2.reshape(l, d // LANES, LANES), pos1, pos2, n_sorted)
    ys = _moe_experts(xs.reshape(n_sorted, d), meta[:n_tiles, 0], meta[0, 1:2],
                      moe_w_gate[0].astype(BF16), moe_w_up[0].astype(BF16),
                      moe_w_down[0].astype(BF16))
    out = _moe_combine(x1, wts, gt2, final_g, ys, pos1, pos2)
    return out[None]
```

```python
import functools
import math

import jax
import jax.numpy as jnp
from jax import lax
from jax.experimental import pallas as pl
from jax.experimental.pallas import tpu as pltpu

F32 = jnp.float32
BF16 = jnp.bfloat16

EPS = 1e-6
GRID_W = 64
SSD_HEADS = 32
SSD_HEADDIM = 64
SSD_GROUPS = 4
SSD_STATE = 128
SSD_CHUNK = 128
SSD_D_INNER = SSD_HEADS * SSD_HEADDIM
SSD_GN = SSD_GROUPS * SSD_STATE
HY_BANDS = 16
HY_EMB = 33
HY_HIDDEN = 64
MOE_GROUPS = 4
MOE_PER_GROUP = 4
MOE_EXPERTS = 16

LANES = 128
DFT_B = 128
PITCH = 136
VMEM_LIMIT = 56 * 1024 * 1024
UNROLL_OUTER = 16
UNROLL_MID = 8
ROW_ALIGN = 16
PROJ_SUB_ROWS = 256
MOE_TILE = 256


def _cparams(sem):
    return pltpu.CompilerParams(dimension_semantics=sem, vmem_limit_bytes=VMEM_LIMIT)


def _dot(a, b):
    return jnp.dot(a, b, preferred_element_type=F32)


def _dot_nt(a, bt):
    return lax.dot_general(a, bt, (((1,), (1,)), ((), ())), preferred_element_type=F32)


def _sigmoid(x):
    return 1.0 / (1.0 + jnp.exp(-x))


def _silu(x):
    return x * _sigmoid(x)


def _split3(v):
    b1 = v.astype(BF16)
    r1 = v - b1.astype(F32)
    b2 = r1.astype(BF16)
    b3 = (r1 - b2.astype(F32)).astype(BF16)
    return b1, b2, b3


def _dot_sel_r(v, sel_b):
    b1, b2, b3 = _split3(v)
    return _dot(b1, sel_b) + _dot(b2, sel_b) + _dot(b3, sel_b)


def _dot_sel_l(sel_b, v):
    b1, b2, b3 = _split3(v)
    return _dot(sel_b, b1) + _dot(sel_b, b2) + _dot(sel_b, b3)


def _dot_hi(a, b):
    a1, a2, a3 = _split3(a)
    b1, b2, b3 = _split3(b)
    return (_dot(a1, b1) + (_dot(a1, b2) + _dot(a2, b1))
            + (_dot(a1, b3) + _dot(a2, b2) + _dot(a3, b1)))


def _adaln_kernel(c_ref, w_ref, b_ref, o_ref):
    s = _silu(c_ref[...]).astype(BF16)
    o_ref[...] = _dot(s, w_ref[...].astype(BF16)) + b_ref[...]


def _adaln(cc, w, b, tn=1024):
    m, d = cc.shape
    n = w.shape[1]
    return pl.pallas_call(
        _adaln_kernel,
        out_shape=jax.ShapeDtypeStruct((m, n), F32),
        grid=(n // tn,),
        in_specs=[pl.BlockSpec((m, d), lambda j: (0, 0)),
                  pl.BlockSpec((d, tn), lambda j: (0, j)),
                  pl.BlockSpec((1, tn), lambda j: (0, j))],
        out_specs=pl.BlockSpec((m, tn), lambda j: (0, j)),
        compiler_params=_cparams(("parallel",)),
        name="adaln",
    )(cc, w, b.reshape(1, n))


def _normmod_kernel(x_ref, g_ref, sh_ref, sc_ref, o_ref):
    x = x_ref[...]
    y = x * lax.rsqrt(jnp.mean(x * x, axis=-1, keepdims=True) + EPS) * g_ref[...]
    o_ref[...] = (y * (1.0 + sc_ref[...]) + sh_ref[...]).astype(o_ref.dtype)


def _normmod(x, g, sh, sc, tm):
    m, d = x.shape
    row = pl.BlockSpec((1, d), lambda i: (0, 0))
    return pl.pallas_call(
        _normmod_kernel,
        out_shape=jax.ShapeDtypeStruct((m, d), BF16),
        grid=(m // tm,),
        in_specs=[pl.BlockSpec((tm, d), lambda i: (i, 0)), row, row, row],
        out_specs=pl.BlockSpec((tm, d), lambda i: (i, 0)),
        compiler_params=_cparams(("parallel",)),
        name="normmod",
    )(x, g.reshape(1, d), sh, sc)


def _conv_rows(acc, cw_ref, group):
    tm, tn = acc.shape
    taps = cw_ref.shape[0]
    half = taps // 2
    pos = lax.broadcasted_iota(jnp.int32, (tm, tn), 0) & (group - 1)
    out = acc * cw_ref[half:half + 1, :]
    for k in range(taps):
        d = k - half
        if d == 0:
            continue
        shifted = pltpu.roll(acc, (tm - d) % tm, 0)
        valid = (pos + d >= 0) & (pos + d < group)
        out = out + jnp.where(valid, shifted, 0.0) * cw_ref[k:k + 1, :]
    return out


def _proj_kernel(a_ref, w_ref, *rest, epi, group):
    o_ref = rest[-1]
    tm = a_ref.shape[0]
    sub = max(group, min(tm, PROJ_SUB_ROWS))
    for s in range(tm // sub):
        acc = _dot_nt(a_ref[s * sub:(s + 1) * sub, :], w_ref[...])
        if epi == "none":
            out = acc
        elif epi == "conv":
            cw_ref, b_ref = rest[0], rest[1]
            out = _conv_rows(acc, cw_ref, group) + b_ref[...]
        elif epi == "conv_silu":
            cw_ref, b_ref = rest[0], rest[1]
            out = _silu(_conv_rows(acc, cw_ref, group) + b_ref[...])
        elif epi == "softplus":
            v = acc + rest[0][...]
            out = jnp.maximum(v, 0.0) + jnp.log(1.0 + jnp.exp(-jnp.abs(v)))
        else:
            raise ValueError(epi)
        if len(o_ref.shape) == 3:
            nb, tn = sub // DFT_B, o_ref.shape[2]
            o_ref[s * nb:(s + 1) * nb, 0:DFT_B, :] = out.reshape(nb, DFT_B, tn).astype(o_ref.dtype)
            o_ref[s * nb:(s + 1) * nb, DFT_B:PITCH, :] = jnp.zeros((nb, PITCH - DFT_B, tn), o_ref.dtype)
        else:
            o_ref[s * sub:(s + 1) * sub, :] = out.astype(o_ref.dtype)


def _proj(a, wt, row0, n, epi, tm, tn, conv_w=None, bias=None, group=GRID_W, out_dtype=F32,
          pitched=False):
    m, k = a.shape
    tn = min(tn, n)
    if pitched:
        out_shape = jax.ShapeDtypeStruct((m // DFT_B, PITCH, n), out_dtype)
        out_spec = pl.BlockSpec((tm // DFT_B, PITCH, tn), lambda i, j: (i, 0, j))
    else:
        out_shape = jax.ShapeDtypeStruct((m, n), out_dtype)
        out_spec = pl.BlockSpec((tm, tn), lambda i, j: (i, j))
    extra, extra_specs = [], []
    if conv_w is not None:
        extra.append(conv_w)
        extra_specs.append(pl.BlockSpec((conv_w.shape[0], tn), lambda i, j: (0, j)))
    if bias is not None:
        extra.append(bias.reshape(1, n))
        extra_specs.append(pl.BlockSpec((1, tn), lambda i, j: (0, j)))
    return pl.pallas_call(
        functools.partial(_proj_kernel, epi=epi, group=group),
        out_shape=out_shape,
        grid=(m // tm, n // tn),
        in_specs=[pl.BlockSpec((tm, k), lambda i, j: (i, 0)),
                  pl.BlockSpec((pl.Element(tn), pl.Element(k)),
                               lambda i, j: (pl.multiple_of(row0 + j * tn, ROW_ALIGN), 0))]
        + extra_specs,
        out_specs=out_spec,
        compiler_params=_cparams(("parallel", "parallel")),
        name="proj_" + epi,
    )(a, wt, *extra)


def _ssd_kernel(xs_ref, b_ref, c_ref, dt_ref, dtT_ref, alog_ref, alogT_ref, h0_ref, *rest,
                reverse, mode):
    if mode == "state":
        hfin_ref, st_ref = rest
    elif mode == "fwd":
        y_ref, st_ref = rest
    else:
        yf_ref, z_ref, dx_ref, ng_ref, y_ref, st_ref = rest
    q = xs_ref.shape[0]
    nh = SSD_HEADS
    rp = SSD_D_INNER // SSD_GROUPS
    step = pl.program_id(0)

    @pl.when(step == 0)
    def _():
        st_ref[...] = h0_ref[...]

    d = 1 if reverse else 0
    dtd = dt_ref[:, d * nh:(d + 1) * nh]
    dtdT = dtT_ref[d * nh:(d + 1) * nh, :]
    a_row = -jnp.exp(alog_ref[d:d + 1, :])
    a_col = -jnp.exp(alogT_ref[:, d:d + 1])
    ad = dtd * a_row
    adT = dtdT * a_col
    row = lax.broadcasted_iota(jnp.int32, (q, q), 0)
    col = lax.broadcasted_iota(jnp.int32, (q, q), 1)
    lower = col <= row
    upper = col >= row
    causal = upper if reverse else lower
    tri = jnp.where(causal, 1.0, 0.0).astype(BF16)
    triT = jnp.where(lower if reverse else upper, 1.0, 0.0).astype(BF16)
    acs = _dot_sel_l(tri, ad)
    acsT = _dot_sel_r(adT, triT)
    tot = jnp.sum(ad, axis=0, keepdims=True)
    dte = jnp.exp(tot - acs)
    eacs = jnp.exp(acs)
    cdec = jnp.exp(tot)
    hsel = lax.broadcasted_iota(jnp.int32, (nh, SSD_D_INNER), 0)
    lsel = lax.broadcasted_iota(jnp.int32, (nh, SSD_D_INNER), 1) // SSD_HEADDIM
    expand = jnp.where(hsel == lsel, 1.0, 0.0).astype(BF16)
    stacked = jnp.concatenate([dtd * dte, eacs, jnp.broadcast_to(cdec, (8, nh))], axis=0)
    exp_all = _dot_sel_r(stacked, expand)
    w_x = exp_all[0:q]
    eacs_x = exp_all[q:2 * q]
    cdec_x = exp_all[2 * q:2 * q + 1]

    xs = xs_ref[...]
    xs_b = xs.astype(BF16)
    xdw = (xs * w_x).astype(BF16)
    neg_inf = jnp.float32(-jnp.inf)
    y_groups = []
    for g in range(SSD_GROUPS):
        bg = b_ref[:, g * SSD_STATE:(g + 1) * SSD_STATE]
        st_old = st_ref[g]
        s_new = _dot(bg.T.astype(BF16), xdw[:, g * rp:(g + 1) * rp])
        if mode != "state":
            bg_b = bg.astype(BF16)
            cg_b = c_ref[:, g * SSD_STATE:(g + 1) * SSD_STATE].astype(BF16)
            cb = lax.dot_general(cg_b, bg_b, (((1,), (1,)), ((), ())),
                                 preferred_element_type=F32)
            y_off = _dot(cg_b, st_old.astype(BF16)) * eacs_x[:, g * rp:(g + 1) * rp]
            parts = []
            for r in range(SSD_HEADS // SSD_GROUPS):
                h = g * (SSD_HEADS // SSD_GROUPS) + r
                seg = acs[:, h:h + 1] - acsT[h:h + 1, :]
                dec = jnp.exp(jnp.where(causal, seg, neg_inf))
                mat = (cb * dec * dtdT[h:h + 1, :]).astype(BF16)
                parts.append(_dot(mat, xs_b[:, h * SSD_HEADDIM:(h + 1) * SSD_HEADDIM]))
            y_groups.append(jnp.concatenate(parts, axis=1) + y_off)
        st_ref[g] = st_old * cdec_x[:, g * rp:(g + 1) * rp] + s_new

    if mode == "state":
        @pl.when(step == pl.num_programs(0) - 1)
        def _():
            hfin_ref[...] = st_ref[...]
    elif mode == "fwd":
        for g in range(SSD_GROUPS):
            y_ref[:, g * rp:(g + 1) * rp] = y_groups[g]
    else:
        for g in range(SSD_GROUPS):
            sl = slice(g * rp, (g + 1) * rp)
            y = yf_ref[:, sl] + y_groups[g] + dx_ref[:, sl] * xs[:, sl]
            yz = y * _silu(z_ref[:, sl])
            yz = yz * lax.rsqrt(jnp.mean(yz * yz, axis=-1, keepdims=True) + EPS)
            y_ref[:, sl] = (yz * ng_ref[:, sl]).astype(y_ref.dtype)


def _ssd(xbc, dt, dtT, alog, alogT, h0, *, reverse, mode, yf=None, z=None, dx=None, ng=None):
    l = xbc.shape[0]
    q = SSD_CHUNK
    nc = l // q
    di = SSD_D_INNER
    nb = di // SSD_GN
    if reverse:
        cidx = lambda c: nc - 1 - c
    else:
        cidx = lambda c: c
    st_shape = (SSD_GROUPS, SSD_STATE, di // SSD_GROUPS)
    full3 = pl.BlockSpec(st_shape, lambda c: (0, 0, 0))
    in_specs = [pl.BlockSpec((q, di), lambda c: (cidx(c), 0)),
                pl.BlockSpec((q, SSD_GN), lambda c: (cidx(c), nb)),
                pl.BlockSpec((q, SSD_GN), lambda c: (cidx(c), nb + 1)),
                pl.BlockSpec((q, dt.shape[1]), lambda c: (cidx(c), 0)),
                pl.BlockSpec((dtT.shape[0], q), lambda c: (0, cidx(c))),
                pl.BlockSpec((2, SSD_HEADS), lambda c: (0, 0)),
                pl.BlockSpec((SSD_HEADS, 2), lambda c: (0, 0)),
                full3]
    args = [xbc, xbc, xbc, dt, dtT, alog, alogT, h0]
    wide = pl.BlockSpec((q, di), lambda c: (cidx(c), 0))
    rowspec = pl.BlockSpec((1, di), lambda c: (0, 0))
    if mode == "state":
        out_shape = jax.ShapeDtypeStruct(st_shape, F32)
        out_specs = full3
    elif mode == "fwd":
        out_shape = jax.ShapeDtypeStruct((l, di), F32)
        out_specs = wide
    else:
        in_specs += [wide, wide, rowspec, rowspec]
        args += [yf, z, dx, ng]
        out_shape = jax.ShapeDtypeStruct((l, di), BF16)
        out_specs = wide
    return pl.pallas_call(
        functools.partial(_ssd_kernel, reverse=reverse, mode=mode),
        out_shape=out_shape,
        grid=(nc,),
        in_specs=in_specs,
        out_specs=out_specs,
        scratch_shapes=[pltpu.VMEM(st_shape, F32)],
        compiler_params=_cparams(("arbitrary",)),
        name="ssd_%s_%s" % (mode, "rev" if reverse else "fwd"),
    )(*args)


def _dft_tables(l):
    n = 2 * l
    hh = l // DFT_B
    ka = jnp.arange(hh, dtype=jnp.int32)
    n1 = jnp.arange(hh, dtype=jnp.int32)
    n2 = jnp.arange(DFT_B, dtype=jnp.int32)
    odd = 2 * ka + 1
    ang_a = ((odd[:, None] * (DFT_B * n1)[None, :]) % (2 * n)).astype(F32) * (math.pi / n)
    ang_b = ((n2[:, None] * odd[None, :]) % (2 * n)).astype(F32) * (math.pi / n)
    ca, sa = jnp.cos(ang_a)[None, :, :], jnp.sin(ang_a)[None, :, :]
    cb, sb = jnp.cos(ang_b)[:, :, None], jnp.sin(ang_b)[:, :, None]
    cs = ca * cb - sa * sb
    sn = sa * cb + ca * sb
    g1 = jnp.concatenate([cs, -sn], axis=1).astype(BF16)
    scale = 2.0 / n
    h2 = jnp.concatenate([jnp.swapaxes(cs, 1, 2), -jnp.swapaxes(sn, 1, 2)], axis=2) * scale
    kb = jnp.arange(DFT_B, dtype=jnp.int32)
    ph2 = (kb[:, None] * n2[None, :]) % DFT_B
    ang2 = ph2.astype(F32) * (2.0 * math.pi / DFT_B)
    fr, fi = jnp.cos(ang2), -jnp.sin(ang2)
    f2 = jnp.concatenate([jnp.concatenate([fr, -fi], axis=1),
                          jnp.concatenate([fi, fr], axis=1)], axis=0).astype(BF16)
    f2i = jnp.concatenate([jnp.concatenate([fr, fi], axis=1),
                           jnp.concatenate([-fi, fr], axis=1)], axis=0).astype(BF16)
    return g1, f2, f2i, h2.astype(BF16)


def _filter_mlp_kernel(bands_ref, w1_ref, b1_ref, f1_ref, w2_ref, b2_ref, f2_ref, o_ref, *, l):
    tm = o_ref.shape[0]
    base = pl.program_id(0) * tm
    idx = (lax.broadcasted_iota(jnp.int32, (tm, LANES), 0) + base).astype(F32)
    lane = lax.broadcasted_iota(jnp.int32, (tm, LANES), 1)
    t = idx * (1.0 / (l - 1))
    w = idx * (2.0 * math.pi / l)
    arg = bands_ref[...] * w
    feats = jnp.where(lane == 0, t,
                      jnp.where(lane <= HY_BANDS, jnp.cos(arg),
                                jnp.where(lane < HY_EMB, -jnp.sin(arg), 0.0)))
    h = jnp.sin(f1_ref[...] * (_dot_hi(feats, w1_ref[...]) + b1_ref[...]))
    h = jnp.sin(f2_ref[...] * (_dot_hi(h, w2_ref[...]) + b2_ref[...]))
    o_ref[...] = h.astype(o_ref.dtype)


def _filter_mlp(l, w1, b1, f1, w2, b2, f2, tm=1024):
    tm = min(tm, l)
    bands = jnp.linspace(1e-4, HY_BANDS - 1, HY_BANDS, dtype=F32)
    bands_row = jnp.zeros((1, LANES), F32).at[0, 1:1 + HY_BANDS].set(bands)
    bands_row = bands_row.at[0, 1 + HY_BANDS:HY_EMB].set(bands)
    w1p = jnp.zeros((LANES, HY_HIDDEN), F32).at[:HY_EMB].set(w1)
    full = lambda a: pl.BlockSpec(a.shape, lambda i: (0,) * a.ndim)
    args = [bands_row, w1p, b1.reshape(1, -1), f1.reshape(1, -1), w2, b2.reshape(1, -1),
            f2.reshape(1, -1)]
    return pl.pallas_call(
        functools.partial(_filter_mlp_kernel, l=l),
        out_shape=jax.ShapeDtypeStruct((l, HY_HIDDEN), BF16),
        grid=(l // tm,),
        in_specs=[full(a) for a in args],
        out_specs=pl.BlockSpec((tm, HY_HIDDEN), lambda i: (i, 0)),
        compiler_params=_cparams(("parallel",)),
        name="hyena_filter_mlp",
    )(*args)


def _pq_pitch(hh):
    return 2 * hh + 8


def _fwd_stage1(src_ref, g1_ref, pq_ref, hh):
    pqp = _pq_pitch(hh)

    def body(n2, carry):
        rows = src_ref[pl.ds(n2, hh, stride=PITCH), :]
        p = _dot(g1_ref[n2], rows.astype(BF16))
        pq_ref[pl.ds(pl.multiple_of(n2 * pqp, 8), 2 * hh), :] = p
        return carry
    lax.fori_loop(0, DFT_B, body, 0, unroll=UNROLL_OUTER)


def _load_spectrum_rows(pq_ref, ka, hh):
    pqp = _pq_pitch(hh)
    return jnp.concatenate([pq_ref[pl.ds(ka, DFT_B, stride=pqp), :],
                            pq_ref[pl.ds(hh + ka, DFT_B, stride=pqp), :]], axis=0)


def _filter_spec_kernel(hid_ref, wf_ref, wb_ref, dl_ref, g1_ref, f2_ref, kr_ref, ki_ref,
                        s_ref, d_ref, pq_ref, *, l):
    hh = l // DFT_B
    t = lax.broadcasted_iota(jnp.int32, (l, LANES), 0).astype(F32) * (1.0 / (l - 1))
    dec = jnp.exp(-t * dl_ref[...])
    hid = hid_ref[...]
    hf = _dot(hid, wf_ref[...].astype(BF16)) * dec
    hb = _dot(hid, wb_ref[...].astype(BF16)) * dec
    first = lax.broadcasted_iota(jnp.int32, (l, LANES), 0) == 0
    hb = jnp.where(first, 0.0, hb)
    inv = 1.0 / (jnp.sum(jnp.abs(hf), axis=0, keepdims=True)
                 + jnp.sum(jnp.abs(hb), axis=0, keepdims=True))
    hs = hf + hb
    hd = hf - hb
    for n1 in range(hh):
        s_ref[n1 * PITCH:n1 * PITCH + DFT_B, :] = hs[n1 * DFT_B:(n1 + 1) * DFT_B]
        d_ref[n1 * PITCH:n1 * PITCH + DFT_B, :] = hd[n1 * DFT_B:(n1 + 1) * DFT_B]
    f2 = f2_ref[...]
    for src_ref, out_ref, lo in ((s_ref, kr_ref, 0), (d_ref, ki_ref, DFT_B)):
        _fwd_stage1(src_ref, g1_ref, pq_ref, hh)

        def body(ka, carry, out_ref=out_ref, lo=lo):
            pp = _load_spectrum_rows(pq_ref, ka, hh).astype(BF16)
            x = _dot(f2[lo:lo + DFT_B, :], pp)
            out_ref[pl.ds(pl.multiple_of(ka * DFT_B, DFT_B), DFT_B), :] = (x * inv).astype(
                out_ref.dtype)
            return carry
        lax.fori_loop(0, hh, body, 0, unroll=UNROLL_MID)


def _filter_spec(hid, w3, deltas, g1, f2, l, c):
    hh = l // DFT_B
    nct = c // LANES
    orders = w3.shape[1] // (2 * c)
    spec = pl.BlockSpec((None, l, LANES), lambda o, j: (o, 0, j))
    out_sd = jax.ShapeDtypeStruct((orders, l, c), BF16)
    return pl.pallas_call(
        functools.partial(_filter_spec_kernel, l=l),
        out_shape=(out_sd, out_sd),
        grid=(orders, nct),
        in_specs=[pl.BlockSpec((l, HY_HIDDEN), lambda o, j: (0, 0)),
                  pl.BlockSpec((HY_HIDDEN, LANES), lambda o, j: (0, (2 * o) * nct + j)),
                  pl.BlockSpec((HY_HIDDEN, LANES), lambda o, j: (0, (2 * o + 1) * nct + j)),
                  pl.BlockSpec((1, LANES), lambda o, j: (0, j)),
                  pl.BlockSpec(g1.shape, lambda o, j: (0, 0, 0)),
                  pl.BlockSpec(f2.shape, lambda o, j: (0, 0))],
        out_specs=(spec, spec),
        scratch_shapes=[pltpu.VMEM((hh * PITCH, LANES), F32)] * 2
        + [pltpu.VMEM((DFT_B * _pq_pitch(hh), LANES), F32)],
        compiler_params=_cparams(("parallel", "parallel")),
        name="hyena_filter_spec",
    )(hid, w3, w3, deltas, g1, f2)


def _long_conv_kernel(u_ref, m_ref, kr_ref, ki_ref, bias_ref, g1_ref, f2_ref, f2i_ref, h2_ref,
                      o_ref, pq_ref, *, l):
    hh = l // DFT_B
    pqp = _pq_pitch(hh)
    _fwd_stage1(u_ref, g1_ref, pq_ref, hh)
    f2 = f2_ref[...]
    f2i = f2i_ref[...]

    def mid(ka, carry):
        koff = pl.multiple_of(ka * DFT_B, DFT_B)
        x = _dot(f2, _load_spectrum_rows(pq_ref, ka, hh).astype(BF16))
        xr, xi = x[:DFT_B], x[DFT_B:]
        kr = kr_ref[pl.ds(koff, DFT_B), :].astype(F32)
        ki = ki_ref[pl.ds(koff, DFT_B), :].astype(F32)
        yy = jnp.concatenate([xr * kr - xi * ki, xr * ki + xi * kr], axis=0).astype(BF16)
        qq = _dot(f2i, yy)
        pq_ref[pl.ds(ka, DFT_B, stride=pqp), :] = qq[:DFT_B]
        pq_ref[pl.ds(hh + ka, DFT_B, stride=pqp), :] = qq[DFT_B:]
        return carry
    lax.fori_loop(0, hh, mid, 0, unroll=UNROLL_MID)

    bias = bias_ref[...]

    def last(n2, carry):
        qq = pq_ref[pl.ds(pl.multiple_of(n2 * pqp, 8), 2 * hh), :].astype(BF16)
        y = _dot(h2_ref[n2], qq)
        u = u_ref[pl.ds(n2, hh, stride=PITCH), :]
        m = m_ref[pl.ds(n2, hh, stride=PITCH), :]
        o_ref[pl.ds(n2, hh, stride=PITCH), :] = (m * (y + bias * u)).astype(o_ref.dtype)
        return carry
    lax.fori_loop(0, DFT_B, last, 0, unroll=UNROLL_OUTER)
    for n1 in range(hh):
        o_ref[n1 * PITCH + DFT_B:(n1 + 1) * PITCH, :] = jnp.zeros((PITCH - DFT_B, LANES), o_ref.dtype)


def _long_conv(u_arr, u_blk, m_arr, m_blk, kr, ki, order, bias, tables, l, c):
    g1, f2, f2i, h2 = tables
    hh = l // DFT_B
    nct = c // LANES
    kspec = pl.BlockSpec((None, l, LANES), lambda j: (order, 0, j))
    return pl.pallas_call(
        functools.partial(_long_conv_kernel, l=l),
        out_shape=jax.ShapeDtypeStruct((hh * PITCH, c), F32),
        grid=(nct,),
        in_specs=[pl.BlockSpec((hh * PITCH, LANES), lambda j: (0, u_blk * nct + j)),
                  pl.BlockSpec((hh * PITCH, LANES), lambda j: (0, m_blk * nct + j)),
                  kspec, kspec,
                  pl.BlockSpec((1, LANES), lambda j: (0, j)),
                  pl.BlockSpec(g1.shape, lambda j: (0, 0, 0)),
                  pl.BlockSpec(f2.shape, lambda j: (0, 0)),
                  pl.BlockSpec(f2i.shape, lambda j: (0, 0)),
                  pl.BlockSpec(h2.shape, lambda j: (0, 0, 0))],
        out_specs=pl.BlockSpec((hh * PITCH, LANES), lambda j: (0, j)),
        scratch_shapes=[pltpu.VMEM((DFT_B * _pq_pitch(hh), LANES), F32)],
        compiler_params=_cparams(("parallel",)),
        name="hyena_long_conv",
    )(u_arr, m_arr, kr, ki, bias.reshape(1, c), g1, f2, f2i, h2)


def _merge_kernel(hx_ref, ys_ref, yh_ref, wg1_ref, wg2_ref, w1_ref, w2_ref, gb1_ref, gb2_ref, o_ref):
    hx = hx_ref[...]
    g1 = _sigmoid(_dot_nt(hx, wg1_ref[...]) + gb1_ref[...])
    g2 = _sigmoid(_dot_nt(hx, wg2_ref[...]) + gb2_ref[...])
    yh = yh_ref[:, 0:DFT_B, :].reshape(hx.shape).astype(BF16)
    out = g1 * _dot(ys_ref[...], w1_ref[...]) + g2 * _dot(yh, w2_ref[...])
    o_ref[...] = out.astype(o_ref.dtype)


def _merge(hx, ys, yh, wgt, row0, w1, w2, gate_b, tm=512, tn=512):
    m, d = hx.shape
    nt = d // tn
    a_spec = pl.BlockSpec((tm, d), lambda i, j: (i, 0))
    w_spec = pl.BlockSpec((d, tn), lambda i, j: (0, j))
    gb = gate_b.reshape(1, 2 * d)
    return pl.pallas_call(
        _merge_kernel,
        out_shape=jax.ShapeDtypeStruct((m, d), BF16),
        grid=(m // tm, nt),
        in_specs=[a_spec, a_spec,
                  pl.BlockSpec((tm // DFT_B, PITCH, d), lambda i, j: (i, 0, 0)),
                  pl.BlockSpec((pl.Element(tn), pl.Element(d)),
                               lambda i, j: (pl.multiple_of(row0 + j * tn, ROW_ALIGN), 0)),
                  pl.BlockSpec((pl.Element(tn), pl.Element(d)),
                               lambda i, j: (pl.multiple_of(row0 + (nt + j) * tn, ROW_ALIGN), 0)),
                  w_spec, w_spec,
                  pl.BlockSpec((1, tn), lambda i, j: (0, j)),
                  pl.BlockSpec((1, tn), lambda i, j: (0, nt + j))],
        out_specs=pl.BlockSpec((tm, tn), lambda i, j: (i, j)),
        compiler_params=_cparams(("parallel", "parallel")),
        name="merge",
    )(hx, ys, yh, wgt, wgt, w1, w2, gb, gb)


def _oproj_kernel(a_ref, w_ref, x_ref, gt_ref, o_ref):
    o_ref[...] = x_ref[...] + gt_ref[...] * _dot(a_ref[...], w_ref[...])


def _oproj(a, w, x, gt, tm=1024, tn=512):
    m, d = a.shape
    n = w.shape[1]
    return pl.pallas_call(
        _oproj_kernel,
        out_shape=jax.ShapeDtypeStruct((m, n), F32),
        grid=(m // tm, n // tn),
        in_specs=[pl.BlockSpec((tm, d), lambda i, j: (i, 0)),
                  pl.BlockSpec((d, tn), lambda i, j: (0, j)),
                  pl.BlockSpec((tm, tn), lambda i, j: (i, j)),
                  pl.BlockSpec((1, tn), lambda i, j: (0, j))],
        out_specs=pl.BlockSpec((tm, tn), lambda i, j: (i, j)),
        compiler_params=_cparams(("parallel", "parallel")),
        name="oproj",
    )(a, w, x, gt)


def _pack_bf16_pairs(v):
    half = v.shape[1] // 2
    bits = lax.bitcast_convert_type(v.astype(BF16).astype(F32), jnp.uint32)
    return (bits[:, :half] >> 16) | (bits[:, half:] & jnp.uint32(0xFFFF0000))


def _unpack_bf16_pairs(p):
    lo = lax.bitcast_convert_type(p << 16, F32).astype(BF16)
    hi = lax.bitcast_convert_type(p & jnp.uint32(0xFFFF0000), F32).astype(BF16)
    return lo, hi


def _router_kernel(x_ref, g_ref, sh_ref, sc_ref, wr_ref, br_ref, h_ref, ids_ref, wts_ref):
    x = x_ref[...]
    y = x * lax.rsqrt(jnp.mean(x * x, axis=-1, keepdims=True) + EPS) * g_ref[...]
    h = y * (1.0 + sc_ref[...]) + sh_ref[...]
    h_ref[...] = _pack_bf16_pairs(h)
    logits = _dot_hi(h, wr_ref[...]) + br_ref[...]
    tm = x.shape[0]
    lane = lax.broadcasted_iota(jnp.int32, (tm, LANES), 1)
    neg = jnp.float32(-jnp.inf)
    big = jnp.int32(LANES)
    is_grp = (lane >= MOE_EXPERTS) & (lane < MOE_EXPERTS + MOE_GROUPS)
    gl = jnp.where(is_grp, logits, neg)
    gmax = jnp.max(gl, axis=-1, keepdims=True)
    gidx = jnp.min(jnp.where(gl == gmax, lane, big), axis=-1, keepdims=True) - MOE_EXPERTS
    gw = 1.0 / jnp.sum(jnp.where(is_grp, jnp.exp(logits - gmax), 0.0), axis=-1, keepdims=True)
    in_grp = (lane < MOE_EXPERTS) & ((lane // MOE_PER_GROUP) == gidx)
    el = jnp.where(in_grp, logits, neg)
    m1 = jnp.max(el, axis=-1, keepdims=True)
    i1 = jnp.min(jnp.where(el == m1, lane, big), axis=-1, keepdims=True)
    el2 = jnp.where(lane == i1, neg, el)
    m2 = jnp.max(el2, axis=-1, keepdims=True)
    i2 = jnp.min(jnp.where(el2 == m2, lane, big), axis=-1, keepdims=True)
    e21 = jnp.exp(m2 - m1)
    w1 = gw / (1.0 + e21)
    w2 = gw * e21 / (1.0 + e21)
    ids_ref[...] = jnp.where(lane == 0, i1, jnp.where(lane == 1, i2, -1))
    wts_ref[...] = jnp.where(lane == 0, w1, jnp.where(lane == 1, w2, 0.0))


def _router(x1, g, sh, sc, w_rg, b_rg, w_re, b_re, tm=512):
    m, d = x1.shape
    wr = jnp.zeros((d, LANES), F32).at[:, :MOE_EXPERTS].set(w_re)
    wr = wr.at[:, MOE_EXPERTS:MOE_EXPERTS + MOE_GROUPS].set(w_rg)
    br = jnp.zeros((1, LANES), F32).at[0, :MOE_EXPERTS].set(b_re)
    br = br.at[0, MOE_EXPERTS:MOE_EXPERTS + MOE_GROUPS].set(b_rg)
    row = pl.BlockSpec((1, d), lambda i: (0, 0))
    return pl.pallas_call(
        _router_kernel,
        out_shape=(jax.ShapeDtypeStruct((m, d // 2), jnp.uint32),
                   jax.ShapeDtypeStruct((m, LANES), jnp.int32),
                   jax.ShapeDtypeStruct((m, LANES), F32)),
        grid=(m // tm,),
        in_specs=[pl.BlockSpec((tm, d), lambda i: (i, 0)), row, row, row,
                  pl.BlockSpec((d, LANES), lambda i: (0, 0)),
                  pl.BlockSpec((1, LANES), lambda i: (0, 0))],
        out_specs=(pl.BlockSpec((tm, d // 2), lambda i: (i, 0)),
                   pl.BlockSpec((tm, LANES), lambda i: (i, 0)),
                   pl.BlockSpec((tm, LANES), lambda i: (i, 0))),
        compiler_params=_cparams(("parallel",)),
        name="router",
    )(x1, g.reshape(1, d), sh, sc, wr, br)


def _moe_positions_kernel(ids_ref, pos_ref, meta_ref):
    n = ids_ref.shape[0]
    t = MOE_TILE
    lane = lax.broadcasted_iota(jnp.int32, (t, LANES), 1)

    def onehot(k):
        idt = ids_ref[pl.ds(pl.multiple_of(k * t, t), t), :]
        i1, i2 = idt[:, 0:1], idt[:, 1:2]
        return i1, i2, jnp.where((lane == i1) | (lane == i2), 1.0, 0.0)

    def count(k, acc):
        return acc + jnp.sum(onehot(k)[2], axis=0, keepdims=True)
    total = lax.fori_loop(0, n // t, count, jnp.zeros((1, LANES), F32))
    padded = (((total.astype(jnp.int32) + (t - 1)) // t) * t).astype(F32)
    r128 = lax.broadcasted_iota(jnp.int32, (LANES, LANES), 0)
    c128 = lax.broadcasted_iota(jnp.int32, (LANES, LANES), 1)
    before = jnp.where(r128 < c128, 1.0, 0.0).astype(BF16)
    off = _dot_sel_r(jnp.broadcast_to(padded, (8, LANES)), before)[0:1]
    row = lax.broadcasted_iota(jnp.int32, (t, t), 0)
    col = lax.broadcasted_iota(jnp.int32, (t, t), 1)
    tri = jnp.where(col <= row, 1.0, 0.0).astype(BF16)

    def place(k, seen):
        i1, i2, oh = onehot(k)
        base = off + seen + _dot(tri, oh.astype(BF16)) - oh
        p1 = jnp.sum(jnp.where(lane == i1, base, 0.0), axis=1, keepdims=True)
        p2 = jnp.sum(jnp.where(lane == i2, base, 0.0), axis=1, keepdims=True)
        pos_ref[pl.ds(pl.multiple_of(k * t, t), t), :] = jnp.where(
            lane == 0, p1, jnp.where(lane == 1, p2, 0.0)).astype(jnp.int32)
        return seen + jnp.sum(oh, axis=0, keepdims=True)
    lax.fori_loop(0, n // t, place, jnp.zeros((1, LANES), F32))

    ends = off + padded
    start = (r128 * t).astype(F32)
    done = jnp.where((jnp.broadcast_to(ends, (LANES, LANES)) <= start) & (c128 < MOE_EXPERTS), 1.0, 0.0)
    tile_expert = jnp.minimum(jnp.sum(done, axis=1, keepdims=True), MOE_EXPERTS - 1.0)
    used = jnp.sum(jnp.where(c128[0:1] == MOE_EXPERTS - 1, ends, 0.0), axis=1, keepdims=True) / t
    meta_ref[...] = jnp.where(c128 == 0, tile_expert, jnp.where(c128 == 1, used, 0.0)).astype(jnp.int32)


def _moe_positions(ids):
    m = ids.shape[0]
    return pl.pallas_call(
        _moe_positions_kernel,
        out_shape=(jax.ShapeDtypeStruct((m, LANES), jnp.int32),
                   jax.ShapeDtypeStruct((LANES, LANES), jnp.int32)),
        grid=(1,),
        in_specs=[pl.BlockSpec((m, LANES), lambda i: (0, 0))],
        out_specs=(pl.BlockSpec((m, LANES), lambda i: (0, 0)),
                   pl.BlockSpec((LANES, LANES), lambda i: (0, 0))),
        compiler_params=_cparams(("arbitrary",)),
        name="moe_positions",
    )(ids)


def _moe_dispatch_kernel(p1_ref, p2_ref, h_ref, xs_in_ref, xs_ref, sem):
    del xs_in_ref
    t = h_ref.shape[0]
    base = pl.program_id(0) * t

    def issue(r, carry):
        src = h_ref.at[pl.ds(r, 1), :]
        pltpu.make_async_copy(src, xs_ref.at[pl.ds(p1_ref[base + r], 1), :], sem.at[0]).start()
        pltpu.make_async_copy(src, xs_ref.at[pl.ds(p2_ref[base + r], 1), :], sem.at[1]).start()
        return carry
    lax.fori_loop(0, t, issue, 0, unroll=8)
    pltpu.make_async_copy(h_ref, xs_ref.at[pl.ds(0, t), :], sem.at[0]).wait()
    pltpu.make_async_copy(h_ref, xs_ref.at[pl.ds(0, t), :], sem.at[1]).wait()


def _moe_dispatch(hp, pos1, pos2, n_sorted):
    m, w = hp.shape
    t = MOE_TILE
    zeros = jnp.zeros((n_sorted, w), hp.dtype)
    return pl.pallas_call(
        _moe_dispatch_kernel,
        out_shape=jax.ShapeDtypeStruct((n_sorted, w), hp.dtype),
        grid_spec=pltpu.PrefetchScalarGridSpec(
            num_scalar_prefetch=2,
            grid=(m // t,),
            in_specs=[pl.BlockSpec((t, w), lambda i, p1, p2: (i, 0)),
                      pl.BlockSpec(memory_space=pl.ANY)],
            out_specs=pl.BlockSpec(memory_space=pl.ANY),
            scratch_shapes=[pltpu.SemaphoreType.DMA((2,))]),
        input_output_aliases={3: 0},
        compiler_params=_cparams(("arbitrary",)),
        name="moe_dispatch",
    )(pos1, pos2, hp, zeros)


def _moe_experts_kernel(te_ref, nu_ref, x_ref, wg_ref, wu_ref, wd_ref, y_ref, wg_b, wu_b, wd_b):
    i = pl.program_id(0)
    live = i < nu_ref[0]
    new_expert = (i == 0) | (te_ref[i] != te_ref[jnp.maximum(i - 1, 0)])

    @pl.when(live & new_expert)
    def _():
        wg_b[...] = wg_ref[...].astype(BF16)
        wu_b[...] = wu_ref[...].astype(BF16)
        wd_b[...] = wd_ref[...].astype(BF16)

    @pl.when(live)
    def _():
        x_lo, x_hi = _unpack_bf16_pairs(x_ref[...])
        half = x_lo.shape[1]
        hg = _dot(x_lo, wg_b[0:half, :]) + _dot(x_hi, wg_b[half:, :])
        hu = _dot(x_lo, wu_b[0:half, :]) + _dot(x_hi, wu_b[half:, :])
        y_ref[...] = _dot((_silu(hg) * hu).astype(BF16), wd_b[...])

    @pl.when(jnp.logical_not(live))
    def _():
        y_ref[...] = jnp.zeros_like(y_ref)


def _moe_experts(xs, tile_expert, n_used, wg, wu, wd):
    ns = xs.shape[0]
    ne, d, f = wg.shape
    t = MOE_TILE
    return pl.pallas_call(
        _moe_experts_kernel,
        out_shape=jax.ShapeDtypeStruct((ns, d), F32),
        grid_spec=pltpu.PrefetchScalarGridSpec(
            num_scalar_prefetch=2,
            grid=(ns // t,),
            in_specs=[pl.BlockSpec((t, d // 2), lambda i, te, nu: (i, 0)),
                      pl.BlockSpec((None, d, f), lambda i, te, nu: (te[i], 0, 0)),
                      pl.BlockSpec((None, d, f), lambda i, te, nu: (te[i], 0, 0)),
                      pl.BlockSpec((None, f, d), lambda i, te, nu: (te[i], 0, 0))],
            out_specs=pl.BlockSpec((t, d), lambda i, te, nu: (i, 0)),
            scratch_shapes=[pltpu.VMEM((d, f), BF16), pltpu.VMEM((d, f), BF16),
                            pltpu.VMEM((f, d), BF16)]),
        compiler_params=_cparams(("arbitrary",)),
        name="moe_experts",
    )(tile_expert, n_used, xs, wg, wu, wd)


def _moe_combine_kernel(p1_ref, p2_ref, x_ref, wts_ref, gt_ref, fg_ref, y_ref, o_ref, ya, yb, sem):
    t = x_ref.shape[0]
    i = pl.program_id(0)
    n = pl.num_programs(0)

    def row_copy(src_row, buf, slot, r, which):
        return pltpu.make_async_copy(y_ref.at[pl.ds(src_row, 1), :], buf.at[slot, pl.ds(r, 1), :],
                                     sem.at[which, slot])

    def issue(tile, slot):
        def body(r, carry):
            row_copy(p1_ref[tile * t + r], ya, slot, r, 0).start()
            row_copy(p2_ref[tile * t + r], yb, slot, r, 1).start()
            return carry
        lax.fori_loop(0, t, body, 0, unroll=8)

    @pl.when(i == 0)
    def _():
        issue(0, 0)

    @pl.when(i + 1 < n)
    def _():
        issue(i + 1, (i + 1) % 2)

    slot = i % 2
    pltpu.make_async_copy(y_ref.at[pl.ds(0, t), :], ya.at[slot], sem.at[0, slot]).wait()
    pltpu.make_async_copy(y_ref.at[pl.ds(0, t), :], yb.at[slot], sem.at[1, slot]).wait()
    w = wts_ref[...]
    moe = w[:, 0:1] * ya[slot] + w[:, 1:2] * yb[slot]
    x2 = x_ref[...] + gt_ref[...] * moe
    o_ref[...] = x2 * lax.rsqrt(jnp.mean(x2 * x2, axis=-1, keepdims=True) + EPS) * fg_ref[...]


def _moe_combine(x1, wts, gt, fg, y_sorted, pos1, pos2):
    m, d = x1.shape
    t = MOE_TILE
    row = pl.BlockSpec((1, d), lambda i, p1, p2: (0, 0))
    return pl.pallas_call(
        _moe_combine_kernel,
        out_shape=jax.ShapeDtypeStruct((m, d), F32),
        grid_spec=pltpu.PrefetchScalarGridSpec(
            num_scalar_prefetch=2,
            grid=(m // t,),
            in_specs=[pl.BlockSpec((t, d), lambda i, p1, p2: (i, 0)),
                      pl.BlockSpec((t, LANES), lambda i, p1, p2: (i, 0)),
                      row, row,
                      pl.BlockSpec(memory_space=pl.ANY)],
            out_specs=pl.BlockSpec((t, d), lambda i, p1, p2: (i, 0)),
            scratch_shapes=[pltpu.VMEM((2, t, d), F32), pltpu.VMEM((2, t, d), F32),
                            pltpu.SemaphoreType.DMA((2, 2))]),
        compiler_params=_cparams(("arbitrary",)),
        name="moe_combine",
    )(pos1, pos2, x1, wts, gt, fg.reshape(1, d), y_sorted)


def kernel(x, c, ctx, c_ctx, ada_w, ada_b, norm1_g, w_in, ssd_conv_w, ssd_conv_b, ssd_dt_bias,
           ssd_a_log, ssd_d, ssd_norm_g, w_ssd_out, hy_conv_w, hy_conv_b, hy_f_w1, hy_f_b1,
           hy_f_freq1, hy_f_w2, hy_f_b2, hy_f_freq2, hy_f_w3, hy_bias, w_hy_out, gate_b, w_o,
           norm2_g, moe_w_rg, moe_b_rg, moe_w_re, moe_b_re, moe_w_gate, moe_w_up, moe_w_down,
           final_g):
    bsz, l, d = x.shape
    assert bsz == 1 and ada_w.shape[0] == 1
    lc = ctx.shape[1]
    di = SSD_D_INNER
    xbc_cols = di + 2 * SSD_GN
    col_xbc = di
    col_dt = col_xbc + xbc_cols
    col_hy = col_dt + 2 * SSD_HEADS
    hyw = hy_bias.shape[-1]

    x2d = x[0]
    ctx2d = ctx[0]

    cc = jnp.zeros((8, d), F32).at[0].set(c[0]).at[1].set(c_ctx)
    mods = _adaln(cc, ada_w[0], ada_b[0])
    sh1, sc1, gt1, sh2, sc2, gt2 = [mods[0:1, i * d:(i + 1) * d] for i in range(6)]
    csh1, csc1 = mods[1:2, 0:d], mods[1:2, d:2 * d]

    hx = _normmod(x2d, norm1_g[0], sh1, sc1, tm=512)
    hc = _normmod(ctx2d, norm1_g[0], csh1, csc1, tm=lc)

    w_in_t = w_in[0].T
    col_gate = col_hy + 3 * hyw
    w_in_t = w_in_t.astype(BF16)
    n_dt = 2 * SSD_HEADS
    dt_bias = ssd_dt_bias[0].reshape(-1)

    xbc_c = _proj(hc, w_in_t, col_xbc, xbc_cols, "conv_silu", tm=lc, tn=512, conv_w=ssd_conv_w[0],
                  bias=ssd_conv_b[0], group=lc)
    dt_c = _proj(hc, w_in_t, col_dt, n_dt, "softplus", tm=lc, tn=n_dt, bias=dt_bias)
    alog = ssd_a_log[0]
    alogT = alog.T
    zero_state = jnp.zeros((SSD_GROUPS, SSD_STATE, di // SSD_GROUPS), F32)
    h0f = _ssd(xbc_c, dt_c, dt_c.T, alog, alogT, zero_state, reverse=False, mode="state")
    h0b = _ssd(xbc_c, dt_c, dt_c.T, alog, alogT, zero_state, reverse=True, mode="state")

    z = _proj(hx, w_in_t, 0, di, "none", tm=1024, tn=512)
    xbc = _proj(hx, w_in_t, col_xbc, xbc_cols, "conv_silu", tm=1024, tn=512, conv_w=ssd_conv_w[0],
                bias=ssd_conv_b[0])
    dt = _proj(hx, w_in_t, col_dt, n_dt, "softplus", tm=1024, tn=n_dt, bias=dt_bias)
    u3 = _proj(hx, w_in_t, col_hy, 3 * hyw, "conv", tm=1024, tn=512, conv_w=hy_conv_w[0],
               bias=hy_conv_b[0], pitched=True).reshape(l // DFT_B * PITCH, 3 * hyw)

    dtT = dt.T
    yf = _ssd(xbc, dt, dtT, alog, alogT, h0f, reverse=False, mode="fwd")
    dx = jnp.repeat(ssd_d[0], SSD_HEADDIM).reshape(1, di)
    y_ssd = _ssd(xbc, dt, dtT, alog, alogT, h0b, reverse=True, mode="bwd", yf=yf, z=z, dx=dx,
                 ng=ssd_norm_g[0].reshape(1, di))

    tables = _dft_tables(l)
    hid = _filter_mlp(l, hy_f_w1[0], hy_f_b1[0], hy_f_freq1[0], hy_f_w2[0], hy_f_b2[0],
                      hy_f_freq2[0])
    max_decay = math.log(1e-2) / 0.3
    min_decay = math.log(1e-2) / 1.5
    deltas = jnp.abs(jnp.linspace(min_decay, max_decay, hyw, dtype=F32)).reshape(1, hyw)
    kr, ki = _filter_spec(hid, hy_f_w3[0], deltas, tables[0], tables[1], l, hyw)
    z2 = _long_conv(u3, 0, u3, 1, kr, ki, 0, hy_bias[0, 0], tables, l, hyw)
    y_hy = _long_conv(z2, 0, u3, 2, kr, ki, 1, hy_bias[0, 1], tables, l, hyw)

    merged = _merge(hx, y_ssd, y_hy.reshape(l // DFT_B, PITCH, hyw), w_in_t, col_gate,
                    w_ssd_out[0].astype(BF16), w_hy_out[0].astype(BF16), gate_b[0])
    x1 = _oproj(merged, w_o[0].astype(BF16), x2d, gt1)

    h2, ids, wts = _router(x1, norm2_g[0], sh2, sc2, moe_w_rg[0], moe_b_rg[0], moe_w_re[0],
                           moe_b_re[0])
    pos, meta = _moe_positions(ids)
    pos1, pos2 = pos[:, 0], pos[:, 1]
    n_sorted = 2 * l + MOE_EXPERTS * MOE_TILE
    n_tiles = n_sorted // MOE_TILE
    xs = _moe_dispatch(h2, pos1, pos2, n_sorted)
    ys = _moe_experts(xs, meta[:n_tiles, 0], meta[0, 1:2], moe_w_gate[0], moe_w_up[0],
                      moe_w_down[0])
    out = _moe_combine(x1, wts, gt2, final_g, ys, pos1, pos2)
    return out[None]
```

```python
import functools
import math

import jax
import jax.numpy as jnp
from jax import lax
from jax.experimental import pallas as pl
from jax.experimental.pallas import tpu as pltpu

F32 = jnp.float32
BF16 = jnp.bfloat16

EPS = 1e-6
GRID_W = 64
SSD_HEADS = 32
SSD_HEADDIM = 64
SSD_GROUPS = 4
SSD_STATE = 128
SSD_CHUNK = 128
SSD_D_INNER = SSD_HEADS * SSD_HEADDIM
SSD_GN = SSD_GROUPS * SSD_STATE
HY_BANDS = 16
HY_EMB = 33
HY_HIDDEN = 64
MOE_GROUPS = 4
MOE_PER_GROUP = 4
MOE_EXPERTS = 16

LANES = 128
DFT_B = 128
PITCH = 136
VMEM_LIMIT = 56 * 1024 * 1024
UNROLL_OUTER = 16
UNROLL_MID = 16
ROW_ALIGN = 16
PROJ_TM = 2048
PROJ_SUB_ROWS = 256
MOE_TILE = 256


def _cparams(sem):
    return pltpu.CompilerParams(dimension_semantics=sem, vmem_limit_bytes=VMEM_LIMIT)


def _dot(a, b):
    return jnp.dot(a, b, preferred_element_type=F32)


def _dot_nt(a, bt):
    return lax.dot_general(a, bt, (((1,), (1,)), ((), ())), preferred_element_type=F32)


def _sigmoid(x):
    return 1.0 / (1.0 + jnp.exp(-x))


def _silu(x):
    return x * _sigmoid(x)


def _split3(v):
    b1 = v.astype(BF16)
    r1 = v - b1.astype(F32)
    b2 = r1.astype(BF16)
    b3 = (r1 - b2.astype(F32)).astype(BF16)
    return b1, b2, b3


def _dot_sel_r(v, sel_b):
    b1, b2, b3 = _split3(v)
    return _dot(b1, sel_b) + _dot(b2, sel_b) + _dot(b3, sel_b)


def _dot_sel_l(sel_b, v):
    b1, b2, b3 = _split3(v)
    return _dot(sel_b, b1) + _dot(sel_b, b2) + _dot(sel_b, b3)


def _dot_hi(a, b):
    a1, a2, a3 = _split3(a)
    b1, b2, b3 = _split3(b)
    return (_dot(a1, b1) + (_dot(a1, b2) + _dot(a2, b1))
            + (_dot(a1, b3) + _dot(a2, b2) + _dot(a3, b1)))


def _adaln_kernel(c_ref, w_ref, b_ref, o_ref):
    s = _silu(c_ref[...]).astype(BF16)
    o_ref[...] = _dot(s, w_ref[...].astype(BF16)) + b_ref[...]


def _adaln(cc, w, b, tn=1024):
    m, d = cc.shape
    n = w.shape[1]
    return pl.pallas_call(
        _adaln_kernel,
        out_shape=jax.ShapeDtypeStruct((m, n), F32),
        grid=(n // tn,),
        in_specs=[pl.BlockSpec((m, d), lambda j: (0, 0)),
                  pl.BlockSpec((d, tn), lambda j: (0, j)),
                  pl.BlockSpec((1, tn), lambda j: (0, j))],
        out_specs=pl.BlockSpec((m, tn), lambda j: (0, j)),
        compiler_params=_cparams(("parallel",)),
        name="adaln",
    )(cc, w, b.reshape(1, n))


def _normmod_kernel(x_ref, g_ref, sh_ref, sc_ref, o_ref):
    x = x_ref[...]
    y = x * lax.rsqrt(jnp.mean(x * x, axis=-1, keepdims=True) + EPS) * g_ref[...]
    o_ref[...] = (y * (1.0 + sc_ref[...]) + sh_ref[...]).astype(o_ref.dtype)


def _normmod(x, g, sh, sc, tm):
    m, d = x.shape
    row = pl.BlockSpec((1, d), lambda i: (0, 0))
    return pl.pallas_call(
        _normmod_kernel,
        out_shape=jax.ShapeDtypeStruct((m, d), BF16),
        grid=(m // tm,),
        in_specs=[pl.BlockSpec((tm, d), lambda i: (i, 0)), row, row, row],
        out_specs=pl.BlockSpec((tm, d), lambda i: (i, 0)),
        compiler_params=_cparams(("parallel",)),
        name="normmod",
    )(x, g.reshape(1, d), sh, sc)


def _conv_rows(acc, cw_ref, group):
    tm, tn = acc.shape
    taps = cw_ref.shape[0]
    half = taps // 2
    pos = lax.broadcasted_iota(jnp.int32, (tm, tn), 0) & (group - 1)
    out = acc * cw_ref[half:half + 1, :]
    for k in range(taps):
        d = k - half
        if d == 0:
            continue
        shifted = pltpu.roll(acc, (tm - d) % tm, 0)
        valid = (pos + d >= 0) & (pos + d < group)
        out = out + jnp.where(valid, shifted, 0.0) * cw_ref[k:k + 1, :]
    return out


def _proj_kernel(a_ref, w_ref, *rest, epi, group):
    o_ref = rest[-1]
    tm = a_ref.shape[0]
    sub = max(group, min(tm, PROJ_SUB_ROWS))
    w = w_ref[...].astype(BF16)
    for s in range(tm // sub):
        acc = _dot_nt(a_ref[s * sub:(s + 1) * sub, :], w)
        if epi == "none":
            out = acc
        elif epi == "conv":
            cw_ref, b_ref = rest[0], rest[1]
            out = _conv_rows(acc, cw_ref, group) + b_ref[...]
        elif epi == "conv_silu":
            cw_ref, b_ref = rest[0], rest[1]
            out = _silu(_conv_rows(acc, cw_ref, group) + b_ref[...])
        elif epi == "softplus":
            v = acc + rest[0][...]
            out = jnp.maximum(v, 0.0) + jnp.log(1.0 + jnp.exp(-jnp.abs(v)))
        else:
            raise ValueError(epi)
        if len(o_ref.shape) == 3:
            nb, tn = sub // DFT_B, o_ref.shape[2]
            o_ref[s * nb:(s + 1) * nb, 0:DFT_B, :] = out.reshape(nb, DFT_B, tn).astype(o_ref.dtype)
            o_ref[s * nb:(s + 1) * nb, DFT_B:PITCH, :] = jnp.zeros((nb, PITCH - DFT_B, tn), o_ref.dtype)
        else:
            o_ref[s * sub:(s + 1) * sub, :] = out.astype(o_ref.dtype)


def _proj(a, wt, row0, n, epi, tm, tn, conv_w=None, bias=None, group=GRID_W, out_dtype=F32,
          pitched=False):
    m, k = a.shape
    tm = min(tm, m)
    tn = min(tn, n)
    if pitched:
        out_shape = jax.ShapeDtypeStruct((m // DFT_B, PITCH, n), out_dtype)
        out_spec = pl.BlockSpec((tm // DFT_B, PITCH, tn), lambda i, j: (i, 0, j))
    else:
        out_shape = jax.ShapeDtypeStruct((m, n), out_dtype)
        out_spec = pl.BlockSpec((tm, tn), lambda i, j: (i, j))
    extra, extra_specs = [], []
    if conv_w is not None:
        extra.append(conv_w)
        extra_specs.append(pl.BlockSpec((conv_w.shape[0], tn), lambda i, j: (0, j)))
    if bias is not None:
        extra.append(bias.reshape(1, n))
        extra_specs.append(pl.BlockSpec((1, tn), lambda i, j: (0, j)))
    return pl.pallas_call(
        functools.partial(_proj_kernel, epi=epi, group=group),
        out_shape=out_shape,
        grid=(m // tm, n // tn),
        in_specs=[pl.BlockSpec((tm, k), lambda i, j: (i, 0)),
                  pl.BlockSpec((pl.Element(tn), pl.Element(k)),
                               lambda i, j: (pl.multiple_of(row0 + j * tn, ROW_ALIGN), 0))]
        + extra_specs,
        out_specs=out_spec,
        compiler_params=_cparams(("parallel", "parallel")),
        name="proj_" + epi,
    )(a, wt, *extra)


def _ssd_kernel(xs_ref, b_ref, c_ref, dt_ref, dtT_ref, alog_ref, alogT_ref, h0_ref, *rest,
                reverse, mode):
    if mode == "state":
        hfin_ref, st_ref = rest
    elif mode == "fwd":
        y_ref, st_ref = rest
    else:
        yf_ref, z_ref, dx_ref, ng_ref, y_ref, st_ref = rest
    q = xs_ref.shape[0]
    nh = SSD_HEADS
    rp = SSD_D_INNER // SSD_GROUPS
    step = pl.program_id(0)

    @pl.when(step == 0)
    def _():
        st_ref[...] = h0_ref[...]

    d = 1 if reverse else 0
    dtd = dt_ref[:, d * nh:(d + 1) * nh]
    dtdT = dtT_ref[d * nh:(d + 1) * nh, :]
    a_row = -jnp.exp(alog_ref[d:d + 1, :])
    a_col = -jnp.exp(alogT_ref[:, d:d + 1])
    ad = dtd * a_row
    adT = dtdT * a_col
    row = lax.broadcasted_iota(jnp.int32, (q, q), 0)
    col = lax.broadcasted_iota(jnp.int32, (q, q), 1)
    lower = col <= row
    upper = col >= row
    causal = upper if reverse else lower
    tri = jnp.where(causal, 1.0, 0.0).astype(BF16)
    triT = jnp.where(lower if reverse else upper, 1.0, 0.0).astype(BF16)
    acs = _dot_sel_l(tri, ad)
    acsT = _dot_sel_r(adT, triT)
    tot = jnp.sum(ad, axis=0, keepdims=True)
    dte = jnp.exp(tot - acs)
    eacs = jnp.exp(acs)
    cdec = jnp.exp(tot)
    hsel = lax.broadcasted_iota(jnp.int32, (nh, SSD_D_INNER), 0)
    lsel = lax.broadcasted_iota(jnp.int32, (nh, SSD_D_INNER), 1) // SSD_HEADDIM
    expand = jnp.where(hsel == lsel, 1.0, 0.0).astype(BF16)
    stacked = jnp.concatenate([dtd * dte, eacs, jnp.broadcast_to(cdec, (8, nh))], axis=0)
    s1 = stacked.astype(BF16)
    s2 = (stacked - s1.astype(F32)).astype(BF16)
    exp_all = _dot(s1, expand) + _dot(s2, expand)
    w_x = exp_all[0:q]
    eacs_x = exp_all[q:2 * q]
    cdec_x = exp_all[2 * q:2 * q + 1]

    xs = xs_ref[...]
    xs_b = xs.astype(BF16)
    xdw = (xs * w_x).astype(BF16)
    neg_inf = jnp.float32(-jnp.inf)
    y_groups = []
    for g in range(SSD_GROUPS):
        bg = b_ref[:, g * SSD_STATE:(g + 1) * SSD_STATE]
        st_old = st_ref[g]
        s_new = _dot(bg.T.astype(BF16), xdw[:, g * rp:(g + 1) * rp])
        if mode != "state":
            bg_b = bg.astype(BF16)
            cg_b = c_ref[:, g * SSD_STATE:(g + 1) * SSD_STATE].astype(BF16)
            cb = lax.dot_general(cg_b, bg_b, (((1,), (1,)), ((), ())),
                                 preferred_element_type=F32)
            y_off = _dot(cg_b, st_old.astype(BF16)) * eacs_x[:, g * rp:(g + 1) * rp]
            parts = []
            for r in range(SSD_HEADS // SSD_GROUPS):
                h = g * (SSD_HEADS // SSD_GROUPS) + r
                seg = acs[:, h:h + 1] - acsT[h:h + 1, :]
                dec = jnp.exp(jnp.where(causal, seg, neg_inf))
                mat = (cb * dec * dtdT[h:h + 1, :]).astype(BF16)
                parts.append(_dot(mat, xs_b[:, h * SSD_HEADDIM:(h + 1) * SSD_HEADDIM]))
            y_groups.append(jnp.concatenate(parts, axis=1) + y_off)
        st_ref[g] = st_old * cdec_x[:, g * rp:(g + 1) * rp] + s_new

    if mode == "state":
        @pl.when(step == pl.num_programs(0) - 1)
        def _():
            hfin_ref[...] = st_ref[...]
    elif mode == "fwd":
        for g in range(SSD_GROUPS):
            y_ref[:, g * rp:(g + 1) * rp] = y_groups[g]
    else:
        for g in range(SSD_GROUPS):
            sl = slice(g * rp, (g + 1) * rp)
            y = yf_ref[:, sl] + y_groups[g] + dx_ref[:, sl] * xs[:, sl]
            yz = y * _silu(z_ref[:, sl])
            yz = yz * lax.rsqrt(jnp.mean(yz * yz, axis=-1, keepdims=True) + EPS)
            y_ref[:, sl] = (yz * ng_ref[:, sl]).astype(y_ref.dtype)


def _ssd(xbc, dt, dtT, alog, alogT, h0, *, reverse, mode, yf=None, z=None, dx=None, ng=None):
    l = xbc.shape[0]
    q = SSD_CHUNK
    nc = l // q
    di = SSD_D_INNER
    nb = di // SSD_GN
    if reverse:
        cidx = lambda c: nc - 1 - c
    else:
        cidx = lambda c: c
    st_shape = (SSD_GROUPS, SSD_STATE, di // SSD_GROUPS)
    full3 = pl.BlockSpec(st_shape, lambda c: (0, 0, 0))
    in_specs = [pl.BlockSpec((q, di), lambda c: (cidx(c), 0)),
                pl.BlockSpec((q, SSD_GN), lambda c: (cidx(c), nb)),
                pl.BlockSpec((q, SSD_GN), lambda c: (cidx(c), nb + 1)),
                pl.BlockSpec((q, dt.shape[1]), lambda c: (cidx(c), 0)),
                pl.BlockSpec((dtT.shape[0], q), lambda c: (0, cidx(c))),
                pl.BlockSpec((2, SSD_HEADS), lambda c: (0, 0)),
                pl.BlockSpec((SSD_HEADS, 2), lambda c: (0, 0)),
                full3]
    args = [xbc, xbc, xbc, dt, dtT, alog, alogT, h0]
    wide = pl.BlockSpec((q, di), lambda c: (cidx(c), 0))
    rowspec = pl.BlockSpec((1, di), lambda c: (0, 0))
    if mode == "state":
        out_shape = jax.ShapeDtypeStruct(st_shape, F32)
        out_specs = full3
    elif mode == "fwd":
        out_shape = jax.ShapeDtypeStruct((l, di), F32)
        out_specs = wide
    else:
        in_specs += [wide, wide, rowspec, rowspec]
        args += [yf, z, dx, ng]
        out_shape = jax.ShapeDtypeStruct((l, di), BF16)
        out_specs = wide
    return pl.pallas_call(
        functools.partial(_ssd_kernel, reverse=reverse, mode=mode),
        out_shape=out_shape,
        grid=(nc,),
        in_specs=in_specs,
        out_specs=out_specs,
        scratch_shapes=[pltpu.VMEM(st_shape, F32)],
        compiler_params=_cparams(("arbitrary",)),
        name="ssd_%s_%s" % (mode, "rev" if reverse else "fwd"),
    )(*args)


def _dft_tables(l):
    n = 2 * l
    hh = l // DFT_B
    ka = jnp.arange(hh, dtype=jnp.int32)
    n1 = jnp.arange(hh, dtype=jnp.int32)
    n2 = jnp.arange(DFT_B, dtype=jnp.int32)
    odd = 2 * ka + 1
    ang_a = ((odd[:, None] * (DFT_B * n1)[None, :]) % (2 * n)).astype(F32) * (math.pi / n)
    ang_b = ((n2[:, None] * odd[None, :]) % (2 * n)).astype(F32) * (math.pi / n)
    ca, sa = jnp.cos(ang_a)[None, :, :], jnp.sin(ang_a)[None, :, :]
    cb, sb = jnp.cos(ang_b)[:, :, None], jnp.sin(ang_b)[:, :, None]
    cs = ca * cb - sa * sb
    sn = sa * cb + ca * sb
    g1 = jnp.concatenate([cs, -sn], axis=1).astype(BF16)
    scale = 2.0 / n
    h2 = jnp.concatenate([jnp.swapaxes(cs, 1, 2), -jnp.swapaxes(sn, 1, 2)], axis=2) * scale
    kb = jnp.arange(DFT_B, dtype=jnp.int32)
    ph2 = (kb[:, None] * n2[None, :]) % DFT_B
    ang2 = ph2.astype(F32) * (2.0 * math.pi / DFT_B)
    fr, fi = jnp.cos(ang2), -jnp.sin(ang2)
    f2 = jnp.concatenate([jnp.concatenate([fr, -fi], axis=1),
                          jnp.concatenate([fi, fr], axis=1)], axis=0).astype(BF16)
    f2i = jnp.concatenate([jnp.concatenate([fr, fi], axis=1),
                           jnp.concatenate([-fi, fr], axis=1)], axis=0).astype(BF16)
    return g1, f2, f2i, h2.astype(BF16)


def _filter_mlp_kernel(bands_ref, w1_ref, b1_ref, f1_ref, w2_ref, b2_ref, f2_ref, o_ref, *, l):
    tm = o_ref.shape[0]
    base = pl.program_id(0) * tm
    idx = (lax.broadcasted_iota(jnp.int32, (tm, LANES), 0) + base).astype(F32)
    lane = lax.broadcasted_iota(jnp.int32, (tm, LANES), 1)
    t = idx * (1.0 / (l - 1))
    w = idx * (2.0 * math.pi / l)
    arg = bands_ref[...] * w
    feats = jnp.where(lane == 0, t,
                      jnp.where(lane <= HY_BANDS, jnp.cos(arg),
                                jnp.where(lane < HY_EMB, -jnp.sin(arg), 0.0)))
    h = jnp.sin(f1_ref[...] * (_dot_hi(feats, w1_ref[...]) + b1_ref[...]))
    h = jnp.sin(f2_ref[...] * (_dot_hi(h, w2_ref[...]) + b2_ref[...]))
    o_ref[...] = h.astype(o_ref.dtype)


def _filter_mlp(l, w1, b1, f1, w2, b2, f2, tm=1024):
    tm = min(tm, l)
    bands = jnp.linspace(1e-4, HY_BANDS - 1, HY_BANDS, dtype=F32)
    bands_row = jnp.zeros((1, LANES), F32).at[0, 1:1 + HY_BANDS].set(bands)
    bands_row = bands_row.at[0, 1 + HY_BANDS:HY_EMB].set(bands)
    w1p = jnp.zeros((LANES, HY_HIDDEN), F32).at[:HY_EMB].set(w1)
    full = lambda a: pl.BlockSpec(a.shape, lambda i: (0,) * a.ndim)
    args = [bands_row, w1p, b1.reshape(1, -1), f1.reshape(1, -1), w2, b2.reshape(1, -1),
            f2.reshape(1, -1)]
    return pl.pallas_call(
        functools.partial(_filter_mlp_kernel, l=l),
        out_shape=jax.ShapeDtypeStruct((l, HY_HIDDEN), BF16),
        grid=(l // tm,),
        in_specs=[full(a) for a in args],
        out_specs=pl.BlockSpec((tm, HY_HIDDEN), lambda i: (i, 0)),
        compiler_params=_cparams(("parallel",)),
        name="hyena_filter_mlp",
    )(*args)


def _pq_pitch(hh):
    return 2 * hh + 8


def _fwd_stage1(src_ref, g1_ref, pq_ref, hh):
    pqp = _pq_pitch(hh)

    def body(n2, carry):
        rows = src_ref[pl.ds(n2, hh, stride=PITCH), :]
        p = _dot(g1_ref[n2], rows.astype(BF16))
        pq_ref[pl.ds(pl.multiple_of(n2 * pqp, 8), 2 * hh), :] = p
        return carry
    lax.fori_loop(0, DFT_B, body, 0, unroll=UNROLL_OUTER)


def _load_spectrum_rows(pq_ref, ka, hh):
    pqp = _pq_pitch(hh)
    return jnp.concatenate([pq_ref[pl.ds(ka, DFT_B, stride=pqp), :],
                            pq_ref[pl.ds(hh + ka, DFT_B, stride=pqp), :]], axis=0)


def _filter_spec_kernel(hid_ref, wf_ref, wb_ref, dl_ref, g1_ref, f2_ref, kr_ref, ki_ref,
                        s_ref, d_ref, pq_ref, *, l):
    hh = l // DFT_B
    t = lax.broadcasted_iota(jnp.int32, (l, LANES), 0).astype(F32) * (1.0 / (l - 1))
    dec = jnp.exp(-t * dl_ref[...])
    hid = hid_ref[...]
    hf = _dot(hid, wf_ref[...].astype(BF16)) * dec
    hb = _dot(hid, wb_ref[...].astype(BF16)) * dec
    first = lax.broadcasted_iota(jnp.int32, (l, LANES), 0) == 0
    hb = jnp.where(first, 0.0, hb)
    inv = 1.0 / (jnp.sum(jnp.abs(hf), axis=0, keepdims=True)
                 + jnp.sum(jnp.abs(hb), axis=0, keepdims=True))
    hs = hf + hb
    hd = hf - hb
    for n1 in range(hh):
        s_ref[n1 * PITCH:n1 * PITCH + DFT_B, :] = hs[n1 * DFT_B:(n1 + 1) * DFT_B]
        d_ref[n1 * PITCH:n1 * PITCH + DFT_B, :] = hd[n1 * DFT_B:(n1 + 1) * DFT_B]
    f2 = f2_ref[...]
    for src_ref, out_ref, lo in ((s_ref, kr_ref, 0), (d_ref, ki_ref, DFT_B)):
        _fwd_stage1(src_ref, g1_ref, pq_ref, hh)

        def body(ka, carry, out_ref=out_ref, lo=lo):
            pp = _load_spectrum_rows(pq_ref, ka, hh).astype(BF16)
            x = _dot(f2[lo:lo + DFT_B, :], pp)
            out_ref[pl.ds(pl.multiple_of(ka * DFT_B, DFT_B), DFT_B), :] = (x * inv).astype(
                out_ref.dtype)
            return carry
        lax.fori_loop(0, hh, body, 0, unroll=UNROLL_MID)


def _filter_spec(hid, w3, deltas, g1, f2, l, c):
    hh = l // DFT_B
    nct = c // LANES
    orders = w3.shape[1] // (2 * c)
    spec = pl.BlockSpec((None, l, LANES), lambda o, j: (o, 0, j))
    out_sd = jax.ShapeDtypeStruct((orders, l, c), BF16)
    return pl.pallas_call(
        functools.partial(_filter_spec_kernel, l=l),
        out_shape=(out_sd, out_sd),
        grid=(orders, nct),
        in_specs=[pl.BlockSpec((l, HY_HIDDEN), lambda o, j: (0, 0)),
                  pl.BlockSpec((HY_HIDDEN, LANES), lambda o, j: (0, (2 * o) * nct + j)),
                  pl.BlockSpec((HY_HIDDEN, LANES), lambda o, j: (0, (2 * o + 1) * nct + j)),
                  pl.BlockSpec((1, LANES), lambda o, j: (0, j)),
                  pl.BlockSpec(g1.shape, lambda o, j: (0, 0, 0)),
                  pl.BlockSpec(f2.shape, lambda o, j: (0, 0))],
        out_specs=(spec, spec),
        scratch_shapes=[pltpu.VMEM((hh * PITCH, LANES), F32)] * 2
        + [pltpu.VMEM((DFT_B * _pq_pitch(hh), LANES), F32)],
        compiler_params=_cparams(("parallel", "parallel")),
        name="hyena_filter_spec",
    )(hid, w3, w3, deltas, g1, f2)


def _long_conv_kernel(u_ref, m_ref, kr_ref, ki_ref, bias_ref, g1_ref, f2_ref, f2i_ref, h2_ref,
                      o_ref, pq_ref, *, l):
    hh = l // DFT_B
    pqp = _pq_pitch(hh)
    _fwd_stage1(u_ref, g1_ref, pq_ref, hh)
    f2 = f2_ref[...]
    f2i = f2i_ref[...]

    def mid(ka, carry):
        koff = pl.multiple_of(ka * DFT_B, DFT_B)
        x = _dot(f2, _load_spectrum_rows(pq_ref, ka, hh).astype(BF16))
        xr, xi = x[:DFT_B], x[DFT_B:]
        kr = kr_ref[pl.ds(koff, DFT_B), :].astype(F32)
        ki = ki_ref[pl.ds(koff, DFT_B), :].astype(F32)
        yy = jnp.concatenate([xr * kr - xi * ki, xr * ki + xi * kr], axis=0).astype(BF16)
        qq = _dot(f2i, yy)
        pq_ref[pl.ds(ka, DFT_B, stride=pqp), :] = qq[:DFT_B]
        pq_ref[pl.ds(hh + ka, DFT_B, stride=pqp), :] = qq[DFT_B:]
        return carry
    lax.fori_loop(0, hh, mid, 0, unroll=UNROLL_MID)

    bias = bias_ref[...]

    def last(n2, carry):
        qq = pq_ref[pl.ds(pl.multiple_of(n2 * pqp, 8), 2 * hh), :].astype(BF16)
        y = _dot(h2_ref[n2], qq)
        u = u_ref[pl.ds(n2, hh, stride=PITCH), :]
        m = m_ref[pl.ds(n2, hh, stride=PITCH), :]
        o_ref[pl.ds(n2, hh, stride=PITCH), :] = (m * (y + bias * u)).astype(o_ref.dtype)
        return carry
    lax.fori_loop(0, DFT_B, last, 0, unroll=UNROLL_OUTER)
    for n1 in range(hh):
        o_ref[n1 * PITCH + DFT_B:(n1 + 1) * PITCH, :] = jnp.zeros((PITCH - DFT_B, LANES), o_ref.dtype)


def _long_conv(u_arr, u_blk, m_arr, m_blk, kr, ki, order, bias, tables, l, c):
    g1, f2, f2i, h2 = tables
    hh = l // DFT_B
    nct = c // LANES
    kspec = pl.BlockSpec((None, l, LANES), lambda j: (order, 0, j))
    return pl.pallas_call(
        functools.partial(_long_conv_kernel, l=l),
        out_shape=jax.ShapeDtypeStruct((hh * PITCH, c), F32),
        grid=(nct,),
        in_specs=[pl.BlockSpec((hh * PITCH, LANES), lambda j: (0, u_blk * nct + j)),
                  pl.BlockSpec((hh * PITCH, LANES), lambda j: (0, m_blk * nct + j)),
                  kspec, kspec,
                  pl.BlockSpec((1, LANES), lambda j: (0, j)),
                  pl.BlockSpec(g1.shape, lambda j: (0, 0, 0)),
                  pl.BlockSpec(f2.shape, lambda j: (0, 0)),
                  pl.BlockSpec(f2i.shape, lambda j: (0, 0)),
                  pl.BlockSpec(h2.shape, lambda j: (0, 0, 0))],
        out_specs=pl.BlockSpec((hh * PITCH, LANES), lambda j: (0, j)),
        scratch_shapes=[pltpu.VMEM((DFT_B * _pq_pitch(hh), LANES), F32)],
        compiler_params=_cparams(("parallel",)),
        name="hyena_long_conv",
    )(u_arr, m_arr, kr, ki, bias.reshape(1, c), g1, f2, f2i, h2)


def _merge_kernel(hx_ref, ys_ref, yh_ref, wg1_ref, wg2_ref, w1_ref, w2_ref, gb1_ref, gb2_ref, o_ref):
    hx = hx_ref[...]
    g1 = _sigmoid(_dot_nt(hx, wg1_ref[...]) + gb1_ref[...])
    g2 = _sigmoid(_dot_nt(hx, wg2_ref[...]) + gb2_ref[...])
    yh = yh_ref[:, 0:DFT_B, :].reshape(hx.shape).astype(BF16)
    out = g1 * _dot(ys_ref[...], w1_ref[...]) + g2 * _dot(yh, w2_ref[...])
    o_ref[...] = out.astype(o_ref.dtype)


def _merge(hx, ys, yh, wgt, row0, w1, w2, gate_b, tm=512, tn=512):
    m, d = hx.shape
    nt = d // tn
    a_spec = pl.BlockSpec((tm, d), lambda i, j: (i, 0))
    w_spec = pl.BlockSpec((d, tn), lambda i, j: (0, j))
    gb = gate_b.reshape(1, 2 * d)
    return pl.pallas_call(
        _merge_kernel,
        out_shape=jax.ShapeDtypeStruct((m, d), BF16),
        grid=(m // tm, nt),
        in_specs=[a_spec, a_spec,
                  pl.BlockSpec((tm // DFT_B, PITCH, d), lambda i, j: (i, 0, 0)),
                  pl.BlockSpec((pl.Element(tn), pl.Element(d)),
                               lambda i, j: (pl.multiple_of(row0 + j * tn, ROW_ALIGN), 0)),
                  pl.BlockSpec((pl.Element(tn), pl.Element(d)),
                               lambda i, j: (pl.multiple_of(row0 + (nt + j) * tn, ROW_ALIGN), 0)),
                  w_spec, w_spec,
                  pl.BlockSpec((1, tn), lambda i, j: (0, j)),
                  pl.BlockSpec((1, tn), lambda i, j: (0, nt + j))],
        out_specs=pl.BlockSpec((tm, tn), lambda i, j: (i, j)),
        compiler_params=_cparams(("parallel", "parallel")),
        name="merge",
    )(hx, ys, yh, wgt, wgt, w1, w2, gb, gb)


def _oproj_kernel(a_ref, w_ref, x_ref, gt_ref, o_ref):
    o_ref[...] = x_ref[...] + gt_ref[...] * _dot(a_ref[...], w_ref[...])


def _oproj(a, w, x, gt, tm=1024, tn=512):
    m, d = a.shape
    n = w.shape[1]
    return pl.pallas_call(
        _oproj_kernel,
        out_shape=jax.ShapeDtypeStruct((m, n), F32),
        grid=(m // tm, n // tn),
        in_specs=[pl.BlockSpec((tm, d), lambda i, j: (i, 0)),
                  pl.BlockSpec((d, tn), lambda i, j: (0, j)),
                  pl.BlockSpec((tm, tn), lambda i, j: (i, j)),
                  pl.BlockSpec((1, tn), lambda i, j: (0, j))],
        out_specs=pl.BlockSpec((tm, tn), lambda i, j: (i, j)),
        compiler_params=_cparams(("parallel", "parallel")),
        name="oproj",
    )(a, w, x, gt)


def _pack_bf16_pairs(v):
    half = v.shape[1] // 2
    bits = lax.bitcast_convert_type(v.astype(BF16).astype(F32), jnp.uint32)
    return (bits[:, :half] >> 16) | (bits[:, half:] & jnp.uint32(0xFFFF0000))


def _unpack_bf16_pairs(p):
    lo = lax.bitcast_convert_type(p << 16, F32).astype(BF16)
    hi = lax.bitcast_convert_type(p & jnp.uint32(0xFFFF0000), F32).astype(BF16)
    return lo, hi


def _router_kernel(x_ref, g_ref, sh_ref, sc_ref, wr_ref, br_ref, h_ref, ids_ref, wts_ref):
    x = x_ref[...]
    y = x * lax.rsqrt(jnp.mean(x * x, axis=-1, keepdims=True) + EPS) * g_ref[...]
    h = y * (1.0 + sc_ref[...]) + sh_ref[...]
    h_ref[...] = _pack_bf16_pairs(h)
    logits = _dot_hi(h, wr_ref[...]) + br_ref[...]
    tm = x.shape[0]
    lane = lax.broadcasted_iota(jnp.int32, (tm, LANES), 1)
    neg = jnp.float32(-jnp.inf)
    big = jnp.int32(LANES)
    is_grp = (lane >= MOE_EXPERTS) & (lane < MOE_EXPERTS + MOE_GROUPS)
    gl = jnp.where(is_grp, logits, neg)
    gmax = jnp.max(gl, axis=-1, keepdims=True)
    gidx = jnp.min(jnp.where(gl == gmax, lane, big), axis=-1, keepdims=True) - MOE_EXPERTS
    gw = 1.0 / jnp.sum(jnp.where(is_grp, jnp.exp(logits - gmax), 0.0), axis=-1, keepdims=True)
    in_grp = (lane < MOE_EXPERTS) & ((lane // MOE_PER_GROUP) == gidx)
    el = jnp.where(in_grp, logits, neg)
    m1 = jnp.max(el, axis=-1, keepdims=True)
    i1 = jnp.min(jnp.where(el == m1, lane, big), axis=-1, keepdims=True)
    el2 = jnp.where(lane == i1, neg, el)
    m2 = jnp.max(el2, axis=-1, keepdims=True)
    i2 = jnp.min(jnp.where(el2 == m2, lane, big), axis=-1, keepdims=True)
    e21 = jnp.exp(m2 - m1)
    w1 = gw / (1.0 + e21)
    w2 = gw * e21 / (1.0 + e21)
    ids_ref[...] = jnp.where(lane == 0, i1, jnp.where(lane == 1, i2, -1))
    wts_ref[...] = jnp.where(lane == 0, w1, jnp.where(lane == 1, w2, 0.0))


def _router(x1, g, sh, sc, w_rg, b_rg, w_re, b_re, tm=512):
    m, d = x1.shape
    wr = jnp.zeros((d, LANES), F32).at[:, :MOE_EXPERTS].set(w_re)
    wr = wr.at[:, MOE_EXPERTS:MOE_EXPERTS + MOE_GROUPS].set(w_rg)
    br = jnp.zeros((1, LANES), F32).at[0, :MOE_EXPERTS].set(b_re)
    br = br.at[0, MOE_EXPERTS:MOE_EXPERTS + MOE_GROUPS].set(b_rg)
    row = pl.BlockSpec((1, d), lambda i: (0, 0))
    return pl.pallas_call(
        _router_kernel,
        out_shape=(jax.ShapeDtypeStruct((m, d // 2), jnp.uint32),
                   jax.ShapeDtypeStruct((m, LANES), jnp.int32),
                   jax.ShapeDtypeStruct((m, LANES), F32)),
        grid=(m // tm,),
        in_specs=[pl.BlockSpec((tm, d), lambda i: (i, 0)), row, row, row,
                  pl.BlockSpec((d, LANES), lambda i: (0, 0)),
                  pl.BlockSpec((1, LANES), lambda i: (0, 0))],
        out_specs=(pl.BlockSpec((tm, d // 2), lambda i: (i, 0)),
                   pl.BlockSpec((tm, LANES), lambda i: (i, 0)),
                   pl.BlockSpec((tm, LANES), lambda i: (i, 0))),
        compiler_params=_cparams(("parallel",)),
        name="router",
    )(x1, g.reshape(1, d), sh, sc, wr, br)


def _moe_positions_kernel(ids_ref, pos_ref, meta_ref):
    n = ids_ref.shape[0]
    t = MOE_TILE
    lane = lax.broadcasted_iota(jnp.int32, (t, LANES), 1)

    def onehot(k):
        idt = ids_ref[pl.ds(pl.multiple_of(k * t, t), t), :]
        i1, i2 = idt[:, 0:1], idt[:, 1:2]
        return i1, i2, jnp.where((lane == i1) | (lane == i2), 1.0, 0.0)

    def count(k, acc):
        return acc + jnp.sum(onehot(k)[2], axis=0, keepdims=True)
    total = lax.fori_loop(0, n // t, count, jnp.zeros((1, LANES), F32))
    padded = (((total.astype(jnp.int32) + (t - 1)) // t) * t).astype(F32)
    r128 = lax.broadcasted_iota(jnp.int32, (LANES, LANES), 0)
    c128 = lax.broadcasted_iota(jnp.int32, (LANES, LANES), 1)
    before = jnp.where(r128 < c128, 1.0, 0.0).astype(BF16)
    off = _dot_sel_r(jnp.broadcast_to(padded, (8, LANES)), before)[0:1]
    row = lax.broadcasted_iota(jnp.int32, (t, t), 0)
    col = lax.broadcasted_iota(jnp.int32, (t, t), 1)
    tri = jnp.where(col <= row, 1.0, 0.0).astype(BF16)

    def place(k, seen):
        i1, i2, oh = onehot(k)
        base = off + seen + _dot(tri, oh.astype(BF16)) - oh
        p1 = jnp.sum(jnp.where(lane == i1, base, 0.0), axis=1, keepdims=True)
        p2 = jnp.sum(jnp.where(lane == i2, base, 0.0), axis=1, keepdims=True)
        pos_ref[pl.ds(pl.multiple_of(k * t, t), t), :] = jnp.where(
            lane == 0, p1, jnp.where(lane == 1, p2, 0.0)).astype(jnp.int32)
        return seen + jnp.sum(oh, axis=0, keepdims=True)
    lax.fori_loop(0, n // t, place, jnp.zeros((1, LANES), F32))

    ends = off + padded
    start = (r128 * t).astype(F32)
    done = jnp.where((jnp.broadcast_to(ends, (LANES, LANES)) <= start) & (c128 < MOE_EXPERTS), 1.0, 0.0)
    tile_expert = jnp.minimum(jnp.sum(done, axis=1, keepdims=True), MOE_EXPERTS - 1.0)
    used = jnp.sum(jnp.where(c128[0:1] == MOE_EXPERTS - 1, ends, 0.0), axis=1, keepdims=True) / t
    lanef = c128.astype(F32)
    later = (lanef > tile_expert) & (jnp.broadcast_to(padded, (LANES, LANES)) > 0.0) & (c128 < MOE_EXPERTS)
    nxt = jnp.min(jnp.where(later, lanef, float(LANES)), axis=1, keepdims=True)
    nxt = jnp.where(nxt >= float(LANES), -1.0, nxt)
    meta_ref[...] = jnp.where(c128 == 0, tile_expert,
                              jnp.where(c128 == 1, used, jnp.where(c128 == 2, nxt, 0.0))).astype(jnp.int32)


def _moe_positions(ids):
    m = ids.shape[0]
    return pl.pallas_call(
        _moe_positions_kernel,
        out_shape=(jax.ShapeDtypeStruct((m, LANES), jnp.int32),
                   jax.ShapeDtypeStruct((LANES, LANES), jnp.int32)),
        grid=(1,),
        in_specs=[pl.BlockSpec((m, LANES), lambda i: (0, 0))],
        out_specs=(pl.BlockSpec((m, LANES), lambda i: (0, 0)),
                   pl.BlockSpec((LANES, LANES), lambda i: (0, 0))),
        compiler_params=_cparams(("arbitrary",)),
        name="moe_positions",
    )(ids)


def _moe_dispatch_kernel(p1_ref, p2_ref, h_ref, xs_in_ref, xs_ref, sem):
    del xs_in_ref
    t = h_ref.shape[0]
    base = pl.program_id(0) * t

    def issue(r, carry):
        src = h_ref.at[pl.ds(r, 1), :]
        pltpu.make_async_copy(src, xs_ref.at[pl.ds(p1_ref[base + r], 1), :], sem.at[0]).start()
        pltpu.make_async_copy(src, xs_ref.at[pl.ds(p2_ref[base + r], 1), :], sem.at[1]).start()
        return carry
    lax.fori_loop(0, t, issue, 0, unroll=8)
    pltpu.make_async_copy(h_ref, xs_ref.at[pl.ds(0, t), :], sem.at[0]).wait()
    pltpu.make_async_copy(h_ref, xs_ref.at[pl.ds(0, t), :], sem.at[1]).wait()


def _moe_dispatch(hp, pos1, pos2, n_sorted):
    m, w = hp.shape
    t = MOE_TILE
    zeros = jnp.zeros((n_sorted, w), hp.dtype)
    return pl.pallas_call(
        _moe_dispatch_kernel,
        out_shape=jax.ShapeDtypeStruct((n_sorted, w), hp.dtype),
        grid_spec=pltpu.PrefetchScalarGridSpec(
            num_scalar_prefetch=2,
            grid=(m // t,),
            in_specs=[pl.BlockSpec((t, w), lambda i, p1, p2: (i, 0)),
                      pl.BlockSpec(memory_space=pl.ANY)],
            out_specs=pl.BlockSpec(memory_space=pl.ANY),
            scratch_shapes=[pltpu.SemaphoreType.DMA((2,))]),
        input_output_aliases={3: 0},
        compiler_params=_cparams(("arbitrary",)),
        name="moe_dispatch",
    )(pos1, pos2, hp, zeros)


def _moe_experts_kernel(te_ref, nu_ref, nx_ref, x_ref, wg_hbm, wu_hbm, wd_hbm, y_ref,
                        wg_f, wu_f, wd_f, wg_b, wu_b, wd_b, slot_ref, sem):
    i = pl.program_id(0)
    live = i < nu_ref[0]
    new_expert = (i == 0) | (te_ref[i] != te_ref[jnp.maximum(i - 1, 0)])

    def copies(e, s):
        return (pltpu.make_async_copy(wg_hbm.at[e], wg_f.at[s], sem.at[s, 0]),
                pltpu.make_async_copy(wu_hbm.at[e], wu_f.at[s], sem.at[s, 1]),
                pltpu.make_async_copy(wd_hbm.at[e], wd_f.at[s], sem.at[s, 2]))

    @pl.when(i == 0)
    def _():
        slot_ref[0] = 0
        for cp in copies(te_ref[0], 0):
            cp.start()

    @pl.when(live & new_expert)
    def _():
        s = slot_ref[0]
        for cp in copies(te_ref[i], s):
            cp.wait()
        wg_b[...] = wg_f[s].astype(BF16)
        wu_b[...] = wu_f[s].astype(BF16)
        wd_b[...] = wd_f[s].astype(BF16)
        slot_ref[0] = 1 - s

        @pl.when(nx_ref[i] >= 0)
        def _():
            for cp in copies(nx_ref[i], 1 - s):
                cp.start()

    @pl.when(live)
    def _():
        x_lo, x_hi = _unpack_bf16_pairs(x_ref[...])
        half = x_lo.shape[1]
        hg = _dot(x_lo, wg_b[0:half, :]) + _dot(x_hi, wg_b[half:, :])
        hu = _dot(x_lo, wu_b[0:half, :]) + _dot(x_hi, wu_b[half:, :])
        y_ref[...] = _dot((_silu(hg) * hu).astype(BF16), wd_b[...])

    @pl.when(jnp.logical_not(live))
    def _():
        y_ref[...] = jnp.zeros_like(y_ref)


def _moe_experts(xs, tile_expert, n_used, next_expert, wg, wu, wd):
    ns = xs.shape[0]
    ne, d, f = wg.shape
    t = MOE_TILE
    hbm = pl.BlockSpec(memory_space=pl.ANY)
    return pl.pallas_call(
        _moe_experts_kernel,
        out_shape=jax.ShapeDtypeStruct((ns, d), F32),
        grid_spec=pltpu.PrefetchScalarGridSpec(
            num_scalar_prefetch=3,
            grid=(ns // t,),
            in_specs=[pl.BlockSpec((t, d // 2), lambda i, te, nu, nx: (i, 0)), hbm, hbm, hbm],
            out_specs=pl.BlockSpec((t, d), lambda i, te, nu, nx: (i, 0)),
            scratch_shapes=[pltpu.VMEM((2, d, f), F32), pltpu.VMEM((2, d, f), F32),
                            pltpu.VMEM((2, f, d), F32),
                            pltpu.VMEM((d, f), BF16), pltpu.VMEM((d, f), BF16),
                            pltpu.VMEM((f, d), BF16),
                            pltpu.SMEM((1,), jnp.int32), pltpu.SemaphoreType.DMA((2, 3))]),
        compiler_params=_cparams(("arbitrary",)),
        name="moe_experts",
    )(tile_expert, n_used, next_expert, xs, wg, wu, wd)


def _moe_combine_kernel(p1_ref, p2_ref, x_ref, wts_ref, gt_ref, fg_ref, y_ref, o_ref, ya, yb, sem):
    t = x_ref.shape[0]
    i = pl.program_id(0)
    n = pl.num_programs(0)

    def row_copy(src_row, buf, slot, r, which):
        return pltpu.make_async_copy(y_ref.at[pl.ds(src_row, 1), :], buf.at[slot, pl.ds(r, 1), :],
                                     sem.at[which, slot])

    def issue(tile, slot):
        def body(r, carry):
            row_copy(p1_ref[tile * t + r], ya, slot, r, 0).start()
            row_copy(p2_ref[tile * t + r], yb, slot, r, 1).start()
            return carry
        lax.fori_loop(0, t, body, 0, unroll=8)

    @pl.when(i == 0)
    def _():
        issue(0, 0)

    @pl.when(i + 1 < n)
    def _():
        issue(i + 1, (i + 1) % 2)

    slot = i % 2
    pltpu.make_async_copy(y_ref.at[pl.ds(0, t), :], ya.at[slot], sem.at[0, slot]).wait()
    pltpu.make_async_copy(y_ref.at[pl.ds(0, t), :], yb.at[slot], sem.at[1, slot]).wait()
    w = wts_ref[...]
    moe = w[:, 0:1] * ya[slot] + w[:, 1:2] * yb[slot]
    x2 = x_ref[...] + gt_ref[...] * moe
    o_ref[...] = x2 * lax.rsqrt(jnp.mean(x2 * x2, axis=-1, keepdims=True) + EPS) * fg_ref[...]


def _moe_combine(x1, wts, gt, fg, y_sorted, pos1, pos2):
    m, d = x1.shape
    t = MOE_TILE
    row = pl.BlockSpec((1, d), lambda i, p1, p2: (0, 0))
    return pl.pallas_call(
        _moe_combine_kernel,
        out_shape=jax.ShapeDtypeStruct((m, d), F32),
        grid_spec=pltpu.PrefetchScalarGridSpec(
            num_scalar_prefetch=2,
            grid=(m // t,),
            in_specs=[pl.BlockSpec((t, d), lambda i, p1, p2: (i, 0)),
                      pl.BlockSpec((t, LANES), lambda i, p1, p2: (i, 0)),
                      row, row,
                      pl.BlockSpec(memory_space=pl.ANY)],
            out_specs=pl.BlockSpec((t, d), lambda i, p1, p2: (i, 0)),
            scratch_shapes=[pltpu.VMEM((2, t, d), F32), pltpu.VMEM((2, t, d), F32),
                            pltpu.SemaphoreType.DMA((2, 2))]),
        compiler_params=_cparams(("arbitrary",)),
        name="moe_combine",
    )(pos1, pos2, x1, wts, gt, fg.reshape(1, d), y_sorted)


def kernel(x, c, ctx, c_ctx, ada_w, ada_b, norm1_g, w_in, ssd_conv_w, ssd_conv_b, ssd_dt_bias,
           ssd_a_log, ssd_d, ssd_norm_g, w_ssd_out, hy_conv_w, hy_conv_b, hy_f_w1, hy_f_b1,
           hy_f_freq1, hy_f_w2, hy_f_b2, hy_f_freq2, hy_f_w3, hy_bias, w_hy_out, gate_b, w_o,
           norm2_g, moe_w_rg, moe_b_rg, moe_w_re, moe_b_re, moe_w_gate, moe_w_up, moe_w_down,
           final_g):
    bsz, l, d = x.shape
    assert bsz == 1 and ada_w.shape[0] == 1
    lc = ctx.shape[1]
    di = SSD_D_INNER
    xbc_cols = di + 2 * SSD_GN
    col_xbc = di
    col_dt = col_xbc + xbc_cols
    col_hy = col_dt + 2 * SSD_HEADS
    hyw = hy_bias.shape[-1]

    x2d = x[0]
    ctx2d = ctx[0]

    cc = jnp.zeros((8, d), F32).at[0].set(c[0]).at[1].set(c_ctx)
    mods = _adaln(cc, ada_w[0], ada_b[0])
    sh1, sc1, gt1, sh2, sc2, gt2 = [mods[0:1, i * d:(i + 1) * d] for i in range(6)]
    csh1, csc1 = mods[1:2, 0:d], mods[1:2, d:2 * d]

    hx = _normmod(x2d, norm1_g[0], sh1, sc1, tm=512)
    hc = _normmod(ctx2d, norm1_g[0], csh1, csc1, tm=lc)

    w_in_t = w_in[0].T
    col_gate = col_hy + 3 * hyw
    wt_gate = w_in_t[col_gate:].astype(BF16)
    n_dt = 2 * SSD_HEADS
    dt_bias = ssd_dt_bias[0].reshape(-1)

    xbc_c = _proj(hc, w_in_t, col_xbc, xbc_cols, "conv_silu", tm=lc, tn=512, conv_w=ssd_conv_w[0],
                  bias=ssd_conv_b[0], group=lc)
    dt_c = _proj(hc, w_in_t, col_dt, n_dt, "softplus", tm=lc, tn=n_dt, bias=dt_bias)
    alog = ssd_a_log[0]
    alogT = alog.T
    zero_state = jnp.zeros((SSD_GROUPS, SSD_STATE, di // SSD_GROUPS), F32)
    h0f = _ssd(xbc_c, dt_c, dt_c.T, alog, alogT, zero_state, reverse=False, mode="state")
    h0b = _ssd(xbc_c, dt_c, dt_c.T, alog, alogT, zero_state, reverse=True, mode="state")

    z = _proj(hx, w_in_t, 0, di, "none", tm=PROJ_TM, tn=512)
    xbc = _proj(hx, w_in_t, col_xbc, xbc_cols, "conv_silu", tm=PROJ_TM, tn=512, conv_w=ssd_conv_w[0],
                bias=ssd_conv_b[0])
    dt = _proj(hx, w_in_t, col_dt, n_dt, "softplus", tm=PROJ_TM, tn=n_dt, bias=dt_bias)
    u3 = _proj(hx, w_in_t, col_hy, 3 * hyw, "conv", tm=PROJ_TM, tn=512, conv_w=hy_conv_w[0],
               bias=hy_conv_b[0], pitched=True).reshape(l // DFT_B * PITCH, 3 * hyw)

    dtT = dt.T
    yf = _ssd(xbc, dt, dtT, alog, alogT, h0f, reverse=False, mode="fwd")
    dx = jnp.repeat(ssd_d[0], SSD_HEADDIM).reshape(1, di)
    y_ssd = _ssd(xbc, dt, dtT, alog, alogT, h0b, reverse=True, mode="bwd", yf=yf, z=z, dx=dx,
                 ng=ssd_norm_g[0].reshape(1, di))

    tables = _dft_tables(l)
    hid = _filter_mlp(l, hy_f_w1[0], hy_f_b1[0], hy_f_freq1[0], hy_f_w2[0], hy_f_b2[0],
                      hy_f_freq2[0])
    max_decay = math.log(1e-2) / 0.3
    min_decay = math.log(1e-2) / 1.5
    deltas = jnp.abs(jnp.linspace(min_decay, max_decay, hyw, dtype=F32)).reshape(1, hyw)
    kr, ki = _filter_spec(hid, hy_f_w3[0], deltas, tables[0], tables[1], l, hyw)
    z2 = _long_conv(u3, 0, u3, 1, kr, ki, 0, hy_bias[0, 0], tables, l, hyw)
    y_hy = _long_conv(z2, 0, u3, 2, kr, ki, 1, hy_bias[0, 1], tables, l, hyw)

    merged = _merge(hx, y_ssd, y_hy.reshape(l // DFT_B, PITCH, hyw), wt_gate, 0,
                    w_ssd_out[0].astype(BF16), w_hy_out[0].astype(BF16), gate_b[0])
    x1 = _oproj(merged, w_o[0].astype(BF16), x2d, gt1)

    h2, ids, wts = _router(x1, norm2_g[0], sh2, sc2, moe_w_rg[0], moe_b_rg[0], moe_w_re[0],
                           moe_b_re[0])
    pos, meta = _moe_positions(ids)
    pos1, pos2 = pos[:, 0], pos[:, 1]
    n_sorted = 2 * l + MOE_EXPERTS * MOE_TILE
    n_tiles = n_sorted // MOE_TILE
    xs = _moe_dispatch(h2, pos1, pos2, n_sorted)
    ys = _moe_experts(xs, meta[:n_tiles, 0], meta[0, 1:2], meta[:n_tiles, 2], moe_w_gate[0],
                      moe_w_up[0], moe_w_down[0])
    out = _moe_combine(x1, wts, gt2, final_g, ys, pos1, pos2)
    return out[None]
```

```python
import functools
import math

import jax
import jax.numpy as jnp
from jax import lax
from jax.experimental import pallas as pl
from jax.experimental.pallas import tpu as pltpu

F32 = jnp.float32
BF16 = jnp.bfloat16

EPS = 1e-6
GRID_W = 64
SSD_HEADS = 32
SSD_HEADDIM = 64
SSD_GROUPS = 4
SSD_STATE = 128
SSD_CHUNK = 128
SSD_D_INNER = SSD_HEADS * SSD_HEADDIM
SSD_GN = SSD_GROUPS * SSD_STATE
HY_BANDS = 16
HY_EMB = 33
HY_HIDDEN = 64
MOE_GROUPS = 4
MOE_PER_GROUP = 4
MOE_EXPERTS = 16

LANES = 128
DFT_B = 128
PITCH = 136
VMEM_LIMIT = 56 * 1024 * 1024
UNROLL_OUTER = 16
UNROLL_MID = 16
ROW_ALIGN = 16
PROJ_TM = 2048
PROJ_SUB_ROWS = 256
MOE_TILE = 256


def _cparams(sem):
    return pltpu.CompilerParams(dimension_semantics=sem, vmem_limit_bytes=VMEM_LIMIT)


def _dot(a, b):
    return jnp.dot(a, b, preferred_element_type=F32)


def _dot_nt(a, bt):
    return lax.dot_general(a, bt, (((1,), (1,)), ((), ())), preferred_element_type=F32)


def _sigmoid(x):
    return 1.0 / (1.0 + jnp.exp(-x))


def _silu(x):
    return x * _sigmoid(x)


def _split3(v):
    b1 = v.astype(BF16)
    r1 = v - b1.astype(F32)
    b2 = r1.astype(BF16)
    b3 = (r1 - b2.astype(F32)).astype(BF16)
    return b1, b2, b3


def _dot_sel_r(v, sel_b):
    b1, b2, b3 = _split3(v)
    return _dot(b1, sel_b) + _dot(b2, sel_b) + _dot(b3, sel_b)


def _dot_sel_l(sel_b, v):
    b1, b2, b3 = _split3(v)
    return _dot(sel_b, b1) + _dot(sel_b, b2) + _dot(sel_b, b3)


def _dot_mid(a, b):
    a1 = a.astype(BF16)
    a2 = (a - a1.astype(F32)).astype(BF16)
    b1 = b.astype(BF16)
    b2 = (b - b1.astype(F32)).astype(BF16)
    return _dot(a1, b1) + (_dot(a1, b2) + _dot(a2, b1))


def _dot_hi(a, b):
    a1, a2, a3 = _split3(a)
    b1, b2, b3 = _split3(b)
    return (_dot(a1, b1) + (_dot(a1, b2) + _dot(a2, b1))
            + (_dot(a1, b3) + _dot(a2, b2) + _dot(a3, b1)))


def _adaln_kernel(c_ref, w_ref, b_ref, o_ref):
    s = _silu(c_ref[...]).astype(BF16)
    o_ref[...] = _dot(s, w_ref[...].astype(BF16)) + b_ref[...]


def _adaln(cc, w, b, tn=1024):
    m, d = cc.shape
    n = w.shape[1]
    return pl.pallas_call(
        _adaln_kernel,
        out_shape=jax.ShapeDtypeStruct((m, n), F32),
        grid=(n // tn,),
        in_specs=[pl.BlockSpec((m, d), lambda j: (0, 0)),
                  pl.BlockSpec((d, tn), lambda j: (0, j)),
                  pl.BlockSpec((1, tn), lambda j: (0, j))],
        out_specs=pl.BlockSpec((m, tn), lambda j: (0, j)),
        compiler_params=_cparams(("parallel",)),
        name="adaln",
    )(cc, w, b.reshape(1, n))


def _normmod_kernel(x_ref, g_ref, sh_ref, sc_ref, o_ref):
    x = x_ref[...]
    y = x * lax.rsqrt(jnp.mean(x * x, axis=-1, keepdims=True) + EPS) * g_ref[...]
    o_ref[...] = (y * (1.0 + sc_ref[...]) + sh_ref[...]).astype(o_ref.dtype)


def _normmod(x, g, sh, sc, tm):
    m, d = x.shape
    row = pl.BlockSpec((1, d), lambda i: (0, 0))
    return pl.pallas_call(
        _normmod_kernel,
        out_shape=jax.ShapeDtypeStruct((m, d), BF16),
        grid=(m // tm,),
        in_specs=[pl.BlockSpec((tm, d), lambda i: (i, 0)), row, row, row],
        out_specs=pl.BlockSpec((tm, d), lambda i: (i, 0)),
        compiler_params=_cparams(("parallel",)),
        name="normmod",
    )(x, g.reshape(1, d), sh, sc)


def _conv_rows(acc, cw_ref, group):
    tm, tn = acc.shape
    taps = cw_ref.shape[0]
    half = taps // 2
    pos = lax.broadcasted_iota(jnp.int32, (tm, tn), 0) & (group - 1)
    out = acc * cw_ref[half:half + 1, :]
    for k in range(taps):
        d = k - half
        if d == 0:
            continue
        shifted = pltpu.roll(acc, (tm - d) % tm, 0)
        valid = (pos + d >= 0) & (pos + d < group)
        out = out + jnp.where(valid, shifted, 0.0) * cw_ref[k:k + 1, :]
    return out


def _proj_kernel(a_ref, w_ref, *rest, epi, group):
    o_ref = rest[-1]
    tm = a_ref.shape[0]
    sub = max(group, min(tm, PROJ_SUB_ROWS))
    w = w_ref[...].astype(BF16)
    n_sub = tm // sub
    nxt = _dot_nt(a_ref[0:sub, :], w)
    for s in range(n_sub):
        acc = nxt
        if s + 1 < n_sub:
            nxt = _dot_nt(a_ref[(s + 1) * sub:(s + 2) * sub, :], w)
        if epi == "none":
            out = acc
        elif epi == "conv":
            cw_ref, b_ref = rest[0], rest[1]
            out = _conv_rows(acc, cw_ref, group) + b_ref[...]
        elif epi == "conv_silu":
            cw_ref, b_ref = rest[0], rest[1]
            out = _silu(_conv_rows(acc, cw_ref, group) + b_ref[...])
        elif epi == "softplus":
            v = acc + rest[0][...]
            out = jnp.maximum(v, 0.0) + jnp.log(1.0 + jnp.exp(-jnp.abs(v)))
        else:
            raise ValueError(epi)
        if len(o_ref.shape) == 3:
            nb, tn = sub // DFT_B, o_ref.shape[2]
            o_ref[s * nb:(s + 1) * nb, 0:DFT_B, :] = out.reshape(nb, DFT_B, tn).astype(o_ref.dtype)
            o_ref[s * nb:(s + 1) * nb, DFT_B:PITCH, :] = jnp.zeros((nb, PITCH - DFT_B, tn), o_ref.dtype)
        else:
            o_ref[s * sub:(s + 1) * sub, :] = out.astype(o_ref.dtype)


def _proj(a, wt, row0, n, epi, tm, tn, conv_w=None, bias=None, group=GRID_W, out_dtype=F32,
          pitched=False):
    m, k = a.shape
    tm = min(tm, m)
    tn = min(tn, n)
    if pitched:
        out_shape = jax.ShapeDtypeStruct((m // DFT_B, PITCH, n), out_dtype)
        out_spec = pl.BlockSpec((tm // DFT_B, PITCH, tn), lambda i, j: (i, 0, j))
    else:
        out_shape = jax.ShapeDtypeStruct((m, n), out_dtype)
        out_spec = pl.BlockSpec((tm, tn), lambda i, j: (i, j))
    extra, extra_specs = [], []
    if conv_w is not None:
        extra.append(conv_w)
        extra_specs.append(pl.BlockSpec((conv_w.shape[0], tn), lambda i, j: (0, j)))
    if bias is not None:
        extra.append(bias.reshape(1, n))
        extra_specs.append(pl.BlockSpec((1, tn), lambda i, j: (0, j)))
    return pl.pallas_call(
        functools.partial(_proj_kernel, epi=epi, group=group),
        out_shape=out_shape,
        grid=(m // tm, n // tn),
        in_specs=[pl.BlockSpec((tm, k), lambda i, j: (i, 0)),
                  pl.BlockSpec((pl.Element(tn), pl.Element(k)),
                               lambda i, j: (pl.multiple_of(row0 + j * tn, ROW_ALIGN), 0))]
        + extra_specs,
        out_specs=out_spec,
        compiler_params=_cparams(("parallel", "parallel")),
        name="proj_" + epi,
    )(a, wt, *extra)


def _ssd_kernel(xs_ref, b_ref, c_ref, dt_ref, dtT_ref, alog_ref, alogT_ref, h0_ref, *rest,
                reverse, mode):
    if mode == "state":
        hfin_ref, st_ref = rest
    elif mode == "fwd":
        y_ref, st_ref = rest
    else:
        yf_ref, z_ref, dx_ref, ng_ref, y_ref, st_ref = rest
    q = xs_ref.shape[0]
    nh = SSD_HEADS
    rp = SSD_D_INNER // SSD_GROUPS
    step = pl.program_id(0)

    @pl.when(step == 0)
    def _():
        st_ref[...] = h0_ref[...]

    d = 1 if reverse else 0
    dtd = dt_ref[:, d * nh:(d + 1) * nh]
    dtdT = dtT_ref[d * nh:(d + 1) * nh, :]
    a_row = -jnp.exp(alog_ref[d:d + 1, :])
    a_col = -jnp.exp(alogT_ref[:, d:d + 1])
    ad = dtd * a_row
    adT = dtdT * a_col
    row = lax.broadcasted_iota(jnp.int32, (q, q), 0)
    col = lax.broadcasted_iota(jnp.int32, (q, q), 1)
    lower = col <= row
    upper = col >= row
    causal = upper if reverse else lower
    tri = jnp.where(causal, 1.0, 0.0).astype(BF16)
    triT = jnp.where(lower if reverse else upper, 1.0, 0.0).astype(BF16)
    acs = _dot_sel_l(tri, ad)
    acsT = _dot_sel_r(adT, triT)
    tot = jnp.sum(ad, axis=0, keepdims=True)
    dte = jnp.exp(tot - acs)
    eacs = jnp.exp(acs)
    cdec = jnp.exp(tot)
    hsel = lax.broadcasted_iota(jnp.int32, (nh, SSD_D_INNER), 0)
    lsel = lax.broadcasted_iota(jnp.int32, (nh, SSD_D_INNER), 1) // SSD_HEADDIM
    expand = jnp.where(hsel == lsel, 1.0, 0.0).astype(BF16)
    stacked = jnp.concatenate([dtd * dte, eacs, jnp.broadcast_to(cdec, (8, nh))], axis=0)
    s1 = stacked.astype(BF16)
    s2 = (stacked - s1.astype(F32)).astype(BF16)
    exp_all = _dot(s1, expand) + _dot(s2, expand)
    w_x = exp_all[0:q]
    eacs_x = exp_all[q:2 * q]
    cdec_x = exp_all[2 * q:2 * q + 1]

    xs = xs_ref[...]
    xs_b = xs.astype(BF16)
    xdw = (xs * w_x).astype(BF16)
    neg_inf = jnp.float32(-jnp.inf)
    y_groups = []
    for g in range(SSD_GROUPS):
        bg = b_ref[:, g * SSD_STATE:(g + 1) * SSD_STATE]
        st_old = st_ref[g]
        s_new = _dot(bg.T.astype(BF16), xdw[:, g * rp:(g + 1) * rp])
        if mode != "state":
            bg_b = bg.astype(BF16)
            cg_b = c_ref[:, g * SSD_STATE:(g + 1) * SSD_STATE].astype(BF16)
            cb = lax.dot_general(cg_b, bg_b, (((1,), (1,)), ((), ())),
                                 preferred_element_type=F32)
            y_off = _dot(cg_b, st_old.astype(BF16)) * eacs_x[:, g * rp:(g + 1) * rp]
            parts = []
            for r in range(SSD_HEADS // SSD_GROUPS):
                h = g * (SSD_HEADS // SSD_GROUPS) + r
                seg = acs[:, h:h + 1] - acsT[h:h + 1, :]
                dec = jnp.exp(jnp.where(causal, seg, neg_inf))
                mat = (cb * dec * dtdT[h:h + 1, :]).astype(BF16)
                parts.append(_dot(mat, xs_b[:, h * SSD_HEADDIM:(h + 1) * SSD_HEADDIM]))
            y_groups.append(jnp.concatenate(parts, axis=1) + y_off)
        st_ref[g] = st_old * cdec_x[:, g * rp:(g + 1) * rp] + s_new

    if mode == "state":
        @pl.when(step == pl.num_programs(0) - 1)
        def _():
            hfin_ref[...] = st_ref[...]
    elif mode == "fwd":
        for g in range(SSD_GROUPS):
            y_ref[:, g * rp:(g + 1) * rp] = y_groups[g]
    else:
        for g in range(SSD_GROUPS):
            sl = slice(g * rp, (g + 1) * rp)
            y = yf_ref[:, sl] + y_groups[g] + dx_ref[:, sl] * xs[:, sl]
            yz = y * _silu(z_ref[:, sl])
            yz = yz * lax.rsqrt(jnp.mean(yz * yz, axis=-1, keepdims=True) + EPS)
            y_ref[:, sl] = (yz * ng_ref[:, sl]).astype(y_ref.dtype)


def _ssd(xbc, dt, dtT, alog, alogT, h0, *, reverse, mode, yf=None, z=None, dx=None, ng=None):
    l = xbc.shape[0]
    q = SSD_CHUNK
    nc = l // q
    di = SSD_D_INNER
    nb = di // SSD_GN
    if reverse:
        cidx = lambda c: nc - 1 - c
    else:
        cidx = lambda c: c
    st_shape = (SSD_GROUPS, SSD_STATE, di // SSD_GROUPS)
    full3 = pl.BlockSpec(st_shape, lambda c: (0, 0, 0))
    in_specs = [pl.BlockSpec((q, di), lambda c: (cidx(c), 0)),
                pl.BlockSpec((q, SSD_GN), lambda c: (cidx(c), nb)),
                pl.BlockSpec((q, SSD_GN), lambda c: (cidx(c), nb + 1)),
                pl.BlockSpec((q, dt.shape[1]), lambda c: (cidx(c), 0)),
                pl.BlockSpec((dtT.shape[0], q), lambda c: (0, cidx(c))),
                pl.BlockSpec((2, SSD_HEADS), lambda c: (0, 0)),
                pl.BlockSpec((SSD_HEADS, 2), lambda c: (0, 0)),
                full3]
    args = [xbc, xbc, xbc, dt, dtT, alog, alogT, h0]
    wide = pl.BlockSpec((q, di), lambda c: (cidx(c), 0))
    rowspec = pl.BlockSpec((1, di), lambda c: (0, 0))
    if mode == "state":
        out_shape = jax.ShapeDtypeStruct(st_shape, F32)
        out_specs = full3
    elif mode == "fwd":
        out_shape = jax.ShapeDtypeStruct((l, di), F32)
        out_specs = wide
    else:
        in_specs += [wide, wide, rowspec, rowspec]
        args += [yf, z, dx, ng]
        out_shape = jax.ShapeDtypeStruct((l, di), BF16)
        out_specs = wide
    return pl.pallas_call(
        functools.partial(_ssd_kernel, reverse=reverse, mode=mode),
        out_shape=out_shape,
        grid=(nc,),
        in_specs=in_specs,
        out_specs=out_specs,
        scratch_shapes=[pltpu.VMEM(st_shape, F32)],
        compiler_params=_cparams(("arbitrary",)),
        name="ssd_%s_%s" % (mode, "rev" if reverse else "fwd"),
    )(*args)


def _dft_tables(l):
    n = 2 * l
    hh = l // DFT_B
    ka = jnp.arange(hh, dtype=jnp.int32)
    n1 = jnp.arange(hh, dtype=jnp.int32)
    n2 = jnp.arange(DFT_B, dtype=jnp.int32)
    odd = 2 * ka + 1
    ang_a = ((odd[:, None] * (DFT_B * n1)[None, :]) % (2 * n)).astype(F32) * (math.pi / n)
    ang_b = ((n2[:, None] * odd[None, :]) % (2 * n)).astype(F32) * (math.pi / n)
    ca, sa = jnp.cos(ang_a)[None, :, :], jnp.sin(ang_a)[None, :, :]
    cb, sb = jnp.cos(ang_b)[:, :, None], jnp.sin(ang_b)[:, :, None]
    cs = ca * cb - sa * sb
    sn = sa * cb + ca * sb
    g1 = jnp.concatenate([cs, -sn], axis=1).astype(BF16)
    scale = 2.0 / n
    h2 = jnp.concatenate([jnp.swapaxes(cs, 1, 2), -jnp.swapaxes(sn, 1, 2)], axis=2) * scale
    kb = jnp.arange(DFT_B, dtype=jnp.int32)
    ph2 = (kb[:, None] * n2[None, :]) % DFT_B
    ang2 = ph2.astype(F32) * (2.0 * math.pi / DFT_B)
    fr, fi = jnp.cos(ang2), -jnp.sin(ang2)
    f2 = jnp.concatenate([jnp.concatenate([fr, -fi], axis=1),
                          jnp.concatenate([fi, fr], axis=1)], axis=0).astype(BF16)
    f2i = jnp.concatenate([jnp.concatenate([fr, fi], axis=1),
                           jnp.concatenate([-fi, fr], axis=1)], axis=0).astype(BF16)
    return g1, f2, f2i, h2.astype(BF16)


def _filter_mlp_kernel(bands_ref, w1_ref, b1_ref, f1_ref, w2_ref, b2_ref, f2_ref, o_ref, *, l):
    tm = o_ref.shape[0]
    base = pl.program_id(0) * tm
    idx = (lax.broadcasted_iota(jnp.int32, (tm, LANES), 0) + base).astype(F32)
    lane = lax.broadcasted_iota(jnp.int32, (tm, LANES), 1)
    t = idx * (1.0 / (l - 1))
    w = idx * (2.0 * math.pi / l)
    arg = bands_ref[...] * w
    feats = jnp.where(lane == 0, t,
                      jnp.where(lane <= HY_BANDS, jnp.cos(arg),
                                jnp.where(lane < HY_EMB, -jnp.sin(arg), 0.0)))
    h = jnp.sin(f1_ref[...] * (_dot_hi(feats, w1_ref[...]) + b1_ref[...]))
    h = jnp.sin(f2_ref[...] * (_dot_hi(h, w2_ref[...]) + b2_ref[...]))
    o_ref[...] = h.astype(o_ref.dtype)


def _filter_mlp(l, w1, b1, f1, w2, b2, f2, tm=1024):
    tm = min(tm, l)
    bands = jnp.linspace(1e-4, HY_BANDS - 1, HY_BANDS, dtype=F32)
    bands_row = jnp.zeros((1, LANES), F32).at[0, 1:1 + HY_BANDS].set(bands)
    bands_row = bands_row.at[0, 1 + HY_BANDS:HY_EMB].set(bands)
    w1p = jnp.zeros((LANES, HY_HIDDEN), F32).at[:HY_EMB].set(w1)
    full = lambda a: pl.BlockSpec(a.shape, lambda i: (0,) * a.ndim)
    args = [bands_row, w1p, b1.reshape(1, -1), f1.reshape(1, -1), w2, b2.reshape(1, -1),
            f2.reshape(1, -1)]
    return pl.pallas_call(
        functools.partial(_filter_mlp_kernel, l=l),
        out_shape=jax.ShapeDtypeStruct((l, HY_HIDDEN), BF16),
        grid=(l // tm,),
        in_specs=[full(a) for a in args],
        out_specs=pl.BlockSpec((tm, HY_HIDDEN), lambda i: (i, 0)),
        compiler_params=_cparams(("parallel",)),
        name="hyena_filter_mlp",
    )(*args)


def _pq_pitch(hh):
    return 2 * hh + 8


def _fwd_stage1(src_ref, g1_ref, pq_ref, hh):
    pqp = _pq_pitch(hh)

    def body(n2, carry):
        rows = src_ref[pl.ds(n2, hh, stride=PITCH), :]
        p = _dot(g1_ref[n2], rows.astype(BF16))
        pq_ref[pl.ds(pl.multiple_of(n2 * pqp, 8), 2 * hh), :] = p
        return carry
    lax.fori_loop(0, DFT_B, body, 0, unroll=UNROLL_OUTER)


def _load_spectrum_rows(pq_ref, ka, hh):
    pqp = _pq_pitch(hh)
    return jnp.concatenate([pq_ref[pl.ds(ka, DFT_B, stride=pqp), :],
                            pq_ref[pl.ds(hh + ka, DFT_B, stride=pqp), :]], axis=0)


def _filter_spec_kernel(hid_ref, wf_ref, wb_ref, dl_ref, g1_ref, f2_ref, kr_ref, ki_ref,
                        s_ref, d_ref, pq_ref, pq2_ref, *, l):
    hh = l // DFT_B
    t = lax.broadcasted_iota(jnp.int32, (l, LANES), 0).astype(F32) * (1.0 / (l - 1))
    dec = jnp.exp(-t * dl_ref[...])
    hid = hid_ref[...]
    hf = _dot(hid, wf_ref[...].astype(BF16)) * dec
    hb = _dot(hid, wb_ref[...].astype(BF16)) * dec
    first = lax.broadcasted_iota(jnp.int32, (l, LANES), 0) == 0
    hb = jnp.where(first, 0.0, hb)
    inv = 1.0 / (jnp.sum(jnp.abs(hf), axis=0, keepdims=True)
                 + jnp.sum(jnp.abs(hb), axis=0, keepdims=True))
    hs = hf + hb
    hd = hf - hb
    for n1 in range(hh):
        s_ref[n1 * PITCH:n1 * PITCH + DFT_B, :] = hs[n1 * DFT_B:(n1 + 1) * DFT_B]
        d_ref[n1 * PITCH:n1 * PITCH + DFT_B, :] = hd[n1 * DFT_B:(n1 + 1) * DFT_B]
    f2_re = f2_ref[0:DFT_B, :]
    pqp = _pq_pitch(hh)

    def stage1(n2, carry):
        g = g1_ref[n2]
        ps = _dot(g, s_ref[pl.ds(n2, hh, stride=PITCH), :].astype(BF16))
        pd = _dot(g, d_ref[pl.ds(n2, hh, stride=PITCH), :].astype(BF16))
        off = pl.multiple_of(n2 * pqp, 8)
        pq_ref[pl.ds(off, 2 * hh), :] = ps
        pq2_ref[pl.ds(off, 2 * hh), :] = jnp.concatenate([pd[hh:], -pd[:hh]], axis=0)
        return carry
    lax.fori_loop(0, DFT_B, stage1, 0, unroll=UNROLL_OUTER)

    def body(ka, carry):
        pp = jnp.concatenate([_load_spectrum_rows(pq_ref, ka, hh).astype(BF16),
                              _load_spectrum_rows(pq2_ref, ka, hh).astype(BF16)], axis=1)
        x = _dot(f2_re, pp)
        rows = pl.ds(pl.multiple_of(ka * DFT_B, DFT_B), DFT_B)
        kr_ref[rows, :] = (x[:, :LANES] * inv).astype(kr_ref.dtype)
        ki_ref[rows, :] = (x[:, LANES:] * inv).astype(ki_ref.dtype)
        return carry
    lax.fori_loop(0, hh, body, 0, unroll=UNROLL_MID)


def _filter_spec(hid, w3, deltas, g1, f2, l, c):
    hh = l // DFT_B
    nct = c // LANES
    orders = w3.shape[1] // (2 * c)
    spec = pl.BlockSpec((None, l, LANES), lambda o, j: (o, 0, j))
    out_sd = jax.ShapeDtypeStruct((orders, l, c), BF16)
    return pl.pallas_call(
        functools.partial(_filter_spec_kernel, l=l),
        out_shape=(out_sd, out_sd),
        grid=(orders, nct),
        in_specs=[pl.BlockSpec((l, HY_HIDDEN), lambda o, j: (0, 0)),
                  pl.BlockSpec((HY_HIDDEN, LANES), lambda o, j: (0, (2 * o) * nct + j)),
                  pl.BlockSpec((HY_HIDDEN, LANES), lambda o, j: (0, (2 * o + 1) * nct + j)),
                  pl.BlockSpec((1, LANES), lambda o, j: (0, j)),
                  pl.BlockSpec(g1.shape, lambda o, j: (0, 0, 0)),
                  pl.BlockSpec(f2.shape, lambda o, j: (0, 0))],
        out_specs=(spec, spec),
        scratch_shapes=[pltpu.VMEM((hh * PITCH, LANES), F32)] * 2
        + [pltpu.VMEM((DFT_B * _pq_pitch(hh), LANES), F32)] * 2,
        compiler_params=_cparams(("parallel", "parallel")),
        name="hyena_filter_spec",
    )(hid, w3, w3, deltas, g1, f2)


def _long_conv_kernel(u_ref, m_ref, kr_ref, ki_ref, bias_ref, g1_ref, f2_ref, f2i_ref, h2_ref,
                      o_ref, pq_ref, *, l):
    hh = l // DFT_B
    pqp = _pq_pitch(hh)
    _fwd_stage1(u_ref, g1_ref, pq_ref, hh)
    f2 = f2_ref[...]
    f2i = f2i_ref[...]

    def mid(ka, carry):
        koff = pl.multiple_of(ka * DFT_B, DFT_B)
        x = _dot(f2, _load_spectrum_rows(pq_ref, ka, hh).astype(BF16))
        xr, xi = x[:DFT_B], x[DFT_B:]
        kr = kr_ref[pl.ds(koff, DFT_B), :].astype(F32)
        ki = ki_ref[pl.ds(koff, DFT_B), :].astype(F32)
        yy = jnp.concatenate([xr * kr - xi * ki, xr * ki + xi * kr], axis=0).astype(BF16)
        qq = _dot(f2i, yy)
        pq_ref[pl.ds(ka, DFT_B, stride=pqp), :] = qq[:DFT_B]
        pq_ref[pl.ds(hh + ka, DFT_B, stride=pqp), :] = qq[DFT_B:]
        return carry
    lax.fori_loop(0, hh, mid, 0, unroll=UNROLL_MID)

    bias = bias_ref[...]

    def last(n2, carry):
        qq = pq_ref[pl.ds(pl.multiple_of(n2 * pqp, 8), 2 * hh), :].astype(BF16)
        y = _dot(h2_ref[n2], qq)
        u = u_ref[pl.ds(n2, hh, stride=PITCH), :]
        m = m_ref[pl.ds(n2, hh, stride=PITCH), :]
        o_ref[pl.ds(n2, hh, stride=PITCH), :] = (m * (y + bias * u)).astype(o_ref.dtype)
        return carry
    lax.fori_loop(0, DFT_B, last, 0, unroll=UNROLL_OUTER)
    for n1 in range(hh):
        o_ref[n1 * PITCH + DFT_B:(n1 + 1) * PITCH, :] = jnp.zeros((PITCH - DFT_B, LANES), o_ref.dtype)


def _long_conv(u_arr, u_blk, m_arr, m_blk, kr, ki, order, bias, tables, l, c):
    g1, f2, f2i, h2 = tables
    hh = l // DFT_B
    nct = c // LANES
    kspec = pl.BlockSpec((None, l, LANES), lambda j: (order, 0, j))
    return pl.pallas_call(
        functools.partial(_long_conv_kernel, l=l),
        out_shape=jax.ShapeDtypeStruct((hh * PITCH, c), F32),
        grid=(nct,),
        in_specs=[pl.BlockSpec((hh * PITCH, LANES), lambda j: (0, u_blk * nct + j)),
                  pl.BlockSpec((hh * PITCH, LANES), lambda j: (0, m_blk * nct + j)),
                  kspec, kspec,
                  pl.BlockSpec((1, LANES), lambda j: (0, j)),
                  pl.BlockSpec(g1.shape, lambda j: (0, 0, 0)),
                  pl.BlockSpec(f2.shape, lambda j: (0, 0)),
                  pl.BlockSpec(f2i.shape, lambda j: (0, 0)),
                  pl.BlockSpec(h2.shape, lambda j: (0, 0, 0))],
        out_specs=pl.BlockSpec((hh * PITCH, LANES), lambda j: (0, j)),
        scratch_shapes=[pltpu.VMEM((DFT_B * _pq_pitch(hh), LANES), F32)],
        compiler_params=_cparams(("parallel",)),
        name="hyena_long_conv",
    )(u_arr, m_arr, kr, ki, bias.reshape(1, c), g1, f2, f2i, h2)


def _merge_kernel(hx_ref, ys_ref, yh_ref, wg1_ref, wg2_ref, w1_ref, w2_ref, gb1_ref, gb2_ref, o_ref):
    hx = hx_ref[...]
    g1 = _sigmoid(_dot_nt(hx, wg1_ref[...]) + gb1_ref[...])
    g2 = _sigmoid(_dot_nt(hx, wg2_ref[...]) + gb2_ref[...])
    yh = yh_ref[:, 0:DFT_B, :].reshape(hx.shape).astype(BF16)
    out = g1 * _dot(ys_ref[...], w1_ref[...]) + g2 * _dot(yh, w2_ref[...])
    o_ref[...] = out.astype(o_ref.dtype)


def _merge(hx, ys, yh, wgt, row0, w1, w2, gate_b, tm=512, tn=512):
    m, d = hx.shape
    nt = d // tn
    a_spec = pl.BlockSpec((tm, d), lambda i, j: (i, 0))
    w_spec = pl.BlockSpec((d, tn), lambda i, j: (0, j))
    gb = gate_b.reshape(1, 2 * d)
    return pl.pallas_call(
        _merge_kernel,
        out_shape=jax.ShapeDtypeStruct((m, d), BF16),
        grid=(m // tm, nt),
        in_specs=[a_spec, a_spec,
                  pl.BlockSpec((tm // DFT_B, PITCH, d), lambda i, j: (i, 0, 0)),
                  pl.BlockSpec((pl.Element(tn), pl.Element(d)),
                               lambda i, j: (pl.multiple_of(row0 + j * tn, ROW_ALIGN), 0)),
                  pl.BlockSpec((pl.Element(tn), pl.Element(d)),
                               lambda i, j: (pl.multiple_of(row0 + (nt + j) * tn, ROW_ALIGN), 0)),
                  w_spec, w_spec,
                  pl.BlockSpec((1, tn), lambda i, j: (0, j)),
                  pl.BlockSpec((1, tn), lambda i, j: (0, nt + j))],
        out_specs=pl.BlockSpec((tm, tn), lambda i, j: (i, j)),
        compiler_params=_cparams(("parallel", "parallel")),
        name="merge",
    )(hx, ys, yh, wgt, wgt, w1, w2, gb, gb)


def _oproj_kernel(a_ref, w_ref, x_ref, gt_ref, o_ref):
    o_ref[...] = x_ref[...] + gt_ref[...] * _dot(a_ref[...], w_ref[...])


def _oproj(a, w, x, gt, tm=1024, tn=512):
    m, d = a.shape
    n = w.shape[1]
    return pl.pallas_call(
        _oproj_kernel,
        out_shape=jax.ShapeDtypeStruct((m, n), F32),
        grid=(m // tm, n // tn),
        in_specs=[pl.BlockSpec((tm, d), lambda i, j: (i, 0)),
                  pl.BlockSpec((d, tn), lambda i, j: (0, j)),
                  pl.BlockSpec((tm, tn), lambda i, j: (i, j)),
                  pl.BlockSpec((1, tn), lambda i, j: (0, j))],
        out_specs=pl.BlockSpec((tm, tn), lambda i, j: (i, j)),
        compiler_params=_cparams(("parallel", "parallel")),
        name="oproj",
    )(a, w, x, gt)


def _pack_bf16_pairs(v):
    half = v.shape[1] // 2
    bits = lax.bitcast_convert_type(v.astype(BF16).astype(F32), jnp.uint32)
    return (bits[:, :half] >> 16) | (bits[:, half:] & jnp.uint32(0xFFFF0000))


def _unpack_bf16_pairs(p):
    lo = lax.bitcast_convert_type(p << 16, F32).astype(BF16)
    hi = lax.bitcast_convert_type(p & jnp.uint32(0xFFFF0000), F32).astype(BF16)
    return lo, hi


def _router_kernel(x_ref, g_ref, sh_ref, sc_ref, wr_ref, br_ref, h_ref, ids_ref, wts_ref):
    x = x_ref[...]
    y = x * lax.rsqrt(jnp.mean(x * x, axis=-1, keepdims=True) + EPS) * g_ref[...]
    h = y * (1.0 + sc_ref[...]) + sh_ref[...]
    h_ref[...] = _pack_bf16_pairs(h)
    logits = _dot_mid(h, wr_ref[...]) + br_ref[...]
    tm = x.shape[0]
    lane = lax.broadcasted_iota(jnp.int32, (tm, LANES), 1)
    neg = jnp.float32(-jnp.inf)
    big = jnp.int32(LANES)
    is_grp = (lane >= MOE_EXPERTS) & (lane < MOE_EXPERTS + MOE_GROUPS)
    gl = jnp.where(is_grp, logits, neg)
    gmax = jnp.max(gl, axis=-1, keepdims=True)
    gidx = jnp.min(jnp.where(gl == gmax, lane, big), axis=-1, keepdims=True) - MOE_EXPERTS
    gw = 1.0 / jnp.sum(jnp.where(is_grp, jnp.exp(logits - gmax), 0.0), axis=-1, keepdims=True)
    in_grp = (lane < MOE_EXPERTS) & ((lane // MOE_PER_GROUP) == gidx)
    el = jnp.where(in_grp, logits, neg)
    m1 = jnp.max(el, axis=-1, keepdims=True)
    i1 = jnp.min(jnp.where(el == m1, lane, big), axis=-1, keepdims=True)
    el2 = jnp.where(lane == i1, neg, el)
    m2 = jnp.max(el2, axis=-1, keepdims=True)
    i2 = jnp.min(jnp.where(el2 == m2, lane, big), axis=-1, keepdims=True)
    e21 = jnp.exp(m2 - m1)
    w1 = gw / (1.0 + e21)
    w2 = gw * e21 / (1.0 + e21)
    ids_ref[...] = jnp.where(lane == 0, i1, jnp.where(lane == 1, i2, -1))
    wts_ref[...] = jnp.where(lane == 0, w1, jnp.where(lane == 1, w2, 0.0))


def _router(x1, g, sh, sc, w_rg, b_rg, w_re, b_re, tm=512):
    m, d = x1.shape
    wr = jnp.zeros((d, LANES), F32).at[:, :MOE_EXPERTS].set(w_re)
    wr = wr.at[:, MOE_EXPERTS:MOE_EXPERTS + MOE_GROUPS].set(w_rg)
    br = jnp.zeros((1, LANES), F32).at[0, :MOE_EXPERTS].set(b_re)
    br = br.at[0, MOE_EXPERTS:MOE_EXPERTS + MOE_GROUPS].set(b_rg)
    row = pl.BlockSpec((1, d), lambda i: (0, 0))
    return pl.pallas_call(
        _router_kernel,
        out_shape=(jax.ShapeDtypeStruct((m, d // 2), jnp.uint32),
                   jax.ShapeDtypeStruct((m, LANES), jnp.int32),
                   jax.ShapeDtypeStruct((m, LANES), F32)),
        grid=(m // tm,),
        in_specs=[pl.BlockSpec((tm, d), lambda i: (i, 0)), row, row, row,
                  pl.BlockSpec((d, LANES), lambda i: (0, 0)),
                  pl.BlockSpec((1, LANES), lambda i: (0, 0))],
        out_specs=(pl.BlockSpec((tm, d // 2), lambda i: (i, 0)),
                   pl.BlockSpec((tm, LANES), lambda i: (i, 0)),
                   pl.BlockSpec((tm, LANES), lambda i: (i, 0))),
        compiler_params=_cparams(("parallel",)),
        name="router",
    )(x1, g.reshape(1, d), sh, sc, wr, br)


def _moe_positions_kernel(ids_ref, pos_ref, meta_ref):
    n = ids_ref.shape[0]
    t = MOE_TILE
    lane = lax.broadcasted_iota(jnp.int32, (t, LANES), 1)

    def onehot(k):
        idt = ids_ref[pl.ds(pl.multiple_of(k * t, t), t), :]
        i1, i2 = idt[:, 0:1], idt[:, 1:2]
        return i1, i2, jnp.where((lane == i1) | (lane == i2), 1.0, 0.0)

    def count(k, acc):
        return acc + jnp.sum(onehot(k)[2], axis=0, keepdims=True)
    total = lax.fori_loop(0, n // t, count, jnp.zeros((1, LANES), F32))
    padded = (((total.astype(jnp.int32) + (t - 1)) // t) * t).astype(F32)
    r128 = lax.broadcasted_iota(jnp.int32, (LANES, LANES), 0)
    c128 = lax.broadcasted_iota(jnp.int32, (LANES, LANES), 1)
    before = jnp.where(r128 < c128, 1.0, 0.0).astype(BF16)
    off = _dot_sel_r(jnp.broadcast_to(padded, (8, LANES)), before)[0:1]
    row = lax.broadcasted_iota(jnp.int32, (t, t), 0)
    col = lax.broadcasted_iota(jnp.int32, (t, t), 1)
    tri = jnp.where(col <= row, 1.0, 0.0).astype(BF16)

    def place(k, seen):
        i1, i2, oh = onehot(k)
        base = off + seen + _dot(tri, oh.astype(BF16)) - oh
        p1 = jnp.sum(jnp.where(lane == i1, base, 0.0), axis=1, keepdims=True)
        p2 = jnp.sum(jnp.where(lane == i2, base, 0.0), axis=1, keepdims=True)
        pos_ref[pl.ds(pl.multiple_of(k * t, t), t), :] = jnp.where(
            lane == 0, p1, jnp.where(lane == 1, p2, 0.0)).astype(jnp.int32)
        return seen + jnp.sum(oh, axis=0, keepdims=True)
    lax.fori_loop(0, n // t, place, jnp.zeros((1, LANES), F32))

    ends = off + padded
    start = (r128 * t).astype(F32)
    done = jnp.where((jnp.broadcast_to(ends, (LANES, LANES)) <= start) & (c128 < MOE_EXPERTS), 1.0, 0.0)
    tile_expert = jnp.minimum(jnp.sum(done, axis=1, keepdims=True), MOE_EXPERTS - 1.0)
    used = jnp.sum(jnp.where(c128[0:1] == MOE_EXPERTS - 1, ends, 0.0), axis=1, keepdims=True) / t
    lanef = c128.astype(F32)
    later = (lanef > tile_expert) & (jnp.broadcast_to(padded, (LANES, LANES)) > 0.0) & (c128 < MOE_EXPERTS)
    nxt = jnp.min(jnp.where(later, lanef, float(LANES)), axis=1, keepdims=True)
    nxt = jnp.where(nxt >= float(LANES), -1.0, nxt)
    meta_ref[...] = jnp.where(c128 == 0, tile_expert,
                              jnp.where(c128 == 1, used, jnp.where(c128 == 2, nxt, 0.0))).astype(jnp.int32)


def _moe_positions(ids):
    m = ids.shape[0]
    return pl.pallas_call(
        _moe_positions_kernel,
        out_shape=(jax.ShapeDtypeStruct((m, LANES), jnp.int32),
                   jax.ShapeDtypeStruct((LANES, LANES), jnp.int32)),
        grid=(1,),
        in_specs=[pl.BlockSpec((m, LANES), lambda i: (0, 0))],
        out_specs=(pl.BlockSpec((m, LANES), lambda i: (0, 0)),
                   pl.BlockSpec((LANES, LANES), lambda i: (0, 0))),
        compiler_params=_cparams(("arbitrary",)),
        name="moe_positions",
    )(ids)


def _moe_dispatch_kernel(p1_ref, p2_ref, h_ref, xs_in_ref, xs_ref, sem):
    del xs_in_ref
    t = h_ref.shape[0]
    base = pl.program_id(0) * t

    def issue(r, carry):
        src = h_ref.at[pl.ds(r, 1), :]
        pltpu.make_async_copy(src, xs_ref.at[pl.ds(p1_ref[base + r], 1), :], sem.at[0]).start()
        pltpu.make_async_copy(src, xs_ref.at[pl.ds(p2_ref[base + r], 1), :], sem.at[1]).start()
        return carry
    lax.fori_loop(0, t, issue, 0, unroll=8)
    pltpu.make_async_copy(h_ref, xs_ref.at[pl.ds(0, t), :], sem.at[0]).wait()
    pltpu.make_async_copy(h_ref, xs_ref.at[pl.ds(0, t), :], sem.at[1]).wait()


def _moe_dispatch(hp, pos1, pos2, n_sorted):
    m, w = hp.shape
    t = MOE_TILE
    zeros = jnp.zeros((n_sorted, w), hp.dtype)
    return pl.pallas_call(
        _moe_dispatch_kernel,
        out_shape=jax.ShapeDtypeStruct((n_sorted, w), hp.dtype),
        grid_spec=pltpu.PrefetchScalarGridSpec(
            num_scalar_prefetch=2,
            grid=(m // t,),
            in_specs=[pl.BlockSpec((t, w), lambda i, p1, p2: (i, 0)),
                      pl.BlockSpec(memory_space=pl.ANY)],
            out_specs=pl.BlockSpec(memory_space=pl.ANY),
            scratch_shapes=[pltpu.SemaphoreType.DMA((2,))]),
        input_output_aliases={3: 0},
        compiler_params=_cparams(("arbitrary",)),
        name="moe_dispatch",
    )(pos1, pos2, hp, zeros)


def _moe_experts_kernel(te_ref, nu_ref, nx_ref, x_ref, wg_hbm, wu_hbm, wd_hbm, y_ref,
                        wg_f, wu_f, wd_f, wg_b, wu_b, wd_b, slot_ref, sem):
    i = pl.program_id(0)
    live = i < nu_ref[0]
    new_expert = (i == 0) | (te_ref[i] != te_ref[jnp.maximum(i - 1, 0)])

    def copies(e, s):
        return (pltpu.make_async_copy(wg_hbm.at[e], wg_f.at[s], sem.at[s, 0]),
                pltpu.make_async_copy(wu_hbm.at[e], wu_f.at[s], sem.at[s, 1]),
                pltpu.make_async_copy(wd_hbm.at[e], wd_f.at[s], sem.at[s, 2]))

    @pl.when(i == 0)
    def _():
        slot_ref[0] = 0
        for cp in copies(te_ref[0], 0):
            cp.start()

    @pl.when(live & new_expert)
    def _():
        s = slot_ref[0]
        for cp in copies(te_ref[i], s):
            cp.wait()
        wg_b[...] = wg_f[s].astype(BF16)
        wu_b[...] = wu_f[s].astype(BF16)
        wd_b[...] = wd_f[s].astype(BF16)
        slot_ref[0] = 1 - s

        @pl.when(nx_ref[i] >= 0)
        def _():
            for cp in copies(nx_ref[i], 1 - s):
                cp.start()

    @pl.when(live)
    def _():
        x_lo, x_hi = _unpack_bf16_pairs(x_ref[...])
        half = x_lo.shape[1]
        hg = _dot(x_lo, wg_b[0:half, :]) + _dot(x_hi, wg_b[half:, :])
        hu = _dot(x_lo, wu_b[0:half, :]) + _dot(x_hi, wu_b[half:, :])
        y_ref[...] = _dot((_silu(hg) * hu).astype(BF16), wd_b[...])

    @pl.when(jnp.logical_not(live))
    def _():
        y_ref[...] = jnp.zeros_like(y_ref)


def _moe_experts(xs, tile_expert, n_used, next_expert, wg, wu, wd):
    ns = xs.shape[0]
    ne, d, f = wg.shape
    t = MOE_TILE
    hbm = pl.BlockSpec(memory_space=pl.ANY)
    return pl.pallas_call(
        _moe_experts_kernel,
        out_shape=jax.ShapeDtypeStruct((ns, d), F32),
        grid_spec=pltpu.PrefetchScalarGridSpec(
            num_scalar_prefetch=3,
            grid=(ns // t,),
            in_specs=[pl.BlockSpec((t, d // 2), lambda i, te, nu, nx: (i, 0)), hbm, hbm, hbm],
            out_specs=pl.BlockSpec((t, d), lambda i, te, nu, nx: (i, 0)),
            scratch_shapes=[pltpu.VMEM((2, d, f), F32), pltpu.VMEM((2, d, f), F32),
                            pltpu.VMEM((2, f, d), F32),
                            pltpu.VMEM((d, f), BF16), pltpu.VMEM((d, f), BF16),
                            pltpu.VMEM((f, d), BF16),
                            pltpu.SMEM((1,), jnp.int32), pltpu.SemaphoreType.DMA((2, 3))]),
        compiler_params=_cparams(("arbitrary",)),
        name="moe_experts",
    )(tile_expert, n_used, next_expert, xs, wg, wu, wd)


def _moe_combine_kernel(p1_ref, p2_ref, x_ref, wts_ref, gt_ref, fg_ref, y_ref, o_ref, ya, yb, sem):
    t = x_ref.shape[0]
    i = pl.program_id(0)
    n = pl.num_programs(0)

    def row_copy(src_row, buf, slot, r, which):
        return pltpu.make_async_copy(y_ref.at[pl.ds(src_row, 1), :], buf.at[slot, pl.ds(r, 1), :],
                                     sem.at[which, slot])

    def issue(tile, slot):
        def body(r, carry):
            row_copy(p1_ref[tile * t + r], ya, slot, r, 0).start()
            row_copy(p2_ref[tile * t + r], yb, slot, r, 1).start()
            return carry
        lax.fori_loop(0, t, body, 0, unroll=8)

    @pl.when(i == 0)
    def _():
        issue(0, 0)

    @pl.when(i + 1 < n)
    def _():
        issue(i + 1, (i + 1) % 2)

    slot = i % 2
    pltpu.make_async_copy(y_ref.at[pl.ds(0, t), :], ya.at[slot], sem.at[0, slot]).wait()
    pltpu.make_async_copy(y_ref.at[pl.ds(0, t), :], yb.at[slot], sem.at[1, slot]).wait()
    w = wts_ref[...]
    moe = w[:, 0:1] * ya[slot] + w[:, 1:2] * yb[slot]
    x2 = x_ref[...] + gt_ref[...] * moe
    o_ref[...] = x2 * lax.rsqrt(jnp.mean(x2 * x2, axis=-1, keepdims=True) + EPS) * fg_ref[...]


def _moe_combine(x1, wts, gt, fg, y_sorted, pos1, pos2):
    m, d = x1.shape
    t = MOE_TILE
    row = pl.BlockSpec((1, d), lambda i, p1, p2: (0, 0))
    return pl.pallas_call(
        _moe_combine_kernel,
        out_shape=jax.ShapeDtypeStruct((m, d), F32),
        grid_spec=pltpu.PrefetchScalarGridSpec(
            num_scalar_prefetch=2,
            grid=(m // t,),
            in_specs=[pl.BlockSpec((t, d), lambda i, p1, p2: (i, 0)),
                      pl.BlockSpec((t, LANES), lambda i, p1, p2: (i, 0)),
                      row, row,
                      pl.BlockSpec(memory_space=pl.ANY)],
            out_specs=pl.BlockSpec((t, d), lambda i, p1, p2: (i, 0)),
            scratch_shapes=[pltpu.VMEM((2, t, d), F32), pltpu.VMEM((2, t, d), F32),
                            pltpu.SemaphoreType.DMA((2, 2))]),
        compiler_params=_cparams(("arbitrary",)),
        name="moe_combine",
    )(pos1, pos2, x1, wts, gt, fg.reshape(1, d), y_sorted)


def kernel(x, c, ctx, c_ctx, ada_w, ada_b, norm1_g, w_in, ssd_conv_w, ssd_conv_b, ssd_dt_bias,
           ssd_a_log, ssd_d, ssd_norm_g, w_ssd_out, hy_conv_w, hy_conv_b, hy_f_w1, hy_f_b1,
           hy_f_freq1, hy_f_w2, hy_f_b2, hy_f_freq2, hy_f_w3, hy_bias, w_hy_out, gate_b, w_o,
           norm2_g, moe_w_rg, moe_b_rg, moe_w_re, moe_b_re, moe_w_gate, moe_w_up, moe_w_down,
           final_g):
    bsz, l, d = x.shape
    assert bsz == 1 and ada_w.shape[0] == 1
    lc = ctx.shape[1]
    di = SSD_D_INNER
    xbc_cols = di + 2 * SSD_GN
    col_xbc = di
    col_dt = col_xbc + xbc_cols
    col_hy = col_dt + 2 * SSD_HEADS
    hyw = hy_bias.shape[-1]

    x2d = x[0]
    ctx2d = ctx[0]

    cc = jnp.zeros((8, d), F32).at[0].set(c[0]).at[1].set(c_ctx)
    mods = _adaln(cc, ada_w[0], ada_b[0])
    sh1, sc1, gt1, sh2, sc2, gt2 = [mods[0:1, i * d:(i + 1) * d] for i in range(6)]
    csh1, csc1 = mods[1:2, 0:d], mods[1:2, d:2 * d]

    hx = _normmod(x2d, norm1_g[0], sh1, sc1, tm=512)
    hc = _normmod(ctx2d, norm1_g[0], csh1, csc1, tm=lc)

    w_in_t = w_in[0].T
    col_gate = col_hy + 3 * hyw
    wt_gate = w_in_t[col_gate:].astype(BF16)
    n_dt = 2 * SSD_HEADS
    dt_bias = ssd_dt_bias[0].reshape(-1)

    xbc_c = _proj(hc, w_in_t, col_xbc, xbc_cols, "conv_silu", tm=lc, tn=512, conv_w=ssd_conv_w[0],
                  bias=ssd_conv_b[0], group=lc)
    dt_c = _proj(hc, w_in_t, col_dt, n_dt, "softplus", tm=lc, tn=n_dt, bias=dt_bias)
    alog = ssd_a_log[0]
    alogT = alog.T
    zero_state = jnp.zeros((SSD_GROUPS, SSD_STATE, di // SSD_GROUPS), F32)
    h0f = _ssd(xbc_c, dt_c, dt_c.T, alog, alogT, zero_state, reverse=False, mode="state")
    h0b = _ssd(xbc_c, dt_c, dt_c.T, alog, alogT, zero_state, reverse=True, mode="state")

    z = _proj(hx, w_in_t, 0, di, "none", tm=PROJ_TM, tn=512)
    xbc = _proj(hx, w_in_t, col_xbc, xbc_cols, "conv_silu", tm=PROJ_TM, tn=512, conv_w=ssd_conv_w[0],
                bias=ssd_conv_b[0])
    dt = _proj(hx, w_in_t, col_dt, n_dt, "softplus", tm=PROJ_TM, tn=n_dt, bias=dt_bias)
    u3 = _proj(hx, w_in_t, col_hy, 3 * hyw, "conv", tm=PROJ_TM, tn=512, conv_w=hy_conv_w[0],
               bias=hy_conv_b[0], pitched=True).reshape(l // DFT_B * PITCH, 3 * hyw)

    dtT = dt.T
    yf = _ssd(xbc, dt, dtT, alog, alogT, h0f, reverse=False, mode="fwd")
    dx = jnp.repeat(ssd_d[0], SSD_HEADDIM).reshape(1, di)
    y_ssd = _ssd(xbc, dt, dtT, alog, alogT, h0b, reverse=True, mode="bwd", yf=yf, z=z, dx=dx,
                 ng=ssd_norm_g[0].reshape(1, di))

    tables = _dft_tables(l)
    hid = _filter_mlp(l, hy_f_w1[0], hy_f_b1[0], hy_f_freq1[0], hy_f_w2[0], hy_f_b2[0],
                      hy_f_freq2[0])
    max_decay = math.log(1e-2) / 0.3
    min_decay = math.log(1e-2) / 1.5
    deltas = jnp.abs(jnp.linspace(min_decay, max_decay, hyw, dtype=F32)).reshape(1, hyw)
    kr, ki = _filter_spec(hid, hy_f_w3[0], deltas, tables[0], tables[1], l, hyw)
    z2 = _long_conv(u3, 0, u3, 1, kr, ki, 0, hy_bias[0, 0], tables, l, hyw)
    y_hy = _long_conv(z2, 0, u3, 2, kr, ki, 1, hy_bias[0, 1], tables, l, hyw)

    merged = _merge(hx, y_ssd, y_hy.reshape(l // DFT_B, PITCH, hyw), wt_gate, 0,
                    w_ssd_out[0].astype(BF16), w_hy_out[0].astype(BF16), gate_b[0])
    x1 = _oproj(merged, w_o[0].astype(BF16), x2d, gt1)

    h2, ids, wts = _router(x1, norm2_g[0], sh2, sc2, moe_w_rg[0], moe_b_rg[0], moe_w_re[0],
                           moe_b_re[0])
    pos, meta = _moe_positions(ids)
    pos1, pos2 = pos[:, 0], pos[:, 1]
    n_sorted = 2 * l + MOE_EXPERTS * MOE_TILE
    n_tiles = n_sorted // MOE_TILE
    xs = _moe_dispatch(h2, pos1, pos2, n_sorted)
    ys = _moe_experts(xs, meta[:n_tiles, 0], meta[0, 1:2], meta[:n_tiles, 2], moe_w_gate[0],
                      moe_w_up[0], moe_w_down[0])
    out = _moe_combine(x1, wts, gt2, final_g, ys, pos1, pos2)
    return out[None]
```

```python
import functools
import math

import jax
import jax.numpy as jnp
from jax import lax
from jax.experimental import pallas as pl
from jax.experimental.pallas import tpu as pltpu

F32 = jnp.float32
BF16 = jnp.bfloat16

EPS = 1e-6
GRID_W = 64
SSD_HEADS = 32
SSD_HEADDIM = 64
SSD_GROUPS = 4
SSD_STATE = 128
SSD_CHUNK = 128
SSD_D_INNER = SSD_HEADS * SSD_HEADDIM
SSD_GN = SSD_GROUPS * SSD_STATE
HY_BANDS = 16
HY_EMB = 33
HY_HIDDEN = 64
MOE_GROUPS = 4
MOE_PER_GROUP = 4
MOE_EXPERTS = 16

LANES = 128
DFT_B = 128
PITCH = 136
VMEM_LIMIT = 56 * 1024 * 1024
UNROLL_OUTER = 16
UNROLL_MID = 16
ROW_ALIGN = 16
PROJ_TM = 2048
PROJ_SUB_ROWS = 512
MOE_TILE = 256


def _cparams(sem):
    return pltpu.CompilerParams(dimension_semantics=sem, vmem_limit_bytes=VMEM_LIMIT)


def _dot(a, b):
    return jnp.dot(a, b, preferred_element_type=F32)


def _dot_nt(a, bt):
    return lax.dot_general(a, bt, (((1,), (1,)), ((), ())), preferred_element_type=F32)


def _sigmoid(x):
    return 1.0 / (1.0 + jnp.exp(-x))


def _silu(x):
    return x * _sigmoid(x)


def _split3(v):
    b1 = v.astype(BF16)
    r1 = v - b1.astype(F32)
    b2 = r1.astype(BF16)
    b3 = (r1 - b2.astype(F32)).astype(BF16)
    return b1, b2, b3


def _dot_sel_r(v, sel_b):
    b1, b2, b3 = _split3(v)
    return _dot(b1, sel_b) + _dot(b2, sel_b) + _dot(b3, sel_b)


def _dot_sel_l(sel_b, v):
    b1, b2, b3 = _split3(v)
    return _dot(sel_b, b1) + _dot(sel_b, b2) + _dot(sel_b, b3)


def _dot_mid(a, b):
    a1 = a.astype(BF16)
    a2 = (a - a1.astype(F32)).astype(BF16)
    b1 = b.astype(BF16)
    b2 = (b - b1.astype(F32)).astype(BF16)
    return _dot(a1, b1) + (_dot(a1, b2) + _dot(a2, b1))


def _dot_hi(a, b):
    a1, a2, a3 = _split3(a)
    b1, b2, b3 = _split3(b)
    return (_dot(a1, b1) + (_dot(a1, b2) + _dot(a2, b1))
            + (_dot(a1, b3) + _dot(a2, b2) + _dot(a3, b1)))


def _adaln_kernel(c_ref, w_ref, b_ref, o_ref):
    s = _silu(c_ref[...]).astype(BF16)
    o_ref[...] = _dot(s, w_ref[...].astype(BF16)) + b_ref[...]


def _adaln(cc, w, b, tn=1024):
    m, d = cc.shape
    n = w.shape[1]
    return pl.pallas_call(
        _adaln_kernel,
        out_shape=jax.ShapeDtypeStruct((m, n), F32),
        grid=(n // tn,),
        in_specs=[pl.BlockSpec((m, d), lambda j: (0, 0)),
                  pl.BlockSpec((d, tn), lambda j: (0, j)),
                  pl.BlockSpec((1, tn), lambda j: (0, j))],
        out_specs=pl.BlockSpec((m, tn), lambda j: (0, j)),
        compiler_params=_cparams(("parallel",)),
        name="adaln",
    )(cc, w, b.reshape(1, n))


def _normmod_kernel(x_ref, g_ref, sh_ref, sc_ref, o_ref):
    x = x_ref[...]
    y = x * lax.rsqrt(jnp.mean(x * x, axis=-1, keepdims=True) + EPS) * g_ref[...]
    o_ref[...] = (y * (1.0 + sc_ref[...]) + sh_ref[...]).astype(o_ref.dtype)


def _normmod(x, g, sh, sc, tm):
    m, d = x.shape
    row = pl.BlockSpec((1, d), lambda i: (0, 0))
    return pl.pallas_call(
        _normmod_kernel,
        out_shape=jax.ShapeDtypeStruct((m, d), BF16),
        grid=(m // tm,),
        in_specs=[pl.BlockSpec((tm, d), lambda i: (i, 0)), row, row, row],
        out_specs=pl.BlockSpec((tm, d), lambda i: (i, 0)),
        compiler_params=_cparams(("parallel",)),
        name="normmod",
    )(x, g.reshape(1, d), sh, sc)


def _conv_taps(cw_ref, rows, group):
    n_taps, tn = cw_ref.shape
    half = n_taps // 2
    pos = lax.broadcasted_iota(jnp.int32, (rows, tn), 0) & (group - 1)
    taps = []
    for k in range(n_taps):
        d = k - half
        wk = cw_ref[k:k + 1, :]
        if d != 0:
            wk = jnp.where((pos + d >= 0) & (pos + d < group), wk, 0.0)
        taps.append((d, wk))
    return taps


def _conv_rows(acc, taps):
    rows = acc.shape[0]
    out = None
    for d, wk in taps:
        term = (acc if d == 0 else pltpu.roll(acc, (rows - d) % rows, 0)) * wk
        out = term if out is None else out + term
    return out


def _proj_kernel(a_ref, w_ref, *rest, epi, group):
    o_ref = rest[-1]
    tm = a_ref.shape[0]
    sub = max(group, min(tm, PROJ_SUB_ROWS))
    w = w_ref[...].astype(BF16)
    taps = _conv_taps(rest[0], sub, group) if epi in ("conv", "conv_silu") else None
    for s in range(tm // sub):
        acc = _dot_nt(a_ref[s * sub:(s + 1) * sub, :], w)
        if epi == "none":
            out = acc
        elif epi == "conv":
            out = _conv_rows(acc, taps) + rest[1][...]
        elif epi == "conv_silu":
            out = _silu(_conv_rows(acc, taps) + rest[1][...])
        elif epi == "softplus":
            v = acc + rest[0][...]
            out = jnp.maximum(v, 0.0) + jnp.log(1.0 + jnp.exp(-jnp.abs(v)))
        else:
            raise ValueError(epi)
        if len(o_ref.shape) == 3:
            nb, tn = sub // DFT_B, o_ref.shape[2]
            o_ref[s * nb:(s + 1) * nb, 0:DFT_B, :] = out.reshape(nb, DFT_B, tn).astype(o_ref.dtype)
            o_ref[s * nb:(s + 1) * nb, DFT_B:PITCH, :] = jnp.zeros((nb, PITCH - DFT_B, tn), o_ref.dtype)
        else:
            o_ref[s * sub:(s + 1) * sub, :] = out.astype(o_ref.dtype)


def _proj(a, wt, row0, n, epi, tm, tn, conv_w=None, bias=None, group=GRID_W, out_dtype=F32,
          pitched=False):
    m, k = a.shape
    tm = min(tm, m)
    tn = min(tn, n)
    if pitched:
        out_shape = jax.ShapeDtypeStruct((m // DFT_B, PITCH, n), out_dtype)
        out_spec = pl.BlockSpec((tm // DFT_B, PITCH, tn), lambda i, j: (i, 0, j))
    else:
        out_shape = jax.ShapeDtypeStruct((m, n), out_dtype)
        out_spec = pl.BlockSpec((tm, tn), lambda i, j: (i, j))
    extra, extra_specs = [], []
    if conv_w is not None:
        extra.append(conv_w)
        extra_specs.append(pl.BlockSpec((conv_w.shape[0], tn), lambda i, j: (0, j)))
    if bias is not None:
        extra.append(bias.reshape(1, n))
        extra_specs.append(pl.BlockSpec((1, tn), lambda i, j: (0, j)))
    return pl.pallas_call(
        functools.partial(_proj_kernel, epi=epi, group=group),
        out_shape=out_shape,
        grid=(m // tm, n // tn),
        in_specs=[pl.BlockSpec((tm, k), lambda i, j: (i, 0)),
                  pl.BlockSpec((pl.Element(tn), pl.Element(k)),
                               lambda i, j: (pl.multiple_of(row0 + j * tn, ROW_ALIGN), 0))]
        + extra_specs,
        out_specs=out_spec,
        compiler_params=_cparams(("parallel", "parallel")),
        name="proj_" + epi,
    )(a, wt, *extra)


def _ssd_kernel(xs_ref, b_ref, c_ref, dt_ref, dtT_ref, alog_ref, alogT_ref, h0_ref, *rest,
                reverse, mode):
    if mode == "state":
        hfin_ref, st_ref = rest
    elif mode == "fwd":
        y_ref, st_ref = rest
    else:
        yf_ref, z_ref, dx_ref, ng_ref, y_ref, st_ref = rest
    q = xs_ref.shape[0]
    nh = SSD_HEADS
    rp = SSD_D_INNER // SSD_GROUPS
    step = pl.program_id(0)

    @pl.when(step == 0)
    def _():
        st_ref[...] = h0_ref[...]

    d = 1 if reverse else 0
    dtd = dt_ref[:, d * nh:(d + 1) * nh]
    dtdT = dtT_ref[d * nh:(d + 1) * nh, :]
    a_row = -jnp.exp(alog_ref[d:d + 1, :])
    a_col = -jnp.exp(alogT_ref[:, d:d + 1])
    ad = dtd * a_row
    adT = dtdT * a_col
    row = lax.broadcasted_iota(jnp.int32, (q, q), 0)
    col = lax.broadcasted_iota(jnp.int32, (q, q), 1)
    lower = col <= row
    upper = col >= row
    causal = upper if reverse else lower
    tri = jnp.where(causal, 1.0, 0.0).astype(BF16)
    triT = jnp.where(lower if reverse else upper, 1.0, 0.0).astype(BF16)
    acs = _dot_sel_l(tri, ad)
    acsT = _dot_sel_r(adT, triT)
    tot = jnp.sum(ad, axis=0, keepdims=True)
    dte = jnp.exp(tot - acs)
    eacs = jnp.exp(acs)
    cdec = jnp.exp(tot)
    hsel = lax.broadcasted_iota(jnp.int32, (nh, SSD_D_INNER), 0)
    lsel = lax.broadcasted_iota(jnp.int32, (nh, SSD_D_INNER), 1) // SSD_HEADDIM
    expand = jnp.where(hsel == lsel, 1.0, 0.0).astype(BF16)
    stacked = jnp.concatenate([dtd * dte, eacs, jnp.broadcast_to(cdec, (8, nh))], axis=0)
    s1 = stacked.astype(BF16)
    s2 = (stacked - s1.astype(F32)).astype(BF16)
    exp_all = _dot(s1, expand) + _dot(s2, expand)
    w_x = exp_all[0:q]
    eacs_x = exp_all[q:2 * q]
    cdec_x = exp_all[2 * q:2 * q + 1]

    xs = xs_ref[...]
    xs_b = xs.astype(BF16)
    xdw = (xs * w_x).astype(BF16)
    neg_inf = jnp.float32(-jnp.inf)
    y_groups = []
    for g in range(SSD_GROUPS):
        bg = b_ref[:, g * SSD_STATE:(g + 1) * SSD_STATE]
        st_old = st_ref[g]
        s_new = _dot(bg.T.astype(BF16), xdw[:, g * rp:(g + 1) * rp])
        if mode != "state":
            bg_b = bg.astype(BF16)
            cg_b = c_ref[:, g * SSD_STATE:(g + 1) * SSD_STATE].astype(BF16)
            cb = lax.dot_general(cg_b, bg_b, (((1,), (1,)), ((), ())),
                                 preferred_element_type=F32)
            y_off = _dot(cg_b, st_old.astype(BF16)) * eacs_x[:, g * rp:(g + 1) * rp]
            parts = []
            for r in range(SSD_HEADS // SSD_GROUPS):
                h = g * (SSD_HEADS // SSD_GROUPS) + r
                seg = acs[:, h:h + 1] - acsT[h:h + 1, :]
                dec = jnp.exp(jnp.where(causal, seg, neg_inf))
                mat = (cb * dec * dtdT[h:h + 1, :]).astype(BF16)
                parts.append(_dot(mat, xs_b[:, h * SSD_HEADDIM:(h + 1) * SSD_HEADDIM]))
            y_groups.append(jnp.concatenate(parts, axis=1) + y_off)
        st_ref[g] = st_old * cdec_x[:, g * rp:(g + 1) * rp] + s_new

    if mode == "state":
        @pl.when(step == pl.num_programs(0) - 1)
        def _():
            hfin_ref[...] = st_ref[...]
    elif mode == "fwd":
        for g in range(SSD_GROUPS):
            y_ref[:, g * rp:(g + 1) * rp] = y_groups[g]
    else:
        for g in range(SSD_GROUPS):
            sl = slice(g * rp, (g + 1) * rp)
            y = yf_ref[:, sl] + y_groups[g] + dx_ref[:, sl] * xs[:, sl]
            yz = y * _silu(z_ref[:, sl])
            yz = yz * lax.rsqrt(jnp.mean(yz * yz, axis=-1, keepdims=True) + EPS)
            y_ref[:, sl] = (yz * ng_ref[:, sl]).astype(y_ref.dtype)


def _ssd(xbc, dt, dtT, alog, alogT, h0, *, reverse, mode, yf=None, z=None, dx=None, ng=None):
    l = xbc.shape[0]
    q = SSD_CHUNK
    nc = l // q
    di = SSD_D_INNER
    nb = di // SSD_GN
    if reverse:
        cidx = lambda c: nc - 1 - c
    else:
        cidx = lambda c: c
    st_shape = (SSD_GROUPS, SSD_STATE, di // SSD_GROUPS)
    full3 = pl.BlockSpec(st_shape, lambda c: (0, 0, 0))
    in_specs = [pl.BlockSpec((q, di), lambda c: (cidx(c), 0)),
                pl.BlockSpec((q, SSD_GN), lambda c: (cidx(c), nb)),
                pl.BlockSpec((q, SSD_GN), lambda c: (cidx(c), nb + 1)),
                pl.BlockSpec((q, dt.shape[1]), lambda c: (cidx(c), 0)),
                pl.BlockSpec((dtT.shape[0], q), lambda c: (0, cidx(c))),
                pl.BlockSpec((2, SSD_HEADS), lambda c: (0, 0)),
                pl.BlockSpec((SSD_HEADS, 2), lambda c: (0, 0)),
                full3]
    args = [xbc, xbc, xbc, dt, dtT, alog, alogT, h0]
    wide = pl.BlockSpec((q, di), lambda c: (cidx(c), 0))
    rowspec = pl.BlockSpec((1, di), lambda c: (0, 0))
    if mode == "state":
        out_shape = jax.ShapeDtypeStruct(st_shape, F32)
        out_specs = full3
    elif mode == "fwd":
        out_shape = jax.ShapeDtypeStruct((l, di), F32)
        out_specs = wide
    else:
        in_specs += [wide, wide, rowspec, rowspec]
        args += [yf, z, dx, ng]
        out_shape = jax.ShapeDtypeStruct((l, di), BF16)
        out_specs = wide
    return pl.pallas_call(
        functools.partial(_ssd_kernel, reverse=reverse, mode=mode),
        out_shape=out_shape,
        grid=(nc,),
        in_specs=in_specs,
        out_specs=out_specs,
        scratch_shapes=[pltpu.VMEM(st_shape, F32)],
        compiler_params=_cparams(("arbitrary",)),
        name="ssd_%s_%s" % (mode, "rev" if reverse else "fwd"),
    )(*args)


def _dft_tables(l):
    n = 2 * l
    hh = l // DFT_B
    ka = jnp.arange(hh, dtype=jnp.int32)
    n1 = jnp.arange(hh, dtype=jnp.int32)
    n2 = jnp.arange(DFT_B, dtype=jnp.int32)
    odd = 2 * ka + 1
    ang_a = ((odd[:, None] * (DFT_B * n1)[None, :]) % (2 * n)).astype(F32) * (math.pi / n)
    ang_b = ((n2[:, None] * odd[None, :]) % (2 * n)).astype(F32) * (math.pi / n)
    ca, sa = jnp.cos(ang_a)[None, :, :], jnp.sin(ang_a)[None, :, :]
    cb, sb = jnp.cos(ang_b)[:, :, None], jnp.sin(ang_b)[:, :, None]
    cs = ca * cb - sa * sb
    sn = sa * cb + ca * sb
    g1 = jnp.concatenate([cs, -sn], axis=1).astype(BF16)
    scale = 2.0 / n
    h2 = jnp.concatenate([jnp.swapaxes(cs, 1, 2), -jnp.swapaxes(sn, 1, 2)], axis=2) * scale
    kb = jnp.arange(DFT_B, dtype=jnp.int32)
    ph2 = (kb[:, None] * n2[None, :]) % DFT_B
    ang2 = ph2.astype(F32) * (2.0 * math.pi / DFT_B)
    fr, fi = jnp.cos(ang2), -jnp.sin(ang2)
    f2 = jnp.concatenate([jnp.concatenate([fr, -fi], axis=1),
                          jnp.concatenate([fi, fr], axis=1)], axis=0).astype(BF16)
    f2i = jnp.concatenate([jnp.concatenate([fr, fi], axis=1),
                           jnp.concatenate([-fi, fr], axis=1)], axis=0).astype(BF16)
    return g1, f2, f2i, h2.astype(BF16)


def _filter_mlp_kernel(bands_ref, w1_ref, b1_ref, f1_ref, w2_ref, b2_ref, f2_ref, o_ref, *, l):
    tm = o_ref.shape[0]
    base = pl.program_id(0) * tm
    idx = (lax.broadcasted_iota(jnp.int32, (tm, LANES), 0) + base).astype(F32)
    lane = lax.broadcasted_iota(jnp.int32, (tm, LANES), 1)
    t = idx * (1.0 / (l - 1))
    w = idx * (2.0 * math.pi / l)
    arg = bands_ref[...] * w
    feats = jnp.where(lane == 0, t,
                      jnp.where(lane <= HY_BANDS, jnp.cos(arg),
                                jnp.where(lane < HY_EMB, -jnp.sin(arg), 0.0)))
    h = jnp.sin(f1_ref[...] * (_dot_hi(feats, w1_ref[...]) + b1_ref[...]))
    h = jnp.sin(f2_ref[...] * (_dot_hi(h, w2_ref[...]) + b2_ref[...]))
    o_ref[...] = h.astype(o_ref.dtype)


def _filter_mlp(l, w1, b1, f1, w2, b2, f2, tm=1024):
    tm = min(tm, l)
    bands = jnp.linspace(1e-4, HY_BANDS - 1, HY_BANDS, dtype=F32)
    bands_row = jnp.zeros((1, LANES), F32).at[0, 1:1 + HY_BANDS].set(bands)
    bands_row = bands_row.at[0, 1 + HY_BANDS:HY_EMB].set(bands)
    w1p = jnp.zeros((LANES, HY_HIDDEN), F32).at[:HY_EMB].set(w1)
    full = lambda a: pl.BlockSpec(a.shape, lambda i: (0,) * a.ndim)
    args = [bands_row, w1p, b1.reshape(1, -1), f1.reshape(1, -1), w2, b2.reshape(1, -1),
            f2.reshape(1, -1)]
    return pl.pallas_call(
        functools.partial(_filter_mlp_kernel, l=l),
        out_shape=jax.ShapeDtypeStruct((l, HY_HIDDEN), BF16),
        grid=(l // tm,),
        in_specs=[full(a) for a in args],
        out_specs=pl.BlockSpec((tm, HY_HIDDEN), lambda i: (i, 0)),
        compiler_params=_cparams(("parallel",)),
        name="hyena_filter_mlp",
    )(*args)


def _pq_pitch(hh):
    return 2 * hh + 8


def _fwd_stage1(src_ref, g1_ref, pq_ref, hh):
    pqp = _pq_pitch(hh)

    def body(n2, carry):
        rows = src_ref[pl.ds(n2, hh, stride=PITCH), :]
        p = _dot(g1_ref[n2], rows.astype(BF16))
        pq_ref[pl.ds(pl.multiple_of(n2 * pqp, 8), 2 * hh), :] = p
        return carry
    lax.fori_loop(0, DFT_B, body, 0, unroll=UNROLL_OUTER)


def _load_spectrum_rows(pq_ref, ka, hh):
    pqp = _pq_pitch(hh)
    return jnp.concatenate([pq_ref[pl.ds(ka, DFT_B, stride=pqp), :],
                            pq_ref[pl.ds(hh + ka, DFT_B, stride=pqp), :]], axis=0)


def _filter_spec_kernel(hid_ref, wf_ref, wb_ref, dl_ref, g1_ref, f2_ref, kr_ref, ki_ref,
                        s_ref, d_ref, pq_ref, pq2_ref, *, l):
    hh = l // DFT_B
    t = lax.broadcasted_iota(jnp.int32, (l, LANES), 0).astype(F32) * (1.0 / (l - 1))
    dec = jnp.exp(-t * dl_ref[...])
    hid = hid_ref[...]
    hf = _dot(hid, wf_ref[...].astype(BF16)) * dec
    hb = _dot(hid, wb_ref[...].astype(BF16)) * dec
    first = lax.broadcasted_iota(jnp.int32, (l, LANES), 0) == 0
    hb = jnp.where(first, 0.0, hb)
    inv = 1.0 / (jnp.sum(jnp.abs(hf), axis=0, keepdims=True)
                 + jnp.sum(jnp.abs(hb), axis=0, keepdims=True))
    hs = hf + hb
    hd = hf - hb
    for n1 in range(hh):
        s_ref[n1 * PITCH:n1 * PITCH + DFT_B, :] = hs[n1 * DFT_B:(n1 + 1) * DFT_B]
        d_ref[n1 * PITCH:n1 * PITCH + DFT_B, :] = hd[n1 * DFT_B:(n1 + 1) * DFT_B]
    f2_re = f2_ref[0:DFT_B, :]
    pqp = _pq_pitch(hh)

    def stage1(n2, carry):
        g = g1_ref[n2]
        ps = _dot(g, s_ref[pl.ds(n2, hh, stride=PITCH), :].astype(BF16))
        pd = _dot(g, d_ref[pl.ds(n2, hh, stride=PITCH), :].astype(BF16))
        off = pl.multiple_of(n2 * pqp, 8)
        pq_ref[pl.ds(off, 2 * hh), :] = ps
        pq2_ref[pl.ds(off, 2 * hh), :] = jnp.concatenate([pd[hh:], -pd[:hh]], axis=0)
        return carry
    lax.fori_loop(0, DFT_B, stage1, 0, unroll=UNROLL_OUTER)

    def body(ka, carry):
        pp = jnp.concatenate([_load_spectrum_rows(pq_ref, ka, hh).astype(BF16),
                              _load_spectrum_rows(pq2_ref, ka, hh).astype(BF16)], axis=1)
        x = _dot(f2_re, pp)
        rows = pl.ds(pl.multiple_of(ka * DFT_B, DFT_B), DFT_B)
        kr_ref[rows, :] = (x[:, :LANES] * inv).astype(kr_ref.dtype)
        ki_ref[rows, :] = (x[:, LANES:] * inv).astype(ki_ref.dtype)
        return carry
    lax.fori_loop(0, hh, body, 0, unroll=UNROLL_MID)


def _filter_spec(hid, w3, deltas, g1, f2, l, c):
    hh = l // DFT_B
    nct = c // LANES
    orders = w3.shape[1] // (2 * c)
    spec = pl.BlockSpec((None, l, LANES), lambda o, j: (o, 0, j))
    out_sd = jax.ShapeDtypeStruct((orders, l, c), BF16)
    return pl.pallas_call(
        functools.partial(_filter_spec_kernel, l=l),
        out_shape=(out_sd, out_sd),
        grid=(orders, nct),
        in_specs=[pl.BlockSpec((l, HY_HIDDEN), lambda o, j: (0, 0)),
                  pl.BlockSpec((HY_HIDDEN, LANES), lambda o, j: (0, (2 * o) * nct + j)),
                  pl.BlockSpec((HY_HIDDEN, LANES), lambda o, j: (0, (2 * o + 1) * nct + j)),
                  pl.BlockSpec((1, LANES), lambda o, j: (0, j)),
                  pl.BlockSpec(g1.shape, lambda o, j: (0, 0, 0)),
                  pl.BlockSpec(f2.shape, lambda o, j: (0, 0))],
        out_specs=(spec, spec),
        scratch_shapes=[pltpu.VMEM((hh * PITCH, LANES), F32)] * 2
        + [pltpu.VMEM((DFT_B * _pq_pitch(hh), LANES), F32)] * 2,
        compiler_params=_cparams(("parallel", "parallel")),
        name="hyena_filter_spec",
    )(hid, w3, w3, deltas, g1, f2)


def _long_conv_kernel(u_ref, m_ref, kr_ref, ki_ref, bias_ref, g1_ref, f2_ref, f2i_ref, h2_ref,
                      o_ref, pq_ref, *, l):
    hh = l // DFT_B
    pqp = _pq_pitch(hh)
    _fwd_stage1(u_ref, g1_ref, pq_ref, hh)
    f2 = f2_ref[...]
    f2i = f2i_ref[...]

    def mid(ka, carry):
        koff = pl.multiple_of(ka * DFT_B, DFT_B)
        x = _dot(f2, _load_spectrum_rows(pq_ref, ka, hh).astype(BF16))
        xr, xi = x[:DFT_B], x[DFT_B:]
        kr = kr_ref[pl.ds(koff, DFT_B), :].astype(F32)
        ki = ki_ref[pl.ds(koff, DFT_B), :].astype(F32)
        yy = jnp.concatenate([xr * kr - xi * ki, xr * ki + xi * kr], axis=0).astype(BF16)
        qq = _dot(f2i, yy)
        pq_ref[pl.ds(ka, DFT_B, stride=pqp), :] = qq[:DFT_B]
        pq_ref[pl.ds(hh + ka, DFT_B, stride=pqp), :] = qq[DFT_B:]
        return carry
    lax.fori_loop(0, hh, mid, 0, unroll=UNROLL_MID)

    bias = bias_ref[...]

    def last(n2, carry):
        qq = pq_ref[pl.ds(pl.multiple_of(n2 * pqp, 8), 2 * hh), :].astype(BF16)
        y = _dot(h2_ref[n2], qq)
        u = u_ref[pl.ds(n2, hh, stride=PITCH), :]
        m = m_ref[pl.ds(n2, hh, stride=PITCH), :]
        o_ref[pl.ds(n2, hh, stride=PITCH), :] = (m * (y + bias * u)).astype(o_ref.dtype)
        return carry
    lax.fori_loop(0, DFT_B, last, 0, unroll=UNROLL_OUTER)
    for n1 in range(hh):
        o_ref[n1 * PITCH + DFT_B:(n1 + 1) * PITCH, :] = jnp.zeros((PITCH - DFT_B, LANES), o_ref.dtype)


def _long_conv(u_arr, u_blk, m_arr, m_blk, kr, ki, order, bias, tables, l, c):
    g1, f2, f2i, h2 = tables
    hh = l // DFT_B
    nct = c // LANES
    kspec = pl.BlockSpec((None, l, LANES), lambda j: (order, 0, j))
    return pl.pallas_call(
        functools.partial(_long_conv_kernel, l=l),
        out_shape=jax.ShapeDtypeStruct((hh * PITCH, c), F32),
        grid=(nct,),
        in_specs=[pl.BlockSpec((hh * PITCH, LANES), lambda j: (0, u_blk * nct + j)),
                  pl.BlockSpec((hh * PITCH, LANES), lambda j: (0, m_blk * nct + j)),
                  kspec, kspec,
                  pl.BlockSpec((1, LANES), lambda j: (0, j)),
                  pl.BlockSpec(g1.shape, lambda j: (0, 0, 0)),
                  pl.BlockSpec(f2.shape, lambda j: (0, 0)),
                  pl.BlockSpec(f2i.shape, lambda j: (0, 0)),
                  pl.BlockSpec(h2.shape, lambda j: (0, 0, 0))],
        out_specs=pl.BlockSpec((hh * PITCH, LANES), lambda j: (0, j)),
        scratch_shapes=[pltpu.VMEM((DFT_B * _pq_pitch(hh), LANES), F32)],
        compiler_params=_cparams(("parallel",)),
        name="hyena_long_conv",
    )(u_arr, m_arr, kr, ki, bias.reshape(1, c), g1, f2, f2i, h2)


def _merge_kernel(hx_ref, ys_ref, yh_ref, wg1_ref, wg2_ref, w1_ref, w2_ref, gb1_ref, gb2_ref, o_ref):
    hx = hx_ref[...]
    g1 = _sigmoid(_dot_nt(hx, wg1_ref[...]) + gb1_ref[...])
    g2 = _sigmoid(_dot_nt(hx, wg2_ref[...]) + gb2_ref[...])
    yh = yh_ref[:, 0:DFT_B, :].reshape(hx.shape).astype(BF16)
    out = g1 * _dot(ys_ref[...], w1_ref[...]) + g2 * _dot(yh, w2_ref[...])
    o_ref[...] = out.astype(o_ref.dtype)


def _merge(hx, ys, yh, wgt, row0, w1, w2, gate_b, tm=1024, tn=256):
    m, d = hx.shape
    nt = d // tn
    a_spec = pl.BlockSpec((tm, d), lambda i, j: (i, 0))
    w_spec = pl.BlockSpec((d, tn), lambda i, j: (0, j))
    gb = gate_b.reshape(1, 2 * d)
    return pl.pallas_call(
        _merge_kernel,
        out_shape=jax.ShapeDtypeStruct((m, d), BF16),
        grid=(m // tm, nt),
        in_specs=[a_spec, a_spec,
                  pl.BlockSpec((tm // DFT_B, PITCH, d), lambda i, j: (i, 0, 0)),
                  pl.BlockSpec((pl.Element(tn), pl.Element(d)),
                               lambda i, j: (pl.multiple_of(row0 + j * tn, ROW_ALIGN), 0)),
                  pl.BlockSpec((pl.Element(tn), pl.Element(d)),
                               lambda i, j: (pl.multiple_of(row0 + (nt + j) * tn, ROW_ALIGN), 0)),
                  w_spec, w_spec,
                  pl.BlockSpec((1, tn), lambda i, j: (0, j)),
                  pl.BlockSpec((1, tn), lambda i, j: (0, nt + j))],
        out_specs=pl.BlockSpec((tm, tn), lambda i, j: (i, j)),
        compiler_params=_cparams(("parallel", "parallel")),
        name="merge",
    )(hx, ys, yh, wgt, wgt, w1, w2, gb, gb)


def _oproj_kernel(a_ref, w_ref, x_ref, gt_ref, o_ref):
    o_ref[...] = x_ref[...] + gt_ref[...] * _dot(a_ref[...], w_ref[...])


def _oproj(a, w, x, gt, tm=2048, tn=512):
    m, d = a.shape
    n = w.shape[1]
    tm = min(tm, m)
    return pl.pallas_call(
        _oproj_kernel,
        out_shape=jax.ShapeDtypeStruct((m, n), F32),
        grid=(m // tm, n // tn),
        in_specs=[pl.BlockSpec((tm, d), lambda i, j: (i, 0)),
                  pl.BlockSpec((d, tn), lambda i, j: (0, j)),
                  pl.BlockSpec((tm, tn), lambda i, j: (i, j)),
                  pl.BlockSpec((1, tn), lambda i, j: (0, j))],
        out_specs=pl.BlockSpec((tm, tn), lambda i, j: (i, j)),
        compiler_params=_cparams(("parallel", "parallel")),
        name="oproj",
    )(a, w, x, gt)


def _pack_bf16_pairs(v):
    half = v.shape[1] // 2
    bits = lax.bitcast_convert_type(v.astype(BF16).astype(F32), jnp.uint32)
    return (bits[:, :half] >> 16) | (bits[:, half:] & jnp.uint32(0xFFFF0000))


def _unpack_bf16_pairs(p):
    lo = lax.bitcast_convert_type(p << 16, F32).astype(BF16)
    hi = lax.bitcast_convert_type(p & jnp.uint32(0xFFFF0000), F32).astype(BF16)
    return lo, hi


def _router_kernel(x_ref, g_ref, sh_ref, sc_ref, wr_ref, br_ref, h_ref, ids_ref, wts_ref):
    x = x_ref[...]
    y = x * lax.rsqrt(jnp.mean(x * x, axis=-1, keepdims=True) + EPS) * g_ref[...]
    h = y * (1.0 + sc_ref[...]) + sh_ref[...]
    h_ref[...] = _pack_bf16_pairs(h)
    logits = _dot_mid(h, wr_ref[...]) + br_ref[...]
    tm = x.shape[0]
    lane = lax.broadcasted_iota(jnp.int32, (tm, LANES), 1)
    neg = jnp.float32(-jnp.inf)
    big = jnp.int32(LANES)
    is_grp = (lane >= MOE_EXPERTS) & (lane < MOE_EXPERTS + MOE_GROUPS)
    gl = jnp.where(is_grp, logits, neg)
    gmax = jnp.max(gl, axis=-1, keepdims=True)
    gidx = jnp.min(jnp.where(gl == gmax, lane, big), axis=-1, keepdims=True) - MOE_EXPERTS
    gw = 1.0 / jnp.sum(jnp.where(is_grp, jnp.exp(logits - gmax), 0.0), axis=-1, keepdims=True)
    in_grp = (lane < MOE_EXPERTS) & ((lane // MOE_PER_GROUP) == gidx)
    el = jnp.where(in_grp, logits, neg)
    m1 = jnp.max(el, axis=-1, keepdims=True)
    i1 = jnp.min(jnp.where(el == m1, lane, big), axis=-1, keepdims=True)
    el2 = jnp.where(lane == i1, neg, el)
    m2 = jnp.max(el2, axis=-1, keepdims=True)
    i2 = jnp.min(jnp.where(el2 == m2, lane, big), axis=-1, keepdims=True)
    e21 = jnp.exp(m2 - m1)
    w1 = gw / (1.0 + e21)
    w2 = gw * e21 / (1.0 + e21)
    ids_ref[...] = jnp.where(lane == 0, i1, jnp.where(lane == 1, i2, -1))
    wts_ref[...] = jnp.where(lane == 0, w1, jnp.where(lane == 1, w2, 0.0))


def _router(x1, g, sh, sc, w_rg, b_rg, w_re, b_re, tm=512):
    m, d = x1.shape
    wr = jnp.zeros((d, LANES), F32).at[:, :MOE_EXPERTS].set(w_re)
    wr = wr.at[:, MOE_EXPERTS:MOE_EXPERTS + MOE_GROUPS].set(w_rg)
    br = jnp.zeros((1, LANES), F32).at[0, :MOE_EXPERTS].set(b_re)
    br = br.at[0, MOE_EXPERTS:MOE_EXPERTS + MOE_GROUPS].set(b_rg)
    row = pl.BlockSpec((1, d), lambda i: (0, 0))
    return pl.pallas_call(
        _router_kernel,
        out_shape=(jax.ShapeDtypeStruct((m, d // 2), jnp.uint32),
                   jax.ShapeDtypeStruct((m, LANES), jnp.int32),
                   jax.ShapeDtypeStruct((m, LANES), F32)),
        grid=(m // tm,),
        in_specs=[pl.BlockSpec((tm, d), lambda i: (i, 0)), row, row, row,
                  pl.BlockSpec((d, LANES), lambda i: (0, 0)),
                  pl.BlockSpec((1, LANES), lambda i: (0, 0))],
        out_specs=(pl.BlockSpec((tm, d // 2), lambda i: (i, 0)),
                   pl.BlockSpec((tm, LANES), lambda i: (i, 0)),
                   pl.BlockSpec((tm, LANES), lambda i: (i, 0))),
        compiler_params=_cparams(("parallel",)),
        name="router",
    )(x1, g.reshape(1, d), sh, sc, wr, br)


def _moe_positions_kernel(ids_ref, pos_ref, meta_ref):
    n = ids_ref.shape[0]
    t = MOE_TILE
    lane = lax.broadcasted_iota(jnp.int32, (t, LANES), 1)

    def onehot(k):
        idt = ids_ref[pl.ds(pl.multiple_of(k * t, t), t), :]
        i1, i2 = idt[:, 0:1], idt[:, 1:2]
        return i1, i2, jnp.where((lane == i1) | (lane == i2), 1.0, 0.0)

    def count(k, acc):
        return acc + jnp.sum(onehot(k)[2], axis=0, keepdims=True)
    total = lax.fori_loop(0, n // t, count, jnp.zeros((1, LANES), F32))
    padded = (((total.astype(jnp.int32) + (t - 1)) // t) * t).astype(F32)
    r128 = lax.broadcasted_iota(jnp.int32, (LANES, LANES), 0)
    c128 = lax.broadcasted_iota(jnp.int32, (LANES, LANES), 1)
    before = jnp.where(r128 < c128, 1.0, 0.0).astype(BF16)
    off = _dot_sel_r(jnp.broadcast_to(padded, (8, LANES)), before)[0:1]
    row = lax.broadcasted_iota(jnp.int32, (t, t), 0)
    col = lax.broadcasted_iota(jnp.int32, (t, t), 1)
    tri = jnp.where(col <= row, 1.0, 0.0).astype(BF16)

    def place(k, seen):
        i1, i2, oh = onehot(k)
        base = off + seen + _dot(tri, oh.astype(BF16)) - oh
        p1 = jnp.sum(jnp.where(lane == i1, base, 0.0), axis=1, keepdims=True)
        p2 = jnp.sum(jnp.where(lane == i2, base, 0.0), axis=1, keepdims=True)
        pos_ref[pl.ds(pl.multiple_of(k * t, t), t), :] = jnp.where(
            lane == 0, p1, jnp.where(lane == 1, p2, 0.0)).astype(jnp.int32)
        return seen + jnp.sum(oh, axis=0, keepdims=True)
    lax.fori_loop(0, n // t, place, jnp.zeros((1, LANES), F32))

    ends = off + padded
    start = (r128 * t).astype(F32)
    done = jnp.where((jnp.broadcast_to(ends, (LANES, LANES)) <= start) & (c128 < MOE_EXPERTS), 1.0, 0.0)
    tile_expert = jnp.minimum(jnp.sum(done, axis=1, keepdims=True), MOE_EXPERTS - 1.0)
    used = jnp.sum(jnp.where(c128[0:1] == MOE_EXPERTS - 1, ends, 0.0), axis=1, keepdims=True) / t
    lanef = c128.astype(F32)
    later = (lanef > tile_expert) & (jnp.broadcast_to(padded, (LANES, LANES)) > 0.0) & (c128 < MOE_EXPERTS)
    nxt = jnp.min(jnp.where(later, lanef, float(LANES)), axis=1, keepdims=True)
    nxt = jnp.where(nxt >= float(LANES), -1.0, nxt)
    meta_ref[...] = jnp.where(c128 == 0, tile_expert,
                              jnp.where(c128 == 1, used, jnp.where(c128 == 2, nxt, 0.0))).astype(jnp.int32)


def _moe_positions(ids):
    m = ids.shape[0]
    return pl.pallas_call(
        _moe_positions_kernel,
        out_shape=(jax.ShapeDtypeStruct((m, LANES), jnp.int32),
                   jax.ShapeDtypeStruct((LANES, LANES), jnp.int32)),
        grid=(1,),
        in_specs=[pl.BlockSpec((m, LANES), lambda i: (0, 0))],
        out_specs=(pl.BlockSpec((m, LANES), lambda i: (0, 0)),
                   pl.BlockSpec((LANES, LANES), lambda i: (0, 0))),
        compiler_params=_cparams(("arbitrary",)),
        name="moe_positions",
    )(ids)


def _moe_dispatch_kernel(p1_ref, p2_ref, h_ref, xs_in_ref, xs_ref, sem):
    del xs_in_ref
    t = h_ref.shape[0]
    base = pl.program_id(0) * t

    def issue(r, carry):
        src = h_ref.at[pl.ds(r, 1), :]
        pltpu.make_async_copy(src, xs_ref.at[pl.ds(p1_ref[base + r], 1), :], sem.at[0]).start()
        pltpu.make_async_copy(src, xs_ref.at[pl.ds(p2_ref[base + r], 1), :], sem.at[1]).start()
        return carry
    lax.fori_loop(0, t, issue, 0, unroll=8)
    pltpu.make_async_copy(h_ref, xs_ref.at[pl.ds(0, t), :], sem.at[0]).wait()
    pltpu.make_async_copy(h_ref, xs_ref.at[pl.ds(0, t), :], sem.at[1]).wait()


def _moe_dispatch(hp, pos1, pos2, n_sorted):
    m, w = hp.shape
    t = MOE_TILE
    zeros = jnp.zeros((n_sorted, w), hp.dtype)
    return pl.pallas_call(
        _moe_dispatch_kernel,
        out_shape=jax.ShapeDtypeStruct((n_sorted, w), hp.dtype),
        grid_spec=pltpu.PrefetchScalarGridSpec(
            num_scalar_prefetch=2,
            grid=(m // t,),
            in_specs=[pl.BlockSpec((t, w), lambda i, p1, p2: (i, 0)),
                      pl.BlockSpec(memory_space=pl.ANY)],
            out_specs=pl.BlockSpec(memory_space=pl.ANY),
            scratch_shapes=[pltpu.SemaphoreType.DMA((2,))]),
        input_output_aliases={3: 0},
        compiler_params=_cparams(("arbitrary",)),
        name="moe_dispatch",
    )(pos1, pos2, hp, zeros)


def _moe_experts_kernel(te_ref, nu_ref, nx_ref, x_ref, wg_hbm, wu_hbm, wd_hbm, y_ref,
                        wg_f, wu_f, wd_f, wg_b, wu_b, wd_b, slot_ref, sem):
    i = pl.program_id(0)
    live = i < nu_ref[0]
    new_expert = (i == 0) | (te_ref[i] != te_ref[jnp.maximum(i - 1, 0)])

    def copies(e, s):
        return (pltpu.make_async_copy(wg_hbm.at[e], wg_f.at[s], sem.at[s, 0]),
                pltpu.make_async_copy(wu_hbm.at[e], wu_f.at[s], sem.at[s, 1]),
                pltpu.make_async_copy(wd_hbm.at[e], wd_f.at[s], sem.at[s, 2]))

    @pl.when(i == 0)
    def _():
        slot_ref[0] = 0
        for cp in copies(te_ref[0], 0):
            cp.start()

    @pl.when(live & new_expert)
    def _():
        s = slot_ref[0]
        for cp in copies(te_ref[i], s):
            cp.wait()
        wg_b[...] = wg_f[s].astype(BF16)
        wu_b[...] = wu_f[s].astype(BF16)
        wd_b[...] = wd_f[s].astype(BF16)
        slot_ref[0] = 1 - s

        @pl.when(nx_ref[i] >= 0)
        def _():
            for cp in copies(nx_ref[i], 1 - s):
                cp.start()

    @pl.when(live)
    def _():
        x_lo, x_hi = _unpack_bf16_pairs(x_ref[...])
        half = x_lo.shape[1]
        hg = _dot(x_lo, wg_b[0:half, :]) + _dot(x_hi, wg_b[half:, :])
        hu = _dot(x_lo, wu_b[0:half, :]) + _dot(x_hi, wu_b[half:, :])
        y_ref[...] = _dot((_silu(hg) * hu).astype(BF16), wd_b[...])

    @pl.when(jnp.logical_not(live))
    def _():
        y_ref[...] = jnp.zeros_like(y_ref)


def _moe_experts(xs, tile_expert, n_used, next_expert, wg, wu, wd):
    ns = xs.shape[0]
    ne, d, f = wg.shape
    t = MOE_TILE
    hbm = pl.BlockSpec(memory_space=pl.ANY)
    return pl.pallas_call(
        _moe_experts_kernel,
        out_shape=jax.ShapeDtypeStruct((ns, d), F32),
        grid_spec=pltpu.PrefetchScalarGridSpec(
            num_scalar_prefetch=3,
            grid=(ns // t,),
            in_specs=[pl.BlockSpec((t, d // 2), lambda i, te, nu, nx: (i, 0)), hbm, hbm, hbm],
            out_specs=pl.BlockSpec((t, d), lambda i, te, nu, nx: (i, 0)),
            scratch_shapes=[pltpu.VMEM((2, d, f), F32), pltpu.VMEM((2, d, f), F32),
                            pltpu.VMEM((2, f, d), F32),
                            pltpu.VMEM((d, f), BF16), pltpu.VMEM((d, f), BF16),
                            pltpu.VMEM((f, d), BF16),
                            pltpu.SMEM((1,), jnp.int32), pltpu.SemaphoreType.DMA((2, 3))]),
        compiler_params=_cparams(("arbitrary",)),
        name="moe_experts",
    )(tile_expert, n_used, next_expert, xs, wg, wu, wd)


def _moe_combine_kernel(p1_ref, p2_ref, x_ref, wts_ref, gt_ref, fg_ref, y_ref, o_ref, ya, yb, sem):
    t = x_ref.shape[0]
    i = pl.program_id(0)
    n = pl.num_programs(0)

    def row_copy(src_row, buf, slot, r, which):
        return pltpu.make_async_copy(y_ref.at[pl.ds(src_row, 1), :], buf.at[slot, pl.ds(r, 1), :],
                                     sem.at[which, slot])

    def issue(tile, slot):
        def body(r, carry):
            row_copy(p1_ref[tile * t + r], ya, slot, r, 0).start()
            row_copy(p2_ref[tile * t + r], yb, slot, r, 1).start()
            return carry
        lax.fori_loop(0, t, body, 0, unroll=8)

    @pl.when(i == 0)
    def _():
        issue(0, 0)

    @pl.when(i + 1 < n)
    def _():
        issue(i + 1, (i + 1) % 2)

    slot = i % 2
    pltpu.make_async_copy(y_ref.at[pl.ds(0, t), :], ya.at[slot], sem.at[0, slot]).wait()
    pltpu.make_async_copy(y_ref.at[pl.ds(0, t), :], yb.at[slot], sem.at[1, slot]).wait()
    w = wts_ref[...]
    moe = w[:, 0:1] * ya[slot] + w[:, 1:2] * yb[slot]
    x2 = x_ref[...] + gt_ref[...] * moe
    o_ref[...] = x2 * lax.rsqrt(jnp.mean(x2 * x2, axis=-1, keepdims=True) + EPS) * fg_ref[...]


def _moe_combine(x1, wts, gt, fg, y_sorted, pos1, pos2):
    m, d = x1.shape
    t = MOE_TILE
    row = pl.BlockSpec((1, d), lambda i, p1, p2: (0, 0))
    return pl.pallas_call(
        _moe_combine_kernel,
        out_shape=jax.ShapeDtypeStruct((m, d), F32),
        grid_spec=pltpu.PrefetchScalarGridSpec(
            num_scalar_prefetch=2,
            grid=(m // t,),
            in_specs=[pl.BlockSpec((t, d), lambda i, p1, p2: (i, 0)),
                      pl.BlockSpec((t, LANES), lambda i, p1, p2: (i, 0)),
                      row, row,
                      pl.BlockSpec(memory_space=pl.ANY)],
            out_specs=pl.BlockSpec((t, d), lambda i, p1, p2: (i, 0)),
            scratch_shapes=[pltpu.VMEM((2, t, d), F32), pltpu.VMEM((2, t, d), F32),
                            pltpu.SemaphoreType.DMA((2, 2))]),
        compiler_params=_cparams(("arbitrary",)),
        name="moe_combine",
    )(pos1, pos2, x1, wts, gt, fg.reshape(1, d), y_sorted)


def kernel(x, c, ctx, c_ctx, ada_w, ada_b, norm1_g, w_in, ssd_conv_w, ssd_conv_b, ssd_dt_bias,
           ssd_a_log, ssd_d, ssd_norm_g, w_ssd_out, hy_conv_w, hy_conv_b, hy_f_w1, hy_f_b1,
           hy_f_freq1, hy_f_w2, hy_f_b2, hy_f_freq2, hy_f_w3, hy_bias, w_hy_out, gate_b, w_o,
           norm2_g, moe_w_rg, moe_b_rg, moe_w_re, moe_b_re, moe_w_gate, moe_w_up, moe_w_down,
           final_g):
    bsz, l, d = x.shape
    assert bsz == 1 and ada_w.shape[0] == 1
    lc = ctx.shape[1]
    di = SSD_D_INNER
    xbc_cols = di + 2 * SSD_GN
    col_xbc = di
    col_dt = col_xbc + xbc_cols
    col_hy = col_dt + 2 * SSD_HEADS
    hyw = hy_bias.shape[-1]

    x2d = x[0]
    ctx2d = ctx[0]

    cc = jnp.zeros((8, d), F32).at[0].set(c[0]).at[1].set(c_ctx)
    mods = _adaln(cc, ada_w[0], ada_b[0])
    sh1, sc1, gt1, sh2, sc2, gt2 = [mods[0:1, i * d:(i + 1) * d] for i in range(6)]
    csh1, csc1 = mods[1:2, 0:d], mods[1:2, d:2 * d]

    hx = _normmod(x2d, norm1_g[0], sh1, sc1, tm=512)
    hc = _normmod(ctx2d, norm1_g[0], csh1, csc1, tm=lc)

    w_in_t = w_in[0].T
    col_gate = col_hy + 3 * hyw
    wt_gate = w_in_t[col_gate:].astype(BF16)
    n_dt = 2 * SSD_HEADS
    dt_bias = ssd_dt_bias[0].reshape(-1)

    xbc_c = _proj(hc, w_in_t, col_xbc, xbc_cols, "conv_silu", tm=lc, tn=512, conv_w=ssd_conv_w[0],
                  bias=ssd_conv_b[0], group=lc)
    dt_c = _proj(hc, w_in_t, col_dt, n_dt, "softplus", tm=lc, tn=n_dt, bias=dt_bias)
    alog = ssd_a_log[0]
    alogT = alog.T
    zero_state = jnp.zeros((SSD_GROUPS, SSD_STATE, di // SSD_GROUPS), F32)
    h0f = _ssd(xbc_c, dt_c, dt_c.T, alog, alogT, zero_state, reverse=False, mode="state")
    h0b = _ssd(xbc_c, dt_c, dt_c.T, alog, alogT, zero_state, reverse=True, mode="state")

    z = _proj(hx, w_in_t, 0, di, "none", tm=PROJ_TM, tn=512)
    xbc = _proj(hx, w_in_t, col_xbc, xbc_cols, "conv_silu", tm=PROJ_TM, tn=512, conv_w=ssd_conv_w[0],
                bias=ssd_conv_b[0])
    dt = _proj(hx, w_in_t, col_dt, n_dt, "softplus", tm=PROJ_TM, tn=n_dt, bias=dt_bias)
    u3 = _proj(hx, w_in_t, col_hy, 3 * hyw, "conv", tm=PROJ_TM, tn=512, conv_w=hy_conv_w[0],
               bias=hy_conv_b[0], pitched=True).reshape(l // DFT_B * PITCH, 3 * hyw)

    dtT = dt.T
    yf = _ssd(xbc, dt, dtT, alog, alogT, h0f, reverse=False, mode="fwd")
    dx = jnp.repeat(ssd_d[0], SSD_HEADDIM).reshape(1, di)
    y_ssd = _ssd(xbc, dt, dtT, alog, alogT, h0b, reverse=True, mode="bwd", yf=yf, z=z, dx=dx,
                 ng=ssd_norm_g[0].reshape(1, di))

    tables = _dft_tables(l)
    hid = _filter_mlp(l, hy_f_w1[0], hy_f_b1[0], hy_f_freq1[0], hy_f_w2[0], hy_f_b2[0],
                      hy_f_freq2[0])
    max_decay = math.log(1e-2) / 0.3
    min_decay = math.log(1e-2) / 1.5
    deltas = jnp.abs(jnp.linspace(min_decay, max_decay, hyw, dtype=F32)).reshape(1, hyw)
    kr, ki = _filter_spec(hid, hy_f_w3[0], deltas, tables[0], tables[1], l, hyw)
    z2 = _long_conv(u3, 0, u3, 1, kr, ki, 0, hy_bias[0, 0], tables, l, hyw)
    y_hy = _long_conv(z2, 0, u3, 2, kr, ki, 1, hy_bias[0, 1], tables, l, hyw)

    merged = _merge(hx, y_ssd, y_hy.reshape(l // DFT_B, PITCH, hyw), wt_gate, 0,
                    w_ssd_out[0].astype(BF16), w_hy_out[0].astype(BF16), gate_b[0])
    x1 = _oproj(merged, w_o[0].astype(BF16), x2d, gt1)

    h2, ids, wts = _router(x1, norm2_g[0], sh2, sc2, moe_w_rg[0], moe_b_rg[0], moe_w_re[0],
                           moe_b_re[0])
    pos, meta = _moe_positions(ids)
    pos1, pos2 = pos[:, 0], pos[:, 1]
    n_sorted = 2 * l + MOE_EXPERTS * MOE_TILE
    n_tiles = n_sorted // MOE_TILE
    xs = _moe_dispatch(h2, pos1, pos2, n_sorted)
    ys = _moe_experts(xs, meta[:n_tiles, 0], meta[0, 1:2], meta[:n_tiles, 2], moe_w_gate[0],
                      moe_w_up[0], moe_w_down[0])
    out = _moe_combine(x1, wts, gt2, final_g, ys, pos1, pos2)
    return out[None]
```

```python
import functools
import math

import jax
import jax.numpy as jnp
from jax import lax
from jax.experimental import pallas as pl
from jax.experimental.pallas import tpu as pltpu

F32 = jnp.float32
BF16 = jnp.bfloat16

EPS = 1e-6
GRID_W = 64
SSD_HEADS = 32
SSD_HEADDIM = 64
SSD_GROUPS = 4
SSD_STATE = 128
SSD_CHUNK = 128
SSD_D_INNER = SSD_HEADS * SSD_HEADDIM
SSD_GN = SSD_GROUPS * SSD_STATE
HY_BANDS = 16
HY_EMB = 33
HY_HIDDEN = 64
MOE_GROUPS = 4
MOE_PER_GROUP = 4
MOE_EXPERTS = 16

LANES = 128
DFT_B = 128
PITCH = 136
VMEM_LIMIT = 56 * 1024 * 1024
UNROLL_OUTER = 16
UNROLL_MID = 16
ROW_ALIGN = 16
PROJ_TM = 2048
PROJ_SUB_ROWS = 512
SSD_STEP_CHUNKS = 4
MOE_TILE = 256


def _cparams(sem):
    return pltpu.CompilerParams(dimension_semantics=sem, vmem_limit_bytes=VMEM_LIMIT)


def _dot(a, b):
    return jnp.dot(a, b, preferred_element_type=F32)


def _dot_nt(a, bt):
    return lax.dot_general(a, bt, (((1,), (1,)), ((), ())), preferred_element_type=F32)


def _sigmoid(x):
    return 1.0 / (1.0 + jnp.exp(-x))


def _silu(x):
    return x * _sigmoid(x)


def _split3(v):
    b1 = v.astype(BF16)
    r1 = v - b1.astype(F32)
    b2 = r1.astype(BF16)
    b3 = (r1 - b2.astype(F32)).astype(BF16)
    return b1, b2, b3


def _dot_sel_r(v, sel_b):
    b1, b2, b3 = _split3(v)
    return _dot(b1, sel_b) + _dot(b2, sel_b) + _dot(b3, sel_b)


def _dot_sel_l(sel_b, v):
    b1, b2, b3 = _split3(v)
    return _dot(sel_b, b1) + _dot(sel_b, b2) + _dot(sel_b, b3)


def _dot_mid(a, b):
    a1 = a.astype(BF16)
    a2 = (a - a1.astype(F32)).astype(BF16)
    b1 = b.astype(BF16)
    b2 = (b - b1.astype(F32)).astype(BF16)
    return _dot(a1, b1) + (_dot(a1, b2) + _dot(a2, b1))


def _dot_hi(a, b):
    a1, a2, a3 = _split3(a)
    b1, b2, b3 = _split3(b)
    return (_dot(a1, b1) + (_dot(a1, b2) + _dot(a2, b1))
            + (_dot(a1, b3) + _dot(a2, b2) + _dot(a3, b1)))


def _adaln_kernel(c_ref, w_ref, b_ref, o_ref):
    s = _silu(c_ref[...]).astype(BF16)
    o_ref[...] = _dot(s, w_ref[...].astype(BF16)) + b_ref[...]


def _adaln(cc, w, b, tn=1024):
    m, d = cc.shape
    n = w.shape[1]
    return pl.pallas_call(
        _adaln_kernel,
        out_shape=jax.ShapeDtypeStruct((m, n), F32),
        grid=(n // tn,),
        in_specs=[pl.BlockSpec((m, d), lambda j: (0, 0)),
                  pl.BlockSpec((d, tn), lambda j: (0, j)),
                  pl.BlockSpec((1, tn), lambda j: (0, j))],
        out_specs=pl.BlockSpec((m, tn), lambda j: (0, j)),
        compiler_params=_cparams(("parallel",)),
        name="adaln",
    )(cc, w, b.reshape(1, n))


def _normmod_kernel(x_ref, g_ref, sh_ref, sc_ref, o_ref):
    x = x_ref[...]
    y = x * lax.rsqrt(jnp.mean(x * x, axis=-1, keepdims=True) + EPS) * g_ref[...]
    o_ref[...] = (y * (1.0 + sc_ref[...]) + sh_ref[...]).astype(o_ref.dtype)


def _normmod(x, g, sh, sc, tm):
    m, d = x.shape
    row = pl.BlockSpec((1, d), lambda i: (0, 0))
    return pl.pallas_call(
        _normmod_kernel,
        out_shape=jax.ShapeDtypeStruct((m, d), BF16),
        grid=(m // tm,),
        in_specs=[pl.BlockSpec((tm, d), lambda i: (i, 0)), row, row, row],
        out_specs=pl.BlockSpec((tm, d), lambda i: (i, 0)),
        compiler_params=_cparams(("parallel",)),
        name="normmod",
    )(x, g.reshape(1, d), sh, sc)


def _conv_taps(cw_ref, rows, group):
    n_taps, tn = cw_ref.shape
    half = n_taps // 2
    pos = lax.broadcasted_iota(jnp.int32, (rows, tn), 0) & (group - 1)
    taps = []
    for k in range(n_taps):
        d = k - half
        wk = cw_ref[k:k + 1, :]
        if d != 0:
            wk = jnp.where((pos + d >= 0) & (pos + d < group), wk, 0.0)
        taps.append((d, wk))
    return taps


def _conv_rows(acc, taps):
    rows = acc.shape[0]
    out = None
    for d, wk in taps:
        term = (acc if d == 0 else pltpu.roll(acc, (rows - d) % rows, 0)) * wk
        out = term if out is None else out + term
    return out


def _proj_kernel(a_ref, w_ref, *rest, epi, group):
    o_ref = rest[-1]
    tm = a_ref.shape[0]
    sub = max(group, min(tm, PROJ_SUB_ROWS))
    w = w_ref[...].astype(BF16)
    taps = _conv_taps(rest[0], sub, group) if epi in ("conv", "conv_silu") else None
    for s in range(tm // sub):
        acc = _dot_nt(a_ref[s * sub:(s + 1) * sub, :], w)
        if epi == "none":
            out = acc
        elif epi == "conv":
            out = _conv_rows(acc, taps) + rest[1][...]
        elif epi == "conv_silu":
            out = _silu(_conv_rows(acc, taps) + rest[1][...])
        elif epi == "softplus":
            v = acc + rest[0][...]
            out = jnp.maximum(v, 0.0) + jnp.log(1.0 + jnp.exp(-jnp.abs(v)))
        else:
            raise ValueError(epi)
        if len(o_ref.shape) == 3:
            nb, tn = sub // DFT_B, o_ref.shape[2]
            o_ref[s * nb:(s + 1) * nb, 0:DFT_B, :] = out.reshape(nb, DFT_B, tn).astype(o_ref.dtype)
            o_ref[s * nb:(s + 1) * nb, DFT_B:PITCH, :] = jnp.zeros((nb, PITCH - DFT_B, tn), o_ref.dtype)
        else:
            o_ref[s * sub:(s + 1) * sub, :] = out.astype(o_ref.dtype)


def _proj(a, wt, row0, n, epi, tm, tn, conv_w=None, bias=None, group=GRID_W, out_dtype=F32,
          pitched=False):
    m, k = a.shape
    tm = min(tm, m)
    tn = min(tn, n)
    if pitched:
        out_shape = jax.ShapeDtypeStruct((m // DFT_B, PITCH, n), out_dtype)
        out_spec = pl.BlockSpec((tm // DFT_B, PITCH, tn), lambda i, j: (i, 0, j))
    else:
        out_shape = jax.ShapeDtypeStruct((m, n), out_dtype)
        out_spec = pl.BlockSpec((tm, tn), lambda i, j: (i, j))
    extra, extra_specs = [], []
    if conv_w is not None:
        extra.append(conv_w)
        extra_specs.append(pl.BlockSpec((conv_w.shape[0], tn), lambda i, j: (0, j)))
    if bias is not None:
        extra.append(bias.reshape(1, n))
        extra_specs.append(pl.BlockSpec((1, tn), lambda i, j: (0, j)))
    return pl.pallas_call(
        functools.partial(_proj_kernel, epi=epi, group=group),
        out_shape=out_shape,
        grid=(m // tm, n // tn),
        in_specs=[pl.BlockSpec((tm, k), lambda i, j: (i, 0)),
                  pl.BlockSpec((pl.Element(tn), pl.Element(k)),
                               lambda i, j: (pl.multiple_of(row0 + j * tn, ROW_ALIGN), 0))]
        + extra_specs,
        out_specs=out_spec,
        compiler_params=_cparams(("parallel", "parallel")),
        name="proj_" + epi,
    )(a, wt, *extra)


def _ssd_kernel(xs_ref, b_ref, c_ref, dt_ref, dtT_ref, alog_ref, alogT_ref, h0_ref, *rest,
                reverse, mode):
    if mode == "state":
        hfin_ref, st_ref = rest
    elif mode == "fwd":
        y_ref, st_ref = rest
    else:
        yf_ref, z_ref, dx_ref, ng_ref, y_ref, st_ref = rest
    q = SSD_CHUNK
    n_sub = xs_ref.shape[0] // q
    nh = SSD_HEADS
    rp = SSD_D_INNER // SSD_GROUPS
    step = pl.program_id(0)

    @pl.when(step == 0)
    def _():
        st_ref[...] = h0_ref[...]

    d = 1 if reverse else 0
    a_row = -jnp.exp(alog_ref[d:d + 1, :])
    a_col = -jnp.exp(alogT_ref[:, d:d + 1])
    row = lax.broadcasted_iota(jnp.int32, (q, q), 0)
    col = lax.broadcasted_iota(jnp.int32, (q, q), 1)
    lower = col <= row
    upper = col >= row
    causal = upper if reverse else lower
    tri = jnp.where(causal, 1.0, 0.0).astype(BF16)
    triT = jnp.where(lower if reverse else upper, 1.0, 0.0).astype(BF16)
    hsel = lax.broadcasted_iota(jnp.int32, (nh, SSD_D_INNER), 0)
    lsel = lax.broadcasted_iota(jnp.int32, (nh, SSD_D_INNER), 1) // SSD_HEADDIM
    expand = jnp.where(hsel == lsel, 1.0, 0.0).astype(BF16)
    neg_inf = jnp.float32(-jnp.inf)

    def one_chunk(r0):
        rows = slice(r0, r0 + q)
        dtd = dt_ref[rows, d * nh:(d + 1) * nh]
        dtdT = dtT_ref[d * nh:(d + 1) * nh, rows]
        ad = dtd * a_row
        adT = dtdT * a_col
        acs = _dot_sel_l(tri, ad)
        acsT = _dot_sel_r(adT, triT)
        tot = jnp.sum(ad, axis=0, keepdims=True)
        dte = jnp.exp(tot - acs)
        eacs = jnp.exp(acs)
        cdec = jnp.exp(tot)
        stacked = jnp.concatenate([dtd * dte, eacs, jnp.broadcast_to(cdec, (8, nh))], axis=0)
        s1 = stacked.astype(BF16)
        s2 = (stacked - s1.astype(F32)).astype(BF16)
        exp_all = _dot(s1, expand) + _dot(s2, expand)
        w_x = exp_all[0:q]
        eacs_x = exp_all[q:2 * q]
        cdec_x = exp_all[2 * q:2 * q + 1]

        xs = xs_ref[rows, :]
        xs_b = xs.astype(BF16)
        xdw = (xs * w_x).astype(BF16)
        y_groups = []
        for g in range(SSD_GROUPS):
            bg = b_ref[rows, g * SSD_STATE:(g + 1) * SSD_STATE]
            st_old = st_ref[g]
            s_new = _dot(bg.T.astype(BF16), xdw[:, g * rp:(g + 1) * rp])
            if mode != "state":
                bg_b = bg.astype(BF16)
                cg_b = c_ref[rows, g * SSD_STATE:(g + 1) * SSD_STATE].astype(BF16)
                cb = lax.dot_general(cg_b, bg_b, (((1,), (1,)), ((), ())),
                                     preferred_element_type=F32)
                y_off = _dot(cg_b, st_old.astype(BF16)) * eacs_x[:, g * rp:(g + 1) * rp]
                parts = []
                for r in range(SSD_HEADS // SSD_GROUPS):
                    h = g * (SSD_HEADS // SSD_GROUPS) + r
                    seg = acs[:, h:h + 1] - acsT[h:h + 1, :]
                    dec = jnp.exp(jnp.where(causal, seg, neg_inf))
                    mat = (cb * dec * dtdT[h:h + 1, :]).astype(BF16)
                    parts.append(_dot(mat, xs_b[:, h * SSD_HEADDIM:(h + 1) * SSD_HEADDIM]))
                y_groups.append(jnp.concatenate(parts, axis=1) + y_off)
            st_ref[g] = st_old * cdec_x[:, g * rp:(g + 1) * rp] + s_new

        if mode == "fwd":
            for g in range(SSD_GROUPS):
                y_ref[rows, g * rp:(g + 1) * rp] = y_groups[g]
        elif mode == "bwd":
            for g in range(SSD_GROUPS):
                sl = slice(g * rp, (g + 1) * rp)
                y = yf_ref[rows, sl] + y_groups[g] + dx_ref[:, sl] * xs[:, sl]
                yz = y * _silu(z_ref[rows, sl])
                yz = yz * lax.rsqrt(jnp.mean(yz * yz, axis=-1, keepdims=True) + EPS)
                y_ref[rows, sl] = (yz * ng_ref[:, sl]).astype(y_ref.dtype)

    for sub in (range(n_sub - 1, -1, -1) if reverse else range(n_sub)):
        one_chunk(sub * q)

    if mode == "state":
        @pl.when(step == pl.num_programs(0) - 1)
        def _():
            hfin_ref[...] = st_ref[...]


def _ssd(xbc, dt, dtT, alog, alogT, h0, *, reverse, mode, yf=None, z=None, dx=None, ng=None):
    l = xbc.shape[0]
    q = min(l, SSD_CHUNK * SSD_STEP_CHUNKS)
    nc = l // q
    di = SSD_D_INNER
    nb = di // SSD_GN
    if reverse:
        cidx = lambda c: nc - 1 - c
    else:
        cidx = lambda c: c
    st_shape = (SSD_GROUPS, SSD_STATE, di // SSD_GROUPS)
    full3 = pl.BlockSpec(st_shape, lambda c: (0, 0, 0))
    in_specs = [pl.BlockSpec((q, di), lambda c: (cidx(c), 0)),
                pl.BlockSpec((q, SSD_GN), lambda c: (cidx(c), nb)),
                pl.BlockSpec((q, SSD_GN), lambda c: (cidx(c), nb + 1)),
                pl.BlockSpec((q, dt.shape[1]), lambda c: (cidx(c), 0)),
                pl.BlockSpec((dtT.shape[0], q), lambda c: (0, cidx(c))),
                pl.BlockSpec((2, SSD_HEADS), lambda c: (0, 0)),
                pl.BlockSpec((SSD_HEADS, 2), lambda c: (0, 0)),
                full3]
    args = [xbc, xbc, xbc, dt, dtT, alog, alogT, h0]
    wide = pl.BlockSpec((q, di), lambda c: (cidx(c), 0))
    rowspec = pl.BlockSpec((1, di), lambda c: (0, 0))
    if mode == "state":
        out_shape = jax.ShapeDtypeStruct(st_shape, F32)
        out_specs = full3
    elif mode == "fwd":
        out_shape = jax.ShapeDtypeStruct((l, di), F32)
        out_specs = wide
    else:
        in_specs += [wide, wide, rowspec, rowspec]
        args += [yf, z, dx, ng]
        out_shape = jax.ShapeDtypeStruct((l, di), BF16)
        out_specs = wide
    return pl.pallas_call(
        functools.partial(_ssd_kernel, reverse=reverse, mode=mode),
        out_shape=out_shape,
        grid=(nc,),
        in_specs=in_specs,
        out_specs=out_specs,
        scratch_shapes=[pltpu.VMEM(st_shape, F32)],
        compiler_params=_cparams(("arbitrary",)),
        name="ssd_%s_%s" % (mode, "rev" if reverse else "fwd"),
    )(*args)


def _dft_tables(l):
    n = 2 * l
    hh = l // DFT_B
    ka = jnp.arange(hh, dtype=jnp.int32)
    n1 = jnp.arange(hh, dtype=jnp.int32)
    n2 = jnp.arange(DFT_B, dtype=jnp.int32)
    odd = 2 * ka + 1
    ang_a = ((odd[:, None] * (DFT_B * n1)[None, :]) % (2 * n)).astype(F32) * (math.pi / n)
    ang_b = ((n2[:, None] * odd[None, :]) % (2 * n)).astype(F32) * (math.pi / n)
    ca, sa = jnp.cos(ang_a)[None, :, :], jnp.sin(ang_a)[None, :, :]
    cb, sb = jnp.cos(ang_b)[:, :, None], jnp.sin(ang_b)[:, :, None]
    cs = ca * cb - sa * sb
    sn = sa * cb + ca * sb
    g1 = jnp.concatenate([cs, -sn], axis=1).astype(BF16)
    scale = 2.0 / n
    h2 = jnp.concatenate([jnp.swapaxes(cs, 1, 2), -jnp.swapaxes(sn, 1, 2)], axis=2) * scale
    kb = jnp.arange(DFT_B, dtype=jnp.int32)
    ph2 = (kb[:, None] * n2[None, :]) % DFT_B
    ang2 = ph2.astype(F32) * (2.0 * math.pi / DFT_B)
    fr, fi = jnp.cos(ang2), -jnp.sin(ang2)
    f2 = jnp.concatenate([jnp.concatenate([fr, -fi], axis=1),
                          jnp.concatenate([fi, fr], axis=1)], axis=0).astype(BF16)
    f2i = jnp.concatenate([jnp.concatenate([fr, fi], axis=1),
                           jnp.concatenate([-fi, fr], axis=1)], axis=0).astype(BF16)
    return g1, f2, f2i, h2.astype(BF16)


def _filter_mlp_kernel(bands_ref, w1_ref, b1_ref, f1_ref, w2_ref, b2_ref, f2_ref, o_ref, *, l):
    tm = o_ref.shape[0]
    base = pl.program_id(0) * tm
    idx = (lax.broadcasted_iota(jnp.int32, (tm, LANES), 0) + base).astype(F32)
    lane = lax.broadcasted_iota(jnp.int32, (tm, LANES), 1)
    t = idx * (1.0 / (l - 1))
    w = idx * (2.0 * math.pi / l)
    arg = bands_ref[...] * w
    feats = jnp.where(lane == 0, t,
                      jnp.where(lane <= HY_BANDS, jnp.cos(arg),
                                jnp.where(lane < HY_EMB, -jnp.sin(arg), 0.0)))
    h = jnp.sin(f1_ref[...] * (_dot_hi(feats, w1_ref[...]) + b1_ref[...]))
    h = jnp.sin(f2_ref[...] * (_dot_hi(h, w2_ref[...]) + b2_ref[...]))
    o_ref[...] = h.astype(o_ref.dtype)


def _filter_mlp(l, w1, b1, f1, w2, b2, f2, tm=1024):
    tm = min(tm, l)
    bands = jnp.linspace(1e-4, HY_BANDS - 1, HY_BANDS, dtype=F32)
    bands_row = jnp.zeros((1, LANES), F32).at[0, 1:1 + HY_BANDS].set(bands)
    bands_row = bands_row.at[0, 1 + HY_BANDS:HY_EMB].set(bands)
    w1p = jnp.zeros((LANES, HY_HIDDEN), F32).at[:HY_EMB].set(w1)
    full = lambda a: pl.BlockSpec(a.shape, lambda i: (0,) * a.ndim)
    args = [bands_row, w1p, b1.reshape(1, -1), f1.reshape(1, -1), w2, b2.reshape(1, -1),
            f2.reshape(1, -1)]
    return pl.pallas_call(
        functools.partial(_filter_mlp_kernel, l=l),
        out_shape=jax.ShapeDtypeStruct((l, HY_HIDDEN), BF16),
        grid=(l // tm,),
        in_specs=[full(a) for a in args],
        out_specs=pl.BlockSpec((tm, HY_HIDDEN), lambda i: (i, 0)),
        compiler_params=_cparams(("parallel",)),
        name="hyena_filter_mlp",
    )(*args)


def _pq_pitch(hh):
    return 2 * hh + 8


def _fwd_stage1(src_ref, g1_ref, pq_ref, hh):
    pqp = _pq_pitch(hh)

    def body(n2, carry):
        rows = src_ref[pl.ds(n2, hh, stride=PITCH), :]
        p = _dot(g1_ref[n2], rows.astype(BF16))
        pq_ref[pl.ds(pl.multiple_of(n2 * pqp, 8), 2 * hh), :] = p
        return carry
    lax.fori_loop(0, DFT_B, body, 0, unroll=UNROLL_OUTER)


def _load_spectrum_rows(pq_ref, ka, hh):
    pqp = _pq_pitch(hh)
    return jnp.concatenate([pq_ref[pl.ds(ka, DFT_B, stride=pqp), :],
                            pq_ref[pl.ds(hh + ka, DFT_B, stride=pqp), :]], axis=0)


def _filter_spec_kernel(hid_ref, wf_ref, wb_ref, dl_ref, g1_ref, f2_ref, kr_ref, ki_ref,
                        s_ref, d_ref, pq_ref, pq2_ref, *, l):
    hh = l // DFT_B
    t = lax.broadcasted_iota(jnp.int32, (l, LANES), 0).astype(F32) * (1.0 / (l - 1))
    dec = jnp.exp(-t * dl_ref[...])
    hid = hid_ref[...]
    hf = _dot(hid, wf_ref[...].astype(BF16)) * dec
    hb = _dot(hid, wb_ref[...].astype(BF16)) * dec
    first = lax.broadcasted_iota(jnp.int32, (l, LANES), 0) == 0
    hb = jnp.where(first, 0.0, hb)
    inv = 1.0 / (jnp.sum(jnp.abs(hf), axis=0, keepdims=True)
                 + jnp.sum(jnp.abs(hb), axis=0, keepdims=True))
    hs = hf + hb
    hd = hf - hb
    for n1 in range(hh):
        s_ref[n1 * PITCH:n1 * PITCH + DFT_B, :] = hs[n1 * DFT_B:(n1 + 1) * DFT_B]
        d_ref[n1 * PITCH:n1 * PITCH + DFT_B, :] = hd[n1 * DFT_B:(n1 + 1) * DFT_B]
    f2_re = f2_ref[0:DFT_B, :]
    pqp = _pq_pitch(hh)

    def stage1(n2, carry):
        g = g1_ref[n2]
        ps = _dot(g, s_ref[pl.ds(n2, hh, stride=PITCH), :].astype(BF16))
        pd = _dot(g, d_ref[pl.ds(n2, hh, stride=PITCH), :].astype(BF16))
        off = pl.multiple_of(n2 * pqp, 8)
        pq_ref[pl.ds(off, 2 * hh), :] = ps
        pq2_ref[pl.ds(off, 2 * hh), :] = jnp.concatenate([pd[hh:], -pd[:hh]], axis=0)
        return carry
    lax.fori_loop(0, DFT_B, stage1, 0, unroll=UNROLL_OUTER)

    def body(ka, carry):
        pp = jnp.concatenate([_load_spectrum_rows(pq_ref, ka, hh).astype(BF16),
                              _load_spectrum_rows(pq2_ref, ka, hh).astype(BF16)], axis=1)
        x = _dot(f2_re, pp)
        rows = pl.ds(pl.multiple_of(ka * DFT_B, DFT_B), DFT_B)
        kr_ref[rows, :] = (x[:, :LANES] * inv).astype(kr_ref.dtype)
        ki_ref[rows, :] = (x[:, LANES:] * inv).astype(ki_ref.dtype)
        return carry
    lax.fori_loop(0, hh, body, 0, unroll=UNROLL_MID)


def _filter_spec(hid, w3, deltas, g1, f2, l, c):
    hh = l // DFT_B
    nct = c // LANES
    orders = w3.shape[1] // (2 * c)
    spec = pl.BlockSpec((None, l, LANES), lambda o, j: (o, 0, j))
    out_sd = jax.ShapeDtypeStruct((orders, l, c), BF16)
    return pl.pallas_call(
        functools.partial(_filter_spec_kernel, l=l),
        out_shape=(out_sd, out_sd),
        grid=(orders, nct),
        in_specs=[pl.BlockSpec((l, HY_HIDDEN), lambda o, j: (0, 0)),
                  pl.BlockSpec((HY_HIDDEN, LANES), lambda o, j: (0, (2 * o) * nct + j)),
                  pl.BlockSpec((HY_HIDDEN, LANES), lambda o, j: (0, (2 * o + 1) * nct + j)),
                  pl.BlockSpec((1, LANES), lambda o, j: (0, j)),
                  pl.BlockSpec(g1.shape, lambda o, j: (0, 0, 0)),
                  pl.BlockSpec(f2.shape, lambda o, j: (0, 0))],
        out_specs=(spec, spec),
        scratch_shapes=[pltpu.VMEM((hh * PITCH, LANES), F32)] * 2
        + [pltpu.VMEM((DFT_B * _pq_pitch(hh), LANES), F32)] * 2,
        compiler_params=_cparams(("parallel", "parallel")),
        name="hyena_filter_spec",
    )(hid, w3, w3, deltas, g1, f2)


def _long_conv_kernel(u_ref, m_ref, kr_ref, ki_ref, bias_ref, g1_ref, f2_ref, f2i_ref, h2_ref,
                      o_ref, pq_ref, *, l):
    hh = l // DFT_B
    pqp = _pq_pitch(hh)
    _fwd_stage1(u_ref, g1_ref, pq_ref, hh)
    f2 = f2_ref[...]
    f2i = f2i_ref[...]

    def mid(ka, carry):
        koff = pl.multiple_of(ka * DFT_B, DFT_B)
        x = _dot(f2, _load_spectrum_rows(pq_ref, ka, hh).astype(BF16))
        xr, xi = x[:DFT_B], x[DFT_B:]
        kr = kr_ref[pl.ds(koff, DFT_B), :].astype(F32)
        ki = ki_ref[pl.ds(koff, DFT_B), :].astype(F32)
        yy = jnp.concatenate([xr * kr - xi * ki, xr * ki + xi * kr], axis=0).astype(BF16)
        qq = _dot(f2i, yy)
        pq_ref[pl.ds(ka, DFT_B, stride=pqp), :] = qq[:DFT_B]
        pq_ref[pl.ds(hh + ka, DFT_B, stride=pqp), :] = qq[DFT_B:]
        return carry
    lax.fori_loop(0, hh, mid, 0, unroll=UNROLL_MID)

    bias = bias_ref[...]

    def last(n2, carry):
        qq = pq_ref[pl.ds(pl.multiple_of(n2 * pqp, 8), 2 * hh), :].astype(BF16)
        y = _dot(h2_ref[n2], qq)
        u = u_ref[pl.ds(n2, hh, stride=PITCH), :]
        m = m_ref[pl.ds(n2, hh, stride=PITCH), :]
        o_ref[pl.ds(n2, hh, stride=PITCH), :] = (m * (y + bias * u)).astype(o_ref.dtype)
        return carry
    lax.fori_loop(0, DFT_B, last, 0, unroll=UNROLL_OUTER)
    for n1 in range(hh):
        o_ref[n1 * PITCH + DFT_B:(n1 + 1) * PITCH, :] = jnp.zeros((PITCH - DFT_B, LANES), o_ref.dtype)


def _long_conv(u_arr, u_blk, m_arr, m_blk, kr, ki, order, bias, tables, l, c):
    g1, f2, f2i, h2 = tables
    hh = l // DFT_B
    nct = c // LANES
    kspec = pl.BlockSpec((None, l, LANES), lambda j: (order, 0, j))
    return pl.pallas_call(
        functools.partial(_long_conv_kernel, l=l),
        out_shape=jax.ShapeDtypeStruct((hh * PITCH, c), F32),
        grid=(nct,),
        in_specs=[pl.BlockSpec((hh * PITCH, LANES), lambda j: (0, u_blk * nct + j)),
                  pl.BlockSpec((hh * PITCH, LANES), lambda j: (0, m_blk * nct + j)),
                  kspec, kspec,
                  pl.BlockSpec((1, LANES), lambda j: (0, j)),
                  pl.BlockSpec(g1.shape, lambda j: (0, 0, 0)),
                  pl.BlockSpec(f2.shape, lambda j: (0, 0)),
                  pl.BlockSpec(f2i.shape, lambda j: (0, 0)),
                  pl.BlockSpec(h2.shape, lambda j: (0, 0, 0))],
        out_specs=pl.BlockSpec((hh * PITCH, LANES), lambda j: (0, j)),
        scratch_shapes=[pltpu.VMEM((DFT_B * _pq_pitch(hh), LANES), F32)],
        compiler_params=_cparams(("parallel",)),
        name="hyena_long_conv",
    )(u_arr, m_arr, kr, ki, bias.reshape(1, c), g1, f2, f2i, h2)


def _merge_kernel(hx_ref, ys_ref, yh_ref, wg1_ref, wg2_ref, w1_ref, w2_ref, gb1_ref, gb2_ref, o_ref):
    hx = hx_ref[...]
    g1 = _sigmoid(_dot_nt(hx, wg1_ref[...]) + gb1_ref[...])
    g2 = _sigmoid(_dot_nt(hx, wg2_ref[...]) + gb2_ref[...])
    yh = yh_ref[:, 0:DFT_B, :].reshape(hx.shape).astype(BF16)
    out = g1 * _dot(ys_ref[...], w1_ref[...]) + g2 * _dot(yh, w2_ref[...])
    o_ref[...] = out.astype(o_ref.dtype)


def _merge(hx, ys, yh, wgt, row0, w1, w2, gate_b, tm=1024, tn=256):
    m, d = hx.shape
    nt = d // tn
    a_spec = pl.BlockSpec((tm, d), lambda i, j: (i, 0))
    w_spec = pl.BlockSpec((d, tn), lambda i, j: (0, j))
    gb = gate_b.reshape(1, 2 * d)
    return pl.pallas_call(
        _merge_kernel,
        out_shape=jax.ShapeDtypeStruct((m, d), BF16),
        grid=(m // tm, nt),
        in_specs=[a_spec, a_spec,
                  pl.BlockSpec((tm // DFT_B, PITCH, d), lambda i, j: (i, 0, 0)),
                  pl.BlockSpec((pl.Element(tn), pl.Element(d)),
                               lambda i, j: (pl.multiple_of(row0 + j * tn, ROW_ALIGN), 0)),
                  pl.BlockSpec((pl.Element(tn), pl.Element(d)),
                               lambda i, j: (pl.multiple_of(row0 + (nt + j) * tn, ROW_ALIGN), 0)),
                  w_spec, w_spec,
                  pl.BlockSpec((1, tn), lambda i, j: (0, j)),
                  pl.BlockSpec((1, tn), lambda i, j: (0, nt + j))],
        out_specs=pl.BlockSpec((tm, tn), lambda i, j: (i, j)),
        compiler_params=_cparams(("parallel", "parallel")),
        name="merge",
    )(hx, ys, yh, wgt, wgt, w1, w2, gb, gb)


def _oproj_kernel(a_ref, w_ref, x_ref, gt_ref, o_ref):
    o_ref[...] = x_ref[...] + gt_ref[...] * _dot(a_ref[...], w_ref[...])


def _oproj(a, w, x, gt, tm=2048, tn=512):
    m, d = a.shape
    n = w.shape[1]
    tm = min(tm, m)
    return pl.pallas_call(
        _oproj_kernel,
        out_shape=jax.ShapeDtypeStruct((m, n), F32),
        grid=(m // tm, n // tn),
        in_specs=[pl.BlockSpec((tm, d), lambda i, j: (i, 0)),
                  pl.BlockSpec((d, tn), lambda i, j: (0, j)),
                  pl.BlockSpec((tm, tn), lambda i, j: (i, j)),
                  pl.BlockSpec((1, tn), lambda i, j: (0, j))],
        out_specs=pl.BlockSpec((tm, tn), lambda i, j: (i, j)),
        compiler_params=_cparams(("parallel", "parallel")),
        name="oproj",
    )(a, w, x, gt)


def _pack_bf16_pairs(v):
    half = v.shape[1] // 2
    bits = lax.bitcast_convert_type(v.astype(BF16).astype(F32), jnp.uint32)
    return (bits[:, :half] >> 16) | (bits[:, half:] & jnp.uint32(0xFFFF0000))


def _unpack_bf16_pairs(p):
    lo = lax.bitcast_convert_type(p << 16, F32).astype(BF16)
    hi = lax.bitcast_convert_type(p & jnp.uint32(0xFFFF0000), F32).astype(BF16)
    return lo, hi


def _router_kernel(x_ref, g_ref, sh_ref, sc_ref, wr_ref, br_ref, h_ref, ids_ref, wts_ref):
    x = x_ref[...]
    y = x * lax.rsqrt(jnp.mean(x * x, axis=-1, keepdims=True) + EPS) * g_ref[...]
    h = y * (1.0 + sc_ref[...]) + sh_ref[...]
    h_ref[...] = _pack_bf16_pairs(h)
    logits = _dot_mid(h, wr_ref[...]) + br_ref[...]
    tm = x.shape[0]
    lane = lax.broadcasted_iota(jnp.int32, (tm, LANES), 1)
    neg = jnp.float32(-jnp.inf)
    big = jnp.int32(LANES)
    is_grp = (lane >= MOE_EXPERTS) & (lane < MOE_EXPERTS + MOE_GROUPS)
    gl = jnp.where(is_grp, logits, neg)
    gmax = jnp.max(gl, axis=-1, keepdims=True)
    gidx = jnp.min(jnp.where(gl == gmax, lane, big), axis=-1, keepdims=True) - MOE_EXPERTS
    gw = 1.0 / jnp.sum(jnp.where(is_grp, jnp.exp(logits - gmax), 0.0), axis=-1, keepdims=True)
    in_grp = (lane < MOE_EXPERTS) & ((lane // MOE_PER_GROUP) == gidx)
    el = jnp.where(in_grp, logits, neg)
    m1 = jnp.max(el, axis=-1, keepdims=True)
    i1 = jnp.min(jnp.where(el == m1, lane, big), axis=-1, keepdims=True)
    el2 = jnp.where(lane == i1, neg, el)
    m2 = jnp.max(el2, axis=-1, keepdims=True)
    i2 = jnp.min(jnp.where(el2 == m2, lane, big), axis=-1, keepdims=True)
    e21 = jnp.exp(m2 - m1)
    w1 = gw / (1.0 + e21)
    w2 = gw * e21 / (1.0 + e21)
    ids_ref[...] = jnp.where(lane == 0, i1, jnp.where(lane == 1, i2, -1))
    wts_ref[...] = jnp.where(lane == 0, w1, jnp.where(lane == 1, w2, 0.0))


def _router(x1, g, sh, sc, w_rg, b_rg, w_re, b_re, tm=512):
    m, d = x1.shape
    wr = jnp.zeros((d, LANES), F32).at[:, :MOE_EXPERTS].set(w_re)
    wr = wr.at[:, MOE_EXPERTS:MOE_EXPERTS + MOE_GROUPS].set(w_rg)
    br = jnp.zeros((1, LANES), F32).at[0, :MOE_EXPERTS].set(b_re)
    br = br.at[0, MOE_EXPERTS:MOE_EXPERTS + MOE_GROUPS].set(b_rg)
    row = pl.BlockSpec((1, d), lambda i: (0, 0))
    return pl.pallas_call(
        _router_kernel,
        out_shape=(jax.ShapeDtypeStruct((m, d // 2), jnp.uint32),
                   jax.ShapeDtypeStruct((m, LANES), jnp.int32),
                   jax.ShapeDtypeStruct((m, LANES), F32)),
        grid=(m // tm,),
        in_specs=[pl.BlockSpec((tm, d), lambda i: (i, 0)), row, row, row,
                  pl.BlockSpec((d, LANES), lambda i: (0, 0)),
                  pl.BlockSpec((1, LANES), lambda i: (0, 0))],
        out_specs=(pl.BlockSpec((tm, d // 2), lambda i: (i, 0)),
                   pl.BlockSpec((tm, LANES), lambda i: (i, 0)),
                   pl.BlockSpec((tm, LANES), lambda i: (i, 0))),
        compiler_params=_cparams(("parallel",)),
        name="router",
    )(x1, g.reshape(1, d), sh, sc, wr, br)


def _moe_positions_kernel(ids_ref, pos_ref, meta_ref):
    n = ids_ref.shape[0]
    t = MOE_TILE
    lane = lax.broadcasted_iota(jnp.int32, (t, LANES), 1)

    def onehot(k):
        idt = ids_ref[pl.ds(pl.multiple_of(k * t, t), t), :]
        i1, i2 = idt[:, 0:1], idt[:, 1:2]
        return i1, i2, jnp.where((lane == i1) | (lane == i2), 1.0, 0.0)

    def count(k, acc):
        return acc + jnp.sum(onehot(k)[2], axis=0, keepdims=True)
    total = lax.fori_loop(0, n // t, count, jnp.zeros((1, LANES), F32))
    padded = (((total.astype(jnp.int32) + (t - 1)) // t) * t).astype(F32)
    r128 = lax.broadcasted_iota(jnp.int32, (LANES, LANES), 0)
    c128 = lax.broadcasted_iota(jnp.int32, (LANES, LANES), 1)
    before = jnp.where(r128 < c128, 1.0, 0.0).astype(BF16)
    off = _dot_sel_r(jnp.broadcast_to(padded, (8, LANES)), before)[0:1]
    row = lax.broadcasted_iota(jnp.int32, (t, t), 0)
    col = lax.broadcasted_iota(jnp.int32, (t, t), 1)
    tri = jnp.where(col <= row, 1.0, 0.0).astype(BF16)

    def place(k, seen):
        i1, i2, oh = onehot(k)
        base = off + seen + _dot(tri, oh.astype(BF16)) - oh
        p1 = jnp.sum(jnp.where(lane == i1, base, 0.0), axis=1, keepdims=True)
        p2 = jnp.sum(jnp.where(lane == i2, base, 0.0), axis=1, keepdims=True)
        pos_ref[pl.ds(pl.multiple_of(k * t, t), t), :] = jnp.where(
            lane == 0, p1, jnp.where(lane == 1, p2, 0.0)).astype(jnp.int32)
        return seen + jnp.sum(oh, axis=0, keepdims=True)
    lax.fori_loop(0, n // t, place, jnp.zeros((1, LANES), F32))

    ends = off + padded
    start = (r128 * t).astype(F32)
    done = jnp.where((jnp.broadcast_to(ends, (LANES, LANES)) <= start) & (c128 < MOE_EXPERTS), 1.0, 0.0)
    tile_expert = jnp.minimum(jnp.sum(done, axis=1, keepdims=True), MOE_EXPERTS - 1.0)
    used = jnp.sum(jnp.where(c128[0:1] == MOE_EXPERTS - 1, ends, 0.0), axis=1, keepdims=True) / t
    lanef = c128.astype(F32)
    later = (lanef > tile_expert) & (jnp.broadcast_to(padded, (LANES, LANES)) > 0.0) & (c128 < MOE_EXPERTS)
    nxt = jnp.min(jnp.where(later, lanef, float(LANES)), axis=1, keepdims=True)
    nxt = jnp.where(nxt >= float(LANES), -1.0, nxt)
    meta_ref[...] = jnp.where(c128 == 0, tile_expert,
                              jnp.where(c128 == 1, used, jnp.where(c128 == 2, nxt, 0.0))).astype(jnp.int32)


def _moe_positions(ids):
    m = ids.shape[0]
    return pl.pallas_call(
        _moe_positions_kernel,
        out_shape=(jax.ShapeDtypeStruct((m, LANES), jnp.int32),
                   jax.ShapeDtypeStruct((LANES, LANES), jnp.int32)),
        grid=(1,),
        in_specs=[pl.BlockSpec((m, LANES), lambda i: (0, 0))],
        out_specs=(pl.BlockSpec((m, LANES), lambda i: (0, 0)),
                   pl.BlockSpec((LANES, LANES), lambda i: (0, 0))),
        compiler_params=_cparams(("arbitrary",)),
        name="moe_positions",
    )(ids)


def _moe_dispatch_kernel(p1_ref, p2_ref, h_ref, xs_in_ref, xs_ref, sem):
    del xs_in_ref
    t = h_ref.shape[0]
    base = pl.program_id(0) * t

    def issue(r, carry):
        src = h_ref.at[pl.ds(r, 1), :]
        pltpu.make_async_copy(src, xs_ref.at[pl.ds(p1_ref[base + r], 1), :], sem.at[0]).start()
        pltpu.make_async_copy(src, xs_ref.at[pl.ds(p2_ref[base + r], 1), :], sem.at[1]).start()
        return carry
    lax.fori_loop(0, t, issue, 0, unroll=8)
    pltpu.make_async_copy(h_ref, xs_ref.at[pl.ds(0, t), :], sem.at[0]).wait()
    pltpu.make_async_copy(h_ref, xs_ref.at[pl.ds(0, t), :], sem.at[1]).wait()


def _moe_dispatch(hp, pos1, pos2, n_sorted):
    m, w = hp.shape
    t = MOE_TILE
    zeros = jnp.zeros((n_sorted, w), hp.dtype)
    return pl.pallas_call(
        _moe_dispatch_kernel,
        out_shape=jax.ShapeDtypeStruct((n_sorted, w), hp.dtype),
        grid_spec=pltpu.PrefetchScalarGridSpec(
            num_scalar_prefetch=2,
            grid=(m // t,),
            in_specs=[pl.BlockSpec((t, w), lambda i, p1, p2: (i, 0)),
                      pl.BlockSpec(memory_space=pl.ANY)],
            out_specs=pl.BlockSpec(memory_space=pl.ANY),
            scratch_shapes=[pltpu.SemaphoreType.DMA((2,))]),
        input_output_aliases={3: 0},
        compiler_params=_cparams(("arbitrary",)),
        name="moe_dispatch",
    )(pos1, pos2, hp, zeros)


def _moe_experts_kernel(te_ref, nu_ref, nx_ref, x_ref, wg_hbm, wu_hbm, wd_hbm, y_ref,
                        wg_f, wu_f, wd_f, wg_b, wu_b, wd_b, slot_ref, sem):
    i = pl.program_id(0)
    live = i < nu_ref[0]
    new_expert = (i == 0) | (te_ref[i] != te_ref[jnp.maximum(i - 1, 0)])

    def copies(e, s):
        return (pltpu.make_async_copy(wg_hbm.at[e], wg_f.at[s], sem.at[s, 0]),
                pltpu.make_async_copy(wu_hbm.at[e], wu_f.at[s], sem.at[s, 1]),
                pltpu.make_async_copy(wd_hbm.at[e], wd_f.at[s], sem.at[s, 2]))

    @pl.when(i == 0)
    def _():
        slot_ref[0] = 0
        for cp in copies(te_ref[0], 0):
            cp.start()

    @pl.when(live & new_expert)
    def _():
        s = slot_ref[0]
        for cp in copies(te_ref[i], s):
            cp.wait()
        wg_b[...] = wg_f[s].astype(BF16)
        wu_b[...] = wu_f[s].astype(BF16)
        wd_b[...] = wd_f[s].astype(BF16)
        slot_ref[0] = 1 - s

        @pl.when(nx_ref[i] >= 0)
        def _():
            for cp in copies(nx_ref[i], 1 - s):
                cp.start()

    @pl.when(live)
    def _():
        x_lo, x_hi = _unpack_bf16_pairs(x_ref[...])
        half = x_lo.shape[1]
        hg = _dot(x_lo, wg_b[0:half, :]) + _dot(x_hi, wg_b[half:, :])
        hu = _dot(x_lo, wu_b[0:half, :]) + _dot(x_hi, wu_b[half:, :])
        y_ref[...] = _dot((_silu(hg) * hu).astype(BF16), wd_b[...])

    @pl.when(jnp.logical_not(live))
    def _():
        y_ref[...] = jnp.zeros_like(y_ref)


def _moe_experts(xs, tile_expert, n_used, next_expert, wg, wu, wd):
    ns = xs.shape[0]
    ne, d, f = wg.shape
    t = MOE_TILE
    hbm = pl.BlockSpec(memory_space=pl.ANY)
    return pl.pallas_call(
        _moe_experts_kernel,
        out_shape=jax.ShapeDtypeStruct((ns, d), F32),
        grid_spec=pltpu.PrefetchScalarGridSpec(
            num_scalar_prefetch=3,
            grid=(ns // t,),
            in_specs=[pl.BlockSpec((t, d // 2), lambda i, te, nu, nx: (i, 0)), hbm, hbm, hbm],
            out_specs=pl.BlockSpec((t, d), lambda i, te, nu, nx: (i, 0)),
            scratch_shapes=[pltpu.VMEM((2, d, f), F32), pltpu.VMEM((2, d, f), F32),
                            pltpu.VMEM((2, f, d), F32),
                            pltpu.VMEM((d, f), BF16), pltpu.VMEM((d, f), BF16),
                            pltpu.VMEM((f, d), BF16),
                            pltpu.SMEM((1,), jnp.int32), pltpu.SemaphoreType.DMA((2, 3))]),
        compiler_params=_cparams(("arbitrary",)),
        name="moe_experts",
    )(tile_expert, n_used, next_expert, xs, wg, wu, wd)


def _moe_combine_kernel(p1_ref, p2_ref, x_ref, wts_ref, gt_ref, fg_ref, y_ref, o_ref, ya, yb, sem):
    t = x_ref.shape[0]
    i = pl.program_id(0)
    n = pl.num_programs(0)

    def row_copy(src_row, buf, slot, r, which):
        return pltpu.make_async_copy(y_ref.at[pl.ds(src_row, 1), :], buf.at[slot, pl.ds(r, 1), :],
                                     sem.at[which, slot])

    def issue(tile, slot):
        def body(r, carry):
            row_copy(p1_ref[tile * t + r], ya, slot, r, 0).start()
            row_copy(p2_ref[tile * t + r], yb, slot, r, 1).start()
            return carry
        lax.fori_loop(0, t, body, 0, unroll=8)

    @pl.when(i == 0)
    def _():
        issue(0, 0)

    @pl.when(i + 1 < n)
    def _():
        issue(i + 1, (i + 1) % 2)

    slot = i % 2
    pltpu.make_async_copy(y_ref.at[pl.ds(0, t), :], ya.at[slot], sem.at[0, slot]).wait()
    pltpu.make_async_copy(y_ref.at[pl.ds(0, t), :], yb.at[slot], sem.at[1, slot]).wait()
    w = wts_ref[...]
    moe = w[:, 0:1] * ya[slot] + w[:, 1:2] * yb[slot]
    x2 = x_ref[...] + gt_ref[...] * moe
    o_ref[...] = x2 * lax.rsqrt(jnp.mean(x2 * x2, axis=-1, keepdims=True) + EPS) * fg_ref[...]


def _moe_combine(x1, wts, gt, fg, y_sorted, pos1, pos2):
    m, d = x1.shape
    t = MOE_TILE
    row = pl.BlockSpec((1, d), lambda i, p1, p2: (0, 0))
    return pl.pallas_call(
        _moe_combine_kernel,
        out_shape=jax.ShapeDtypeStruct((m, d), F32),
        grid_spec=pltpu.PrefetchScalarGridSpec(
            num_scalar_prefetch=2,
            grid=(m // t,),
            in_specs=[pl.BlockSpec((t, d), lambda i, p1, p2: (i, 0)),
                      pl.BlockSpec((t, LANES), lambda i, p1, p2: (i, 0)),
                      row, row,
                      pl.BlockSpec(memory_space=pl.ANY)],
            out_specs=pl.BlockSpec((t, d), lambda i, p1, p2: (i, 0)),
            scratch_shapes=[pltpu.VMEM((2, t, d), F32), pltpu.VMEM((2, t, d), F32),
                            pltpu.SemaphoreType.DMA((2, 2))]),
        compiler_params=_cparams(("arbitrary",)),
        name="moe_combine",
    )(pos1, pos2, x1, wts, gt, fg.reshape(1, d), y_sorted)


def kernel(x, c, ctx, c_ctx, ada_w, ada_b, norm1_g, w_in, ssd_conv_w, ssd_conv_b, ssd_dt_bias,
           ssd_a_log, ssd_d, ssd_norm_g, w_ssd_out, hy_conv_w, hy_conv_b, hy_f_w1, hy_f_b1,
           hy_f_freq1, hy_f_w2, hy_f_b2, hy_f_freq2, hy_f_w3, hy_bias, w_hy_out, gate_b, w_o,
           norm2_g, moe_w_rg, moe_b_rg, moe_w_re, moe_b_re, moe_w_gate, moe_w_up, moe_w_down,
           final_g):
    bsz, l, d = x.shape
    assert bsz == 1 and ada_w.shape[0] == 1
    lc = ctx.shape[1]
    di = SSD_D_INNER
    xbc_cols = di + 2 * SSD_GN
    col_xbc = di
    col_dt = col_xbc + xbc_cols
    col_hy = col_dt + 2 * SSD_HEADS
    hyw = hy_bias.shape[-1]

    x2d = x[0]
    ctx2d = ctx[0]

    cc = jnp.zeros((8, d), F32).at[0].set(c[0]).at[1].set(c_ctx)
    mods = _adaln(cc, ada_w[0], ada_b[0])
    sh1, sc1, gt1, sh2, sc2, gt2 = [mods[0:1, i * d:(i + 1) * d] for i in range(6)]
    csh1, csc1 = mods[1:2, 0:d], mods[1:2, d:2 * d]

    hx = _normmod(x2d, norm1_g[0], sh1, sc1, tm=512)
    hc = _normmod(ctx2d, norm1_g[0], csh1, csc1, tm=lc)

    w_in_t = w_in[0].T
    col_gate = col_hy + 3 * hyw
    wt_gate = w_in_t[col_gate:].astype(BF16)
    n_dt = 2 * SSD_HEADS
    dt_bias = ssd_dt_bias[0].reshape(-1)

    xbc_c = _proj(hc, w_in_t, col_xbc, xbc_cols, "conv_silu", tm=lc, tn=512, conv_w=ssd_conv_w[0],
                  bias=ssd_conv_b[0], group=lc)
    dt_c = _proj(hc, w_in_t, col_dt, n_dt, "softplus", tm=lc, tn=n_dt, bias=dt_bias)
    alog = ssd_a_log[0]
    alogT = alog.T
    zero_state = jnp.zeros((SSD_GROUPS, SSD_STATE, di // SSD_GROUPS), F32)
    h0f = _ssd(xbc_c, dt_c, dt_c.T, alog, alogT, zero_state, reverse=False, mode="state")
    h0b = _ssd(xbc_c, dt_c, dt_c.T, alog, alogT, zero_state, reverse=True, mode="state")

    z = _proj(hx, w_in_t, 0, di, "none", tm=PROJ_TM, tn=512)
    xbc = _proj(hx, w_in_t, col_xbc, xbc_cols, "conv_silu", tm=PROJ_TM, tn=512, conv_w=ssd_conv_w[0],
                bias=ssd_conv_b[0])
    dt = _proj(hx, w_in_t, col_dt, n_dt, "softplus", tm=PROJ_TM, tn=n_dt, bias=dt_bias)
    u3 = _proj(hx, w_in_t, col_hy, 3 * hyw, "conv", tm=PROJ_TM, tn=512, conv_w=hy_conv_w[0],
               bias=hy_conv_b[0], pitched=True).reshape(l // DFT_B * PITCH, 3 * hyw)

    dtT = dt.T
    yf = _ssd(xbc, dt, dtT, alog, alogT, h0f, reverse=False, mode="fwd")
    dx = jnp.repeat(ssd_d[0], SSD_HEADDIM).reshape(1, di)
    y_ssd = _ssd(xbc, dt, dtT, alog, alogT, h0b, reverse=True, mode="bwd", yf=yf, z=z, dx=dx,
                 ng=ssd_norm_g[0].reshape(1, di))

    tables = _dft_tables(l)
    hid = _filter_mlp(l, hy_f_w1[0], hy_f_b1[0], hy_f_freq1[0], hy_f_w2[0], hy_f_b2[0],
                      hy_f_freq2[0])
    max_decay = math.log(1e-2) / 0.3
    min_decay = math.log(1e-2) / 1.5
    deltas = jnp.abs(jnp.linspace(min_decay, max_decay, hyw, dtype=F32)).reshape(1, hyw)
    kr, ki = _filter_spec(hid, hy_f_w3[0], deltas, tables[0], tables[1], l, hyw)
    z2 = _long_conv(u3, 0, u3, 1, kr, ki, 0, hy_bias[0, 0], tables, l, hyw)
    y_hy = _long_conv(z2, 0, u3, 2, kr, ki, 1, hy_bias[0, 1], tables, l, hyw)

    merged = _merge(hx, y_ssd, y_hy.reshape(l // DFT_B, PITCH, hyw), wt_gate, 0,
                    w_ssd_out[0].astype(BF16), w_hy_out[0].astype(BF16), gate_b[0])
    x1 = _oproj(merged, w_o[0].astype(BF16), x2d, gt1)

    h2, ids, wts = _router(x1, norm2_g[0], sh2, sc2, moe_w_rg[0], moe_b_rg[0], moe_w_re[0],
                           moe_b_re[0])
    pos, meta = _moe_positions(ids)
    pos1, pos2 = pos[:, 0], pos[:, 1]
    n_sorted = 2 * l + MOE_EXPERTS * MOE_TILE
    n_tiles = n_sorted // MOE_TILE
    xs = _moe_dispatch(h2, pos1, pos2, n_sorted)
    ys = _moe_experts(xs, meta[:n_tiles, 0], meta[0, 1:2], meta[:n_tiles, 2], moe_w_gate[0],
                      moe_w_up[0], moe_w_down[0])
    out = _moe_combine(x1, wts, gt2, final_g, ys, pos1, pos2)
    return out[None]
```

```python
import functools
import math

import jax
import jax.numpy as jnp
from jax import lax
from jax.experimental import pallas as pl
from jax.experimental.pallas import tpu as pltpu

F32 = jnp.float32
BF16 = jnp.bfloat16

EPS = 1e-6
GRID_W = 64
SSD_HEADS = 32
SSD_HEADDIM = 64
SSD_GROUPS = 4
SSD_STATE = 128
SSD_CHUNK = 128
SSD_D_INNER = SSD_HEADS * SSD_HEADDIM
SSD_GN = SSD_GROUPS * SSD_STATE
HY_BANDS = 16
HY_EMB = 33
HY_HIDDEN = 64
HY_DECAY_TARGET = 1e-2
HY_FAST_DECAY_PCT = 0.3
HY_SLOW_DECAY_PCT = 1.5
MOE_GROUPS = 4
MOE_PER_GROUP = 4
MOE_EXPERTS = 16

LANES = 128
DFT_B = 128
PITCH = 136
VMEM_LIMIT = 56 * 1024 * 1024
UNROLL_OUTER = 32
UNROLL_MID = 16
ROW_ALIGN = 16
PROJ_TM = 2048
PROJ_SUB_ROWS = 512
SSD_STEP_CHUNKS = 4
MOE_TILE = 256


def _cparams(sem):
    return pltpu.CompilerParams(dimension_semantics=sem, vmem_limit_bytes=VMEM_LIMIT)


def _dot(a, b):
    return jnp.dot(a, b, preferred_element_type=F32)


def _dot_nt(a, bt):
    return lax.dot_general(a, bt, (((1,), (1,)), ((), ())), preferred_element_type=F32)


def _sigmoid(x):
    return 1.0 / (1.0 + jnp.exp(-x))


def _silu(x):
    return x * _sigmoid(x)


def _split3(v):
    b1 = v.astype(BF16)
    r1 = v - b1.astype(F32)
    b2 = r1.astype(BF16)
    b3 = (r1 - b2.astype(F32)).astype(BF16)
    return b1, b2, b3


def _dot_sel_r(v, sel_b):
    b1, b2, b3 = _split3(v)
    return _dot(b1, sel_b) + _dot(b2, sel_b) + _dot(b3, sel_b)


def _dot_sel_l(sel_b, v):
    b1, b2, b3 = _split3(v)
    return _dot(sel_b, b1) + _dot(sel_b, b2) + _dot(sel_b, b3)


def _dot_mid(a, b):
    a1 = a.astype(BF16)
    a2 = (a - a1.astype(F32)).astype(BF16)
    b1 = b.astype(BF16)
    b2 = (b - b1.astype(F32)).astype(BF16)
    return _dot(a1, b1) + (_dot(a1, b2) + _dot(a2, b1))


def _dot_hi(a, b):
    a1, a2, a3 = _split3(a)
    b1, b2, b3 = _split3(b)
    return (_dot(a1, b1) + (_dot(a1, b2) + _dot(a2, b1))
            + (_dot(a1, b3) + _dot(a2, b2) + _dot(a3, b1)))


def _adaln_kernel(c_ref, w_ref, b_ref, o_ref):
    s = _silu(c_ref[...]).astype(BF16)
    o_ref[...] = _dot(s, w_ref[...].astype(BF16)) + b_ref[...]


def _adaln(cc, w, b, tn=1024):
    m, d = cc.shape
    n = w.shape[1]
    return pl.pallas_call(
        _adaln_kernel,
        out_shape=jax.ShapeDtypeStruct((m, n), F32),
        grid=(n // tn,),
        in_specs=[pl.BlockSpec((m, d), lambda j: (0, 0)),
                  pl.BlockSpec((d, tn), lambda j: (0, j)),
                  pl.BlockSpec((1, tn), lambda j: (0, j))],
        out_specs=pl.BlockSpec((m, tn), lambda j: (0, j)),
        compiler_params=_cparams(("parallel",)),
        name="adaln",
    )(cc, w, b.reshape(1, n))


def _normmod_kernel(x_ref, g_ref, sh_ref, sc_ref, o_ref):
    x = x_ref[...]
    y = x * lax.rsqrt(jnp.mean(x * x, axis=-1, keepdims=True) + EPS) * g_ref[...]
    o_ref[...] = (y * (1.0 + sc_ref[...]) + sh_ref[...]).astype(o_ref.dtype)


def _normmod(x, g, sh, sc, tm):
    m, d = x.shape
    row = pl.BlockSpec((1, d), lambda i: (0, 0))
    return pl.pallas_call(
        _normmod_kernel,
        out_shape=jax.ShapeDtypeStruct((m, d), BF16),
        grid=(m // tm,),
        in_specs=[pl.BlockSpec((tm, d), lambda i: (i, 0)), row, row, row],
        out_specs=pl.BlockSpec((tm, d), lambda i: (i, 0)),
        compiler_params=_cparams(("parallel",)),
        name="normmod",
    )(x, g.reshape(1, d), sh, sc)


def _conv_taps(cw_ref, rows, group):
    n_taps, tn = cw_ref.shape
    half = n_taps // 2
    pos = lax.broadcasted_iota(jnp.int32, (rows, tn), 0) & (group - 1)
    taps = []
    for k in range(n_taps):
        d = k - half
        wk = cw_ref[k:k + 1, :]
        if d != 0:
            wk = jnp.where((pos + d >= 0) & (pos + d < group), wk, 0.0)
        taps.append((d, wk))
    return taps


def _conv_rows(acc, taps):
    rows = acc.shape[0]
    out = None
    for d, wk in taps:
        term = (acc if d == 0 else pltpu.roll(acc, (rows - d) % rows, 0)) * wk
        out = term if out is None else out + term
    return out


def _proj_kernel(a_ref, w_ref, *rest, epi, group):
    o_ref = rest[-1]
    tm = a_ref.shape[0]
    sub = max(group, min(tm, PROJ_SUB_ROWS))
    w = w_ref[...].astype(BF16)
    taps = _conv_taps(rest[0], sub, group) if epi in ("conv", "conv_silu") else None
    for s in range(tm // sub):
        acc = _dot_nt(a_ref[s * sub:(s + 1) * sub, :], w)
        if epi == "none":
            out = acc
        elif epi == "conv":
            out = _conv_rows(acc, taps) + rest[1][...]
        elif epi == "conv_silu":
            out = _silu(_conv_rows(acc, taps) + rest[1][...])
        elif epi == "softplus":
            v = acc + rest[0][...]
            out = jnp.maximum(v, 0.0) + jnp.log(1.0 + jnp.exp(-jnp.abs(v)))
        else:
            raise ValueError(epi)
        if len(o_ref.shape) == 3:
            nb, tn = sub // DFT_B, o_ref.shape[2]
            o_ref[s * nb:(s + 1) * nb, 0:DFT_B, :] = out.reshape(nb, DFT_B, tn).astype(o_ref.dtype)
            o_ref[s * nb:(s + 1) * nb, DFT_B:PITCH, :] = jnp.zeros((nb, PITCH - DFT_B, tn), o_ref.dtype)
        else:
            o_ref[s * sub:(s + 1) * sub, :] = out.astype(o_ref.dtype)


def _proj(a, wt, row0, n, epi, tm, tn, conv_w=None, bias=None, group=GRID_W, out_dtype=F32,
          pitched=False):
    m, k = a.shape
    tm = min(tm, m)
    tn = min(tn, n)
    if pitched:
        out_shape = jax.ShapeDtypeStruct((m // DFT_B, PITCH, n), out_dtype)
        out_spec = pl.BlockSpec((tm // DFT_B, PITCH, tn), lambda i, j: (i, 0, j))
    else:
        out_shape = jax.ShapeDtypeStruct((m, n), out_dtype)
        out_spec = pl.BlockSpec((tm, tn), lambda i, j: (i, j))
    extra, extra_specs = [], []
    if conv_w is not None:
        extra.append(conv_w)
        extra_specs.append(pl.BlockSpec((conv_w.shape[0], tn), lambda i, j: (0, j)))
    if bias is not None:
        extra.append(bias.reshape(1, n))
        extra_specs.append(pl.BlockSpec((1, tn), lambda i, j: (0, j)))
    return pl.pallas_call(
        functools.partial(_proj_kernel, epi=epi, group=group),
        out_shape=out_shape,
        grid=(m // tm, n // tn),
        in_specs=[pl.BlockSpec((tm, k), lambda i, j: (i, 0)),
                  pl.BlockSpec((pl.Element(tn), pl.Element(k)),
                               lambda i, j: (pl.multiple_of(row0 + j * tn, ROW_ALIGN), 0))]
        + extra_specs,
        out_specs=out_spec,
        compiler_params=_cparams(("parallel", "parallel")),
        name="proj_" + epi,
    )(a, wt, *extra)


def _ssd_kernel(xs_ref, b_ref, c_ref, dt_ref, dtT_ref, alog_ref, alogT_ref, h0_ref, *rest,
                reverse, mode):
    if mode == "state":
        hfin_ref, st_ref = rest
    elif mode == "fwd":
        y_ref, st_ref = rest
    else:
        yf_ref, z_ref, dx_ref, ng_ref, y_ref, st_ref = rest
    q = SSD_CHUNK
    n_sub = xs_ref.shape[0] // q
    nh = SSD_HEADS
    rp = SSD_D_INNER // SSD_GROUPS
    step = pl.program_id(0)

    @pl.when(step == 0)
    def _():
        st_ref[...] = h0_ref[...]

    d = 1 if reverse else 0
    a_row = -jnp.exp(alog_ref[d:d + 1, :])
    a_col = -jnp.exp(alogT_ref[:, d:d + 1])
    row = lax.broadcasted_iota(jnp.int32, (q, q), 0)
    col = lax.broadcasted_iota(jnp.int32, (q, q), 1)
    lower = col <= row
    upper = col >= row
    causal = upper if reverse else lower
    tri = jnp.where(causal, 1.0, 0.0).astype(BF16)
    triT = jnp.where(lower if reverse else upper, 1.0, 0.0).astype(BF16)
    hsel = lax.broadcasted_iota(jnp.int32, (nh, SSD_D_INNER), 0)
    lsel = lax.broadcasted_iota(jnp.int32, (nh, SSD_D_INNER), 1) // SSD_HEADDIM
    expand = jnp.where(hsel == lsel, 1.0, 0.0).astype(BF16)
    neg_inf = jnp.float32(-jnp.inf)

    def one_chunk(r0):
        rows = slice(r0, r0 + q)
        dtd = dt_ref[rows, d * nh:(d + 1) * nh]
        dtdT = dtT_ref[d * nh:(d + 1) * nh, rows]
        ad = dtd * a_row
        adT = dtdT * a_col
        acs = _dot_sel_l(tri, ad)
        acsT = _dot_sel_r(adT, triT)
        tot = jnp.sum(ad, axis=0, keepdims=True)
        dte = jnp.exp(tot - acs)
        eacs = jnp.exp(acs)
        cdec = jnp.exp(tot)
        stacked = jnp.concatenate([dtd * dte, eacs, jnp.broadcast_to(cdec, (8, nh))], axis=0)
        s1 = stacked.astype(BF16)
        s2 = (stacked - s1.astype(F32)).astype(BF16)
        exp_all = _dot(s1, expand) + _dot(s2, expand)
        w_x = exp_all[0:q]
        eacs_x = exp_all[q:2 * q]
        cdec_x = exp_all[2 * q:2 * q + 1]

        xs = xs_ref[rows, :]
        xs_b = xs.astype(BF16)
        xdw = (xs * w_x).astype(BF16)
        y_groups = []
        for g in range(SSD_GROUPS):
            bg = b_ref[rows, g * SSD_STATE:(g + 1) * SSD_STATE]
            st_old = st_ref[g]
            s_new = _dot(bg.T.astype(BF16), xdw[:, g * rp:(g + 1) * rp])
            if mode != "state":
                bg_b = bg.astype(BF16)
                cg_b = c_ref[rows, g * SSD_STATE:(g + 1) * SSD_STATE].astype(BF16)
                cb = lax.dot_general(cg_b, bg_b, (((1,), (1,)), ((), ())),
                                     preferred_element_type=F32)
                y_off = _dot(cg_b, st_old.astype(BF16)) * eacs_x[:, g * rp:(g + 1) * rp]
                parts = []
                for r in range(SSD_HEADS // SSD_GROUPS):
                    h = g * (SSD_HEADS // SSD_GROUPS) + r
                    seg = acs[:, h:h + 1] - acsT[h:h + 1, :]
                    dec = jnp.exp(jnp.where(causal, seg, neg_inf))
                    mat = (cb * dec * dtdT[h:h + 1, :]).astype(BF16)
                    parts.append(_dot(mat, xs_b[:, h * SSD_HEADDIM:(h + 1) * SSD_HEADDIM]))
                y_groups.append(jnp.concatenate(parts, axis=1) + y_off)
            st_ref[g] = st_old * cdec_x[:, g * rp:(g + 1) * rp] + s_new

        if mode == "fwd":
            for g in range(SSD_GROUPS):
                y_ref[rows, g * rp:(g + 1) * rp] = y_groups[g]
        elif mode == "bwd":
            for g in range(SSD_GROUPS):
                sl = slice(g * rp, (g + 1) * rp)
                y = yf_ref[rows, sl] + y_groups[g] + dx_ref[:, sl] * xs[:, sl]
                yz = y * _silu(z_ref[rows, sl])
                yz = yz * lax.rsqrt(jnp.mean(yz * yz, axis=-1, keepdims=True) + EPS)
                y_ref[rows, sl] = (yz * ng_ref[:, sl]).astype(y_ref.dtype)

    for sub in (range(n_sub - 1, -1, -1) if reverse else range(n_sub)):
        one_chunk(sub * q)

    if mode == "state":
        @pl.when(step == pl.num_programs(0) - 1)
        def _():
            hfin_ref[...] = st_ref[...]


def _ssd(xbc, dt, dtT, alog, alogT, h0, *, reverse, mode, yf=None, z=None, dx=None, ng=None):
    l = xbc.shape[0]
    q = min(l, SSD_CHUNK * SSD_STEP_CHUNKS)
    nc = l // q
    di = SSD_D_INNER
    nb = di // SSD_GN
    if reverse:
        cidx = lambda c: nc - 1 - c
    else:
        cidx = lambda c: c
    st_shape = (SSD_GROUPS, SSD_STATE, di // SSD_GROUPS)
    full3 = pl.BlockSpec(st_shape, lambda c: (0, 0, 0))
    in_specs = [pl.BlockSpec((q, di), lambda c: (cidx(c), 0)),
                pl.BlockSpec((q, SSD_GN), lambda c: (cidx(c), nb)),
                pl.BlockSpec((q, SSD_GN), lambda c: (cidx(c), nb + 1)),
                pl.BlockSpec((q, dt.shape[1]), lambda c: (cidx(c), 0)),
                pl.BlockSpec((dtT.shape[0], q), lambda c: (0, cidx(c))),
                pl.BlockSpec((2, SSD_HEADS), lambda c: (0, 0)),
                pl.BlockSpec((SSD_HEADS, 2), lambda c: (0, 0)),
                full3]
    args = [xbc, xbc, xbc, dt, dtT, alog, alogT, h0]
    wide = pl.BlockSpec((q, di), lambda c: (cidx(c), 0))
    rowspec = pl.BlockSpec((1, di), lambda c: (0, 0))
    if mode == "state":
        out_shape = jax.ShapeDtypeStruct(st_shape, F32)
        out_specs = full3
    elif mode == "fwd":
        out_shape = jax.ShapeDtypeStruct((l, di), F32)
        out_specs = wide
    else:
        in_specs += [wide, wide, rowspec, rowspec]
        args += [yf, z, dx, ng]
        out_shape = jax.ShapeDtypeStruct((l, di), BF16)
        out_specs = wide
    return pl.pallas_call(
        functools.partial(_ssd_kernel, reverse=reverse, mode=mode),
        out_shape=out_shape,
        grid=(nc,),
        in_specs=in_specs,
        out_specs=out_specs,
        scratch_shapes=[pltpu.VMEM(st_shape, F32)],
        compiler_params=_cparams(("arbitrary",)),
        name="ssd_%s_%s" % (mode, "rev" if reverse else "fwd"),
    )(*args)


def _dft_tables(l):
    n = 2 * l
    hh = l // DFT_B
    ka = jnp.arange(hh, dtype=jnp.int32)
    n1 = jnp.arange(hh, dtype=jnp.int32)
    n2 = jnp.arange(DFT_B, dtype=jnp.int32)
    odd = 2 * ka + 1
    ang_a = ((odd[:, None] * (DFT_B * n1)[None, :]) % (2 * n)).astype(F32) * (math.pi / n)
    ang_b = ((n2[:, None] * odd[None, :]) % (2 * n)).astype(F32) * (math.pi / n)
    ca, sa = jnp.cos(ang_a)[None, :, :], jnp.sin(ang_a)[None, :, :]
    cb, sb = jnp.cos(ang_b)[:, :, None], jnp.sin(ang_b)[:, :, None]
    cs = ca * cb - sa * sb
    sn = sa * cb + ca * sb
    g1 = jnp.concatenate([cs, -sn], axis=1).astype(BF16)
    scale = 2.0 / n
    h2 = jnp.concatenate([jnp.swapaxes(cs, 1, 2), -jnp.swapaxes(sn, 1, 2)], axis=2) * scale
    kb = jnp.arange(DFT_B, dtype=jnp.int32)
    ph2 = (kb[:, None] * n2[None, :]) % DFT_B
    ang2 = ph2.astype(F32) * (2.0 * math.pi / DFT_B)
    fr, fi = jnp.cos(ang2), -jnp.sin(ang2)
    f2 = jnp.concatenate([jnp.concatenate([fr, -fi], axis=1),
                          jnp.concatenate([fi, fr], axis=1)], axis=0).astype(BF16)
    f2i = jnp.concatenate([jnp.concatenate([fr, fi], axis=1),
                           jnp.concatenate([-fi, fr], axis=1)], axis=0).astype(BF16)
    return g1, f2, f2i, h2.astype(BF16)


def _filter_mlp_kernel(bands_ref, w1_ref, b1_ref, f1_ref, w2_ref, b2_ref, f2_ref, o_ref, *, l):
    tm = o_ref.shape[0]
    base = pl.program_id(0) * tm
    idx = (lax.broadcasted_iota(jnp.int32, (tm, LANES), 0) + base).astype(F32)
    lane = lax.broadcasted_iota(jnp.int32, (tm, LANES), 1)
    t = idx * (1.0 / (l - 1))
    w = idx * (2.0 * math.pi / l)
    phase = jnp.where(lane <= HY_BANDS, 0.5 * math.pi, math.pi)
    feats = jnp.where(lane == 0, t,
                      jnp.where(lane < HY_EMB, jnp.sin(bands_ref[...] * w + phase), 0.0))
    h = jnp.sin(f1_ref[...] * (_dot_hi(feats, w1_ref[...]) + b1_ref[...]))
    h = jnp.sin(f2_ref[...] * (_dot_hi(h, w2_ref[...]) + b2_ref[...]))
    o_ref[...] = h.astype(o_ref.dtype)


def _filter_mlp(l, w1, b1, f1, w2, b2, f2, tm=1024):
    tm = min(tm, l)
    bands = jnp.linspace(1e-4, HY_BANDS - 1, HY_BANDS, dtype=F32)
    bands_row = jnp.zeros((1, LANES), F32).at[0, 1:1 + HY_BANDS].set(bands)
    bands_row = bands_row.at[0, 1 + HY_BANDS:HY_EMB].set(bands)
    w1p = jnp.zeros((LANES, HY_HIDDEN), F32).at[:HY_EMB].set(w1)
    full = lambda a: pl.BlockSpec(a.shape, lambda i: (0,) * a.ndim)
    args = [bands_row, w1p, b1.reshape(1, -1), f1.reshape(1, -1), w2, b2.reshape(1, -1),
            f2.reshape(1, -1)]
    return pl.pallas_call(
        functools.partial(_filter_mlp_kernel, l=l),
        out_shape=jax.ShapeDtypeStruct((l, HY_HIDDEN), BF16),
        grid=(l // tm,),
        in_specs=[full(a) for a in args],
        out_specs=pl.BlockSpec((tm, HY_HIDDEN), lambda i: (i, 0)),
        compiler_params=_cparams(("parallel",)),
        name="hyena_filter_mlp",
    )(*args)


def _pq_pitch(hh):
    return 2 * hh + 8


def _fwd_stage1(src_ref, g1_ref, pq_ref, hh):
    pqp = _pq_pitch(hh)

    def body(n2, carry):
        rows = src_ref[pl.ds(n2, hh, stride=PITCH), :]
        p = _dot(g1_ref[n2], rows.astype(BF16))
        pq_ref[pl.ds(pl.multiple_of(n2 * pqp, 8), 2 * hh), :] = p
        return carry
    lax.fori_loop(0, DFT_B, body, 0, unroll=UNROLL_OUTER)


def _load_spectrum_rows(pq_ref, ka, hh):
    pqp = _pq_pitch(hh)
    return jnp.concatenate([pq_ref[pl.ds(ka, DFT_B, stride=pqp), :],
                            pq_ref[pl.ds(hh + ka, DFT_B, stride=pqp), :]], axis=0)


def _filter_spec_kernel(hid_ref, wf_ref, wb_ref, dl_ref, g1_ref, f2_ref, kr_ref, ki_ref,
                        s_ref, d_ref, pq_ref, pq2_ref, *, l):
    hh = l // DFT_B
    t = lax.broadcasted_iota(jnp.int32, (l, LANES), 0).astype(F32) * (1.0 / (l - 1))
    dec = jnp.exp(-t * dl_ref[...])
    hid = hid_ref[...]
    hf = _dot(hid, wf_ref[...].astype(BF16)) * dec
    hb = _dot(hid, wb_ref[...].astype(BF16)) * dec
    first = lax.broadcasted_iota(jnp.int32, (l, LANES), 0) == 0
    hb = jnp.where(first, 0.0, hb)
    inv = 1.0 / (jnp.sum(jnp.abs(hf), axis=0, keepdims=True)
                 + jnp.sum(jnp.abs(hb), axis=0, keepdims=True))
    hs = hf + hb
    hd = hf - hb
    for n1 in range(hh):
        s_ref[n1 * PITCH:n1 * PITCH + DFT_B, :] = hs[n1 * DFT_B:(n1 + 1) * DFT_B]
        d_ref[n1 * PITCH:n1 * PITCH + DFT_B, :] = hd[n1 * DFT_B:(n1 + 1) * DFT_B]
    f2_re = f2_ref[0:DFT_B, :]
    pqp = _pq_pitch(hh)

    def stage1(n2, carry):
        g = g1_ref[n2]
        ps = _dot(g, s_ref[pl.ds(n2, hh, stride=PITCH), :].astype(BF16))
        pd = _dot(g, d_ref[pl.ds(n2, hh, stride=PITCH), :].astype(BF16))
        off = pl.multiple_of(n2 * pqp, 8)
        pq_ref[pl.ds(off, 2 * hh), :] = ps
        pq2_ref[pl.ds(off, 2 * hh), :] = jnp.concatenate([pd[hh:], -pd[:hh]], axis=0)
        return carry
    lax.fori_loop(0, DFT_B, stage1, 0, unroll=UNROLL_OUTER)

    def body(ka, carry):
        pp = jnp.concatenate([_load_spectrum_rows(pq_ref, ka, hh).astype(BF16),
                              _load_spectrum_rows(pq2_ref, ka, hh).astype(BF16)], axis=1)
        x = _dot(f2_re, pp)
        rows = pl.ds(pl.multiple_of(ka * DFT_B, DFT_B), DFT_B)
        kr_ref[rows, :] = (x[:, :LANES] * inv).astype(kr_ref.dtype)
        ki_ref[rows, :] = (x[:, LANES:] * inv).astype(ki_ref.dtype)
        return carry
    lax.fori_loop(0, hh, body, 0, unroll=UNROLL_MID)


def _filter_spec(hid, w3, deltas, g1, f2, l, c):
    hh = l // DFT_B
    nct = c // LANES
    orders = w3.shape[1] // (2 * c)
    spec = pl.BlockSpec((None, l, LANES), lambda o, j: (o, 0, j))
    out_sd = jax.ShapeDtypeStruct((orders, l, c), BF16)
    return pl.pallas_call(
        functools.partial(_filter_spec_kernel, l=l),
        out_shape=(out_sd, out_sd),
        grid=(orders, nct),
        in_specs=[pl.BlockSpec((l, HY_HIDDEN), lambda o, j: (0, 0)),
                  pl.BlockSpec((HY_HIDDEN, LANES), lambda o, j: (0, (2 * o) * nct + j)),
                  pl.BlockSpec((HY_HIDDEN, LANES), lambda o, j: (0, (2 * o + 1) * nct + j)),
                  pl.BlockSpec((1, LANES), lambda o, j: (0, j)),
                  pl.BlockSpec(g1.shape, lambda o, j: (0, 0, 0)),
                  pl.BlockSpec(f2.shape, lambda o, j: (0, 0))],
        out_specs=(spec, spec),
        scratch_shapes=[pltpu.VMEM((hh * PITCH, LANES), F32)] * 2
        + [pltpu.VMEM((DFT_B * _pq_pitch(hh), LANES), F32)] * 2,
        compiler_params=_cparams(("parallel", "parallel")),
        name="hyena_filter_spec",
    )(hid, w3, w3, deltas, g1, f2)


def _long_conv_kernel(u_ref, m_ref, kr_ref, ki_ref, bias_ref, g1_ref, f2_ref, f2i_ref, h2_ref,
                      o_ref, pq_ref, *, l):
    hh = l // DFT_B
    pqp = _pq_pitch(hh)
    _fwd_stage1(u_ref, g1_ref, pq_ref, hh)
    f2 = f2_ref[...]
    f2i = f2i_ref[...]

    def mid(ka, carry):
        koff = pl.multiple_of(ka * DFT_B, DFT_B)
        x = _dot(f2, _load_spectrum_rows(pq_ref, ka, hh).astype(BF16))
        xr, xi = x[:DFT_B], x[DFT_B:]
        kr = kr_ref[pl.ds(koff, DFT_B), :].astype(F32)
        ki = ki_ref[pl.ds(koff, DFT_B), :].astype(F32)
        yy = jnp.concatenate([xr * kr - xi * ki, xr * ki + xi * kr], axis=0).astype(BF16)
        qq = _dot(f2i, yy)
        pq_ref[pl.ds(ka, DFT_B, stride=pqp), :] = qq[:DFT_B]
        pq_ref[pl.ds(hh + ka, DFT_B, stride=pqp), :] = qq[DFT_B:]
        return carry
    lax.fori_loop(0, hh, mid, 0, unroll=UNROLL_MID)

    bias = bias_ref[...]

    def last(n2, carry):
        qq = pq_ref[pl.ds(pl.multiple_of(n2 * pqp, 8), 2 * hh), :].astype(BF16)
        y = _dot(h2_ref[n2], qq)
        u = u_ref[pl.ds(n2, hh, stride=PITCH), :]
        m = m_ref[pl.ds(n2, hh, stride=PITCH), :]
        o_ref[pl.ds(n2, hh, stride=PITCH), :] = (m * (y + bias * u)).astype(o_ref.dtype)
        return carry
    lax.fori_loop(0, DFT_B, last, 0, unroll=UNROLL_OUTER)
    for n1 in range(hh):
        o_ref[n1 * PITCH + DFT_B:(n1 + 1) * PITCH, :] = jnp.zeros((PITCH - DFT_B, LANES), o_ref.dtype)


def _long_conv(u_arr, u_blk, m_arr, m_blk, kr, ki, order, bias, tables, l, c):
    g1, f2, f2i, h2 = tables
    hh = l // DFT_B
    nct = c // LANES
    kspec = pl.BlockSpec((None, l, LANES), lambda j: (order, 0, j))
    return pl.pallas_call(
        functools.partial(_long_conv_kernel, l=l),
        out_shape=jax.ShapeDtypeStruct((hh * PITCH, c), F32),
        grid=(nct,),
        in_specs=[pl.BlockSpec((hh * PITCH, LANES), lambda j: (0, u_blk * nct + j)),
                  pl.BlockSpec((hh * PITCH, LANES), lambda j: (0, m_blk * nct + j)),
                  kspec, kspec,
                  pl.BlockSpec((1, LANES), lambda j: (0, j)),
                  pl.BlockSpec(g1.shape, lambda j: (0, 0, 0)),
                  pl.BlockSpec(f2.shape, lambda j: (0, 0)),
                  pl.BlockSpec(f2i.shape, lambda j: (0, 0)),
                  pl.BlockSpec(h2.shape, lambda j: (0, 0, 0))],
        out_specs=pl.BlockSpec((hh * PITCH, LANES), lambda j: (0, j)),
        scratch_shapes=[pltpu.VMEM((DFT_B * _pq_pitch(hh), LANES), F32)],
        compiler_params=_cparams(("parallel",)),
        name="hyena_long_conv",
    )(u_arr, m_arr, kr, ki, bias.reshape(1, c), g1, f2, f2i, h2)


def _merge_kernel(hx_ref, ys_ref, yh_ref, wg1_ref, wg2_ref, w1_ref, w2_ref, gb1_ref, gb2_ref, o_ref):
    hx = hx_ref[...]
    g1 = _sigmoid(_dot_nt(hx, wg1_ref[...]) + gb1_ref[...])
    g2 = _sigmoid(_dot_nt(hx, wg2_ref[...]) + gb2_ref[...])
    yh = yh_ref[:, 0:DFT_B, :].reshape(hx.shape).astype(BF16)
    out = g1 * _dot(ys_ref[...], w1_ref[...]) + g2 * _dot(yh, w2_ref[...])
    o_ref[...] = out.astype(o_ref.dtype)


def _merge(hx, ys, yh, wgt, row0, w1, w2, gate_b, tm=1024, tn=256):
    m, d = hx.shape
    nt = d // tn
    a_spec = pl.BlockSpec((tm, d), lambda i, j: (i, 0))
    w_spec = pl.BlockSpec((d, tn), lambda i, j: (0, j))
    gb = gate_b.reshape(1, 2 * d)
    return pl.pallas_call(
        _merge_kernel,
        out_shape=jax.ShapeDtypeStruct((m, d), BF16),
        grid=(m // tm, nt),
        in_specs=[a_spec, a_spec,
                  pl.BlockSpec((tm // DFT_B, PITCH, d), lambda i, j: (i, 0, 0)),
                  pl.BlockSpec((pl.Element(tn), pl.Element(d)),
                               lambda i, j: (pl.multiple_of(row0 + j * tn, ROW_ALIGN), 0)),
                  pl.BlockSpec((pl.Element(tn), pl.Element(d)),
                               lambda i, j: (pl.multiple_of(row0 + (nt + j) * tn, ROW_ALIGN), 0)),
                  w_spec, w_spec,
                  pl.BlockSpec((1, tn), lambda i, j: (0, j)),
                  pl.BlockSpec((1, tn), lambda i, j: (0, nt + j))],
        out_specs=pl.BlockSpec((tm, tn), lambda i, j: (i, j)),
        compiler_params=_cparams(("parallel", "parallel")),
        name="merge",
    )(hx, ys, yh, wgt, wgt, w1, w2, gb, gb)


def _oproj_kernel(a_ref, w_ref, x_ref, gt_ref, o_ref):
    o_ref[...] = x_ref[...] + gt_ref[...] * _dot(a_ref[...], w_ref[...])


def _oproj(a, w, x, gt, tm=2048, tn=512):
    m, d = a.shape
    n = w.shape[1]
    tm = min(tm, m)
    return pl.pallas_call(
        _oproj_kernel,
        out_shape=jax.ShapeDtypeStruct((m, n), F32),
        grid=(m // tm, n // tn),
        in_specs=[pl.BlockSpec((tm, d), lambda i, j: (i, 0)),
                  pl.BlockSpec((d, tn), lambda i, j: (0, j)),
                  pl.BlockSpec((tm, tn), lambda i, j: (i, j)),
                  pl.BlockSpec((1, tn), lambda i, j: (0, j))],
        out_specs=pl.BlockSpec((tm, tn), lambda i, j: (i, j)),
        compiler_params=_cparams(("parallel", "parallel")),
        name="oproj",
    )(a, w, x, gt)


def _pack_bf16_pairs(v):
    half = v.shape[1] // 2
    bits = lax.bitcast_convert_type(v.astype(BF16).astype(F32), jnp.uint32)
    return (bits[:, :half] >> 16) | (bits[:, half:] & jnp.uint32(0xFFFF0000))


def _unpack_bf16_pairs(p):
    lo = lax.bitcast_convert_type(p << 16, F32).astype(BF16)
    hi = lax.bitcast_convert_type(p & jnp.uint32(0xFFFF0000), F32).astype(BF16)
    return lo, hi


def _router_kernel(x_ref, g_ref, sh_ref, sc_ref, wr_ref, br_ref, h_ref, ids_ref, wts_ref):
    x = x_ref[...]
    y = x * lax.rsqrt(jnp.mean(x * x, axis=-1, keepdims=True) + EPS) * g_ref[...]
    h = y * (1.0 + sc_ref[...]) + sh_ref[...]
    h_ref[...] = _pack_bf16_pairs(h)
    logits = _dot_mid(h, wr_ref[...]) + br_ref[...]
    tm = x.shape[0]
    lane = lax.broadcasted_iota(jnp.int32, (tm, LANES), 1)
    neg = jnp.float32(-jnp.inf)
    big = jnp.int32(LANES)
    is_grp = (lane >= MOE_EXPERTS) & (lane < MOE_EXPERTS + MOE_GROUPS)
    gl = jnp.where(is_grp, logits, neg)
    gmax = jnp.max(gl, axis=-1, keepdims=True)
    gidx = jnp.min(jnp.where(gl == gmax, lane, big), axis=-1, keepdims=True) - MOE_EXPERTS
    gw = 1.0 / jnp.sum(jnp.where(is_grp, jnp.exp(logits - gmax), 0.0), axis=-1, keepdims=True)
    in_grp = (lane < MOE_EXPERTS) & ((lane // MOE_PER_GROUP) == gidx)
    el = jnp.where(in_grp, logits, neg)
    m1 = jnp.max(el, axis=-1, keepdims=True)
    i1 = jnp.min(jnp.where(el == m1, lane, big), axis=-1, keepdims=True)
    el2 = jnp.where(lane == i1, neg, el)
    m2 = jnp.max(el2, axis=-1, keepdims=True)
    i2 = jnp.min(jnp.where(el2 == m2, lane, big), axis=-1, keepdims=True)
    e21 = jnp.exp(m2 - m1)
    w1 = gw / (1.0 + e21)
    w2 = gw * e21 / (1.0 + e21)
    ids_ref[...] = jnp.where(lane == 0, i1, jnp.where(lane == 1, i2, -1))
    wts_ref[...] = jnp.where(lane == 0, w1, jnp.where(lane == 1, w2, 0.0))


def _router(x1, g, sh, sc, w_rg, b_rg, w_re, b_re, tm=512):
    m, d = x1.shape
    wr = jnp.zeros((d, LANES), F32).at[:, :MOE_EXPERTS].set(w_re)
    wr = wr.at[:, MOE_EXPERTS:MOE_EXPERTS + MOE_GROUPS].set(w_rg)
    br = jnp.zeros((1, LANES), F32).at[0, :MOE_EXPERTS].set(b_re)
    br = br.at[0, MOE_EXPERTS:MOE_EXPERTS + MOE_GROUPS].set(b_rg)
    row = pl.BlockSpec((1, d), lambda i: (0, 0))
    return pl.pallas_call(
        _router_kernel,
        out_shape=(jax.ShapeDtypeStruct((m, d // 2), jnp.uint32),
                   jax.ShapeDtypeStruct((m, LANES), jnp.int32),
                   jax.ShapeDtypeStruct((m, LANES), F32)),
        grid=(m // tm,),
        in_specs=[pl.BlockSpec((tm, d), lambda i: (i, 0)), row, row, row,
                  pl.BlockSpec((d, LANES), lambda i: (0, 0)),
                  pl.BlockSpec((1, LANES), lambda i: (0, 0))],
        out_specs=(pl.BlockSpec((tm, d // 2), lambda i: (i, 0)),
                   pl.BlockSpec((tm, LANES), lambda i: (i, 0)),
                   pl.BlockSpec((tm, LANES), lambda i: (i, 0))),
        compiler_params=_cparams(("parallel",)),
        name="router",
    )(x1, g.reshape(1, d), sh, sc, wr, br)


def _moe_positions_kernel(ids_ref, pos_ref, meta_ref):
    n = ids_ref.shape[0]
    t = MOE_TILE
    lane = lax.broadcasted_iota(jnp.int32, (t, LANES), 1)

    def onehot(k):
        idt = ids_ref[pl.ds(pl.multiple_of(k * t, t), t), :]
        i1, i2 = idt[:, 0:1], idt[:, 1:2]
        return i1, i2, jnp.where((lane == i1) | (lane == i2), 1.0, 0.0)

    def count(k, acc):
        return acc + jnp.sum(onehot(k)[2], axis=0, keepdims=True)
    total = lax.fori_loop(0, n // t, count, jnp.zeros((1, LANES), F32))
    padded = (((total.astype(jnp.int32) + (t - 1)) // t) * t).astype(F32)
    r128 = lax.broadcasted_iota(jnp.int32, (LANES, LANES), 0)
    c128 = lax.broadcasted_iota(jnp.int32, (LANES, LANES), 1)
    before = jnp.where(r128 < c128, 1.0, 0.0).astype(BF16)
    off = _dot_sel_r(jnp.broadcast_to(padded, (8, LANES)), before)[0:1]
    row = lax.broadcasted_iota(jnp.int32, (t, t), 0)
    col = lax.broadcasted_iota(jnp.int32, (t, t), 1)
    tri = jnp.where(col <= row, 1.0, 0.0).astype(BF16)

    def place(k, seen):
        i1, i2, oh = onehot(k)
        base = off + seen + _dot(tri, oh.astype(BF16)) - oh
        p1 = jnp.sum(jnp.where(lane == i1, base, 0.0), axis=1, keepdims=True)
        p2 = jnp.sum(jnp.where(lane == i2, base, 0.0), axis=1, keepdims=True)
        pos_ref[pl.ds(pl.multiple_of(k * t, t), t), :] = jnp.where(
            lane == 0, p1, jnp.where(lane == 1, p2, 0.0)).astype(jnp.int32)
        return seen + jnp.sum(oh, axis=0, keepdims=True)
    lax.fori_loop(0, n // t, place, jnp.zeros((1, LANES), F32))

    ends = off + padded
    start = (r128 * t).astype(F32)
    done = jnp.where((jnp.broadcast_to(ends, (LANES, LANES)) <= start) & (c128 < MOE_EXPERTS), 1.0, 0.0)
    tile_expert = jnp.minimum(jnp.sum(done, axis=1, keepdims=True), MOE_EXPERTS - 1.0)
    used = jnp.sum(jnp.where(c128[0:1] == MOE_EXPERTS - 1, ends, 0.0), axis=1, keepdims=True) / t
    lanef = c128.astype(F32)
    later = (lanef > tile_expert) & (jnp.broadcast_to(padded, (LANES, LANES)) > 0.0) & (c128 < MOE_EXPERTS)
    nxt = jnp.min(jnp.where(later, lanef, float(LANES)), axis=1, keepdims=True)
    nxt = jnp.where(nxt >= float(LANES), -1.0, nxt)
    meta_ref[...] = jnp.where(c128 == 0, tile_expert,
                              jnp.where(c128 == 1, used, jnp.where(c128 == 2, nxt, 0.0))).astype(jnp.int32)


def _moe_positions(ids):
    m = ids.shape[0]
    return pl.pallas_call(
        _moe_positions_kernel,
        out_shape=(jax.ShapeDtypeStruct((m, LANES), jnp.int32),
                   jax.ShapeDtypeStruct((LANES, LANES), jnp.int32)),
        grid=(1,),
        in_specs=[pl.BlockSpec((m, LANES), lambda i: (0, 0))],
        out_specs=(pl.BlockSpec((m, LANES), lambda i: (0, 0)),
                   pl.BlockSpec((LANES, LANES), lambda i: (0, 0))),
        compiler_params=_cparams(("arbitrary",)),
        name="moe_positions",
    )(ids)


def _moe_dispatch_kernel(p1_ref, p2_ref, h_ref, xs_in_ref, xs_ref, sem):
    del xs_in_ref
    t = h_ref.shape[0]
    base = pl.program_id(0) * t

    def issue(r, carry):
        src = h_ref.at[pl.ds(r, 1), :]
        pltpu.make_async_copy(src, xs_ref.at[pl.ds(p1_ref[base + r], 1), :], sem.at[0]).start()
        pltpu.make_async_copy(src, xs_ref.at[pl.ds(p2_ref[base + r], 1), :], sem.at[1]).start()
        return carry
    lax.fori_loop(0, t, issue, 0, unroll=8)
    pltpu.make_async_copy(h_ref, xs_ref.at[pl.ds(0, t), :], sem.at[0]).wait()
    pltpu.make_async_copy(h_ref, xs_ref.at[pl.ds(0, t), :], sem.at[1]).wait()


def _moe_dispatch(hp, pos1, pos2, n_sorted):
    m, w = hp.shape
    t = MOE_TILE
    zeros = jnp.zeros((n_sorted, w), hp.dtype)
    return pl.pallas_call(
        _moe_dispatch_kernel,
        out_shape=jax.ShapeDtypeStruct((n_sorted, w), hp.dtype),
        grid_spec=pltpu.PrefetchScalarGridSpec(
            num_scalar_prefetch=2,
            grid=(m // t,),
            in_specs=[pl.BlockSpec((t, w), lambda i, p1, p2: (i, 0)),
                      pl.BlockSpec(memory_space=pl.ANY)],
            out_specs=pl.BlockSpec(memory_space=pl.ANY),
            scratch_shapes=[pltpu.SemaphoreType.DMA((2,))]),
        input_output_aliases={3: 0},
        compiler_params=_cparams(("arbitrary",)),
        name="moe_dispatch",
    )(pos1, pos2, hp, zeros)


def _moe_experts_kernel(te_ref, nu_ref, nx_ref, x_ref, wg_hbm, wu_hbm, wd_hbm, y_ref,
                        wg_f, wu_f, wd_f, wg_b, wu_b, wd_b, slot_ref, sem):
    i = pl.program_id(0)
    live = i < nu_ref[0]
    new_expert = (i == 0) | (te_ref[i] != te_ref[jnp.maximum(i - 1, 0)])

    def copies(e, s):
        return (pltpu.make_async_copy(wg_hbm.at[e], wg_f.at[s], sem.at[s, 0]),
                pltpu.make_async_copy(wu_hbm.at[e], wu_f.at[s], sem.at[s, 1]),
                pltpu.make_async_copy(wd_hbm.at[e], wd_f.at[s], sem.at[s, 2]))

    @pl.when(i == 0)
    def _():
        slot_ref[0] = 0
        for cp in copies(te_ref[0], 0):
            cp.start()

    @pl.when(live & new_expert)
    def _():
        s = slot_ref[0]
        for cp in copies(te_ref[i], s):
            cp.wait()
        wg_b[...] = wg_f[s].astype(BF16)
        wu_b[...] = wu_f[s].astype(BF16)
        wd_b[...] = wd_f[s].astype(BF16)
        slot_ref[0] = 1 - s

        @pl.when(nx_ref[i] >= 0)
        def _():
            for cp in copies(nx_ref[i], 1 - s):
                cp.start()

    @pl.when(live)
    def _():
        x_lo, x_hi = _unpack_bf16_pairs(x_ref[...])
        half = x_lo.shape[1]
        hg = _dot(x_lo, wg_b[0:half, :]) + _dot(x_hi, wg_b[half:, :])
        hu = _dot(x_lo, wu_b[0:half, :]) + _dot(x_hi, wu_b[half:, :])
        y_ref[...] = _dot((_silu(hg) * hu).astype(BF16), wd_b[...])

    @pl.when(jnp.logical_not(live))
    def _():
        y_ref[...] = jnp.zeros_like(y_ref)


def _moe_experts(xs, tile_expert, n_used, next_expert, wg, wu, wd):
    ns = xs.shape[0]
    ne, d, f = wg.shape
    t = MOE_TILE
    hbm = pl.BlockSpec(memory_space=pl.ANY)
    return pl.pallas_call(
        _moe_experts_kernel,
        out_shape=jax.ShapeDtypeStruct((ns, d), F32),
        grid_spec=pltpu.PrefetchScalarGridSpec(
            num_scalar_prefetch=3,
            grid=(ns // t,),
            in_specs=[pl.BlockSpec((t, d // 2), lambda i, te, nu, nx: (i, 0)), hbm, hbm, hbm],
            out_specs=pl.BlockSpec((t, d), lambda i, te, nu, nx: (i, 0)),
            scratch_shapes=[pltpu.VMEM((2, d, f), F32), pltpu.VMEM((2, d, f), F32),
                            pltpu.VMEM((2, f, d), F32),
                            pltpu.VMEM((d, f), BF16), pltpu.VMEM((d, f), BF16),
                            pltpu.VMEM((f, d), BF16),
                            pltpu.SMEM((1,), jnp.int32), pltpu.SemaphoreType.DMA((2, 3))]),
        compiler_params=_cparams(("arbitrary",)),
        name="moe_experts",
    )(tile_expert, n_used, next_expert, xs, wg, wu, wd)


def _moe_combine_kernel(p1_ref, p2_ref, x_ref, wts_ref, gt_ref, fg_ref, y_ref, o_ref, ya, yb, sem):
    t = x_ref.shape[0]
    i = pl.program_id(0)
    n = pl.num_programs(0)

    def row_copy(src_row, buf, slot, r, which):
        return pltpu.make_async_copy(y_ref.at[pl.ds(src_row, 1), :], buf.at[slot, pl.ds(r, 1), :],
                                     sem.at[which, slot])

    def issue(tile, slot):
        def body(r, carry):
            row_copy(p1_ref[tile * t + r], ya, slot, r, 0).start()
            row_copy(p2_ref[tile * t + r], yb, slot, r, 1).start()
            return carry
        lax.fori_loop(0, t, body, 0, unroll=8)

    @pl.when(i == 0)
    def _():
        issue(0, 0)

    @pl.when(i + 1 < n)
    def _():
        issue(i + 1, (i + 1) % 2)

    slot = i % 2
    pltpu.make_async_copy(y_ref.at[pl.ds(0, t), :], ya.at[slot], sem.at[0, slot]).wait()
    pltpu.make_async_copy(y_ref.at[pl.ds(0, t), :], yb.at[slot], sem.at[1, slot]).wait()
    w = wts_ref[...]
    moe = w[:, 0:1] * ya[slot] + w[:, 1:2] * yb[slot]
    x2 = x_ref[...] + gt_ref[...] * moe
    o_ref[...] = x2 * lax.rsqrt(jnp.mean(x2 * x2, axis=-1, keepdims=True) + EPS) * fg_ref[...]


def _moe_combine(x1, wts, gt, fg, y_sorted, pos1, pos2):
    m, d = x1.shape
    t = MOE_TILE
    row = pl.BlockSpec((1, d), lambda i, p1, p2: (0, 0))
    return pl.pallas_call(
        _moe_combine_kernel,
        out_shape=jax.ShapeDtypeStruct((m, d), F32),
        grid_spec=pltpu.PrefetchScalarGridSpec(
            num_scalar_prefetch=2,
            grid=(m // t,),
            in_specs=[pl.BlockSpec((t, d), lambda i, p1, p2: (i, 0)),
                      pl.BlockSpec((t, LANES), lambda i, p1, p2: (i, 0)),
                      row, row,
                      pl.BlockSpec(memory_space=pl.ANY)],
            out_specs=pl.BlockSpec((t, d), lambda i, p1, p2: (i, 0)),
            scratch_shapes=[pltpu.VMEM((2, t, d), F32), pltpu.VMEM((2, t, d), F32),
                            pltpu.SemaphoreType.DMA((2, 2))]),
        compiler_params=_cparams(("arbitrary",)),
        name="moe_combine",
    )(pos1, pos2, x1, wts, gt, fg.reshape(1, d), y_sorted)


def kernel(x, c, ctx, c_ctx, ada_w, ada_b, norm1_g, w_in, ssd_conv_w, ssd_conv_b, ssd_dt_bias,
           ssd_a_log, ssd_d, ssd_norm_g, w_ssd_out, hy_conv_w, hy_conv_b, hy_f_w1, hy_f_b1,
           hy_f_freq1, hy_f_w2, hy_f_b2, hy_f_freq2, hy_f_w3, hy_bias, w_hy_out, gate_b, w_o,
           norm2_g, moe_w_rg, moe_b_rg, moe_w_re, moe_b_re, moe_w_gate, moe_w_up, moe_w_down,
           final_g):
    bsz, l, d = x.shape
    assert bsz == 1 and ada_w.shape[0] == 1
    lc = ctx.shape[1]
    di = SSD_D_INNER
    xbc_cols = di + 2 * SSD_GN
    col_xbc = di
    col_dt = col_xbc + xbc_cols
    col_hy = col_dt + 2 * SSD_HEADS
    hyw = hy_bias.shape[-1]

    x2d = x[0]
    ctx2d = ctx[0]

    cc = jnp.zeros((8, d), F32).at[0].set(c[0]).at[1].set(c_ctx)
    mods = _adaln(cc, ada_w[0], ada_b[0])
    sh1, sc1, gt1, sh2, sc2, gt2 = [mods[0:1, i * d:(i + 1) * d] for i in range(6)]
    csh1, csc1 = mods[1:2, 0:d], mods[1:2, d:2 * d]

    hx = _normmod(x2d, norm1_g[0], sh1, sc1, tm=512)
    hc = _normmod(ctx2d, norm1_g[0], csh1, csc1, tm=lc)

    w_in_t = w_in[0].T
    col_gate = col_hy + 3 * hyw
    wt_gate = w_in_t[col_gate:].astype(BF16)
    n_dt = 2 * SSD_HEADS
    dt_bias = ssd_dt_bias[0].reshape(-1)

    xbc_c = _proj(hc, w_in_t, col_xbc, xbc_cols, "conv_silu", tm=lc, tn=512, conv_w=ssd_conv_w[0],
                  bias=ssd_conv_b[0], group=lc)
    dt_c = _proj(hc, w_in_t, col_dt, n_dt, "softplus", tm=lc, tn=n_dt, bias=dt_bias)
    alog = ssd_a_log[0]
    alogT = alog.T
    zero_state = jnp.zeros((SSD_GROUPS, SSD_STATE, di // SSD_GROUPS), F32)
    h0f = _ssd(xbc_c, dt_c, dt_c.T, alog, alogT, zero_state, reverse=False, mode="state")
    h0b = _ssd(xbc_c, dt_c, dt_c.T, alog, alogT, zero_state, reverse=True, mode="state")

    z = _proj(hx, w_in_t, 0, di, "none", tm=PROJ_TM, tn=512)
    xbc = _proj(hx, w_in_t, col_xbc, xbc_cols, "conv_silu", tm=PROJ_TM, tn=512, conv_w=ssd_conv_w[0],
                bias=ssd_conv_b[0])
    dt = _proj(hx, w_in_t, col_dt, n_dt, "softplus", tm=PROJ_TM, tn=n_dt, bias=dt_bias)
    u3 = _proj(hx, w_in_t, col_hy, 3 * hyw, "conv", tm=PROJ_TM, tn=512, conv_w=hy_conv_w[0],
               bias=hy_conv_b[0], pitched=True).reshape(l // DFT_B * PITCH, 3 * hyw)

    dtT = dt.T
    yf = _ssd(xbc, dt, dtT, alog, alogT, h0f, reverse=False, mode="fwd")
    dx = jnp.repeat(ssd_d[0], SSD_HEADDIM).reshape(1, di)
    y_ssd = _ssd(xbc, dt, dtT, alog, alogT, h0b, reverse=True, mode="bwd", yf=yf, z=z, dx=dx,
                 ng=ssd_norm_g[0].reshape(1, di))

    tables = _dft_tables(l)
    hid = _filter_mlp(l, hy_f_w1[0], hy_f_b1[0], hy_f_freq1[0], hy_f_w2[0], hy_f_b2[0],
                      hy_f_freq2[0])
    max_decay = math.log(HY_DECAY_TARGET) / HY_FAST_DECAY_PCT
    min_decay = math.log(HY_DECAY_TARGET) / HY_SLOW_DECAY_PCT
    deltas = jnp.abs(jnp.linspace(min_decay, max_decay, hyw, dtype=F32)).reshape(1, hyw)
    kr, ki = _filter_spec(hid, hy_f_w3[0], deltas, tables[0], tables[1], l, hyw)
    z2 = _long_conv(u3, 0, u3, 1, kr, ki, 0, hy_bias[0, 0], tables, l, hyw)
    y_hy = _long_conv(z2, 0, u3, 2, kr, ki, 1, hy_bias[0, 1], tables, l, hyw)

    merged = _merge(hx, y_ssd, y_hy.reshape(l // DFT_B, PITCH, hyw), wt_gate, 0,
                    w_ssd_out[0].astype(BF16), w_hy_out[0].astype(BF16), gate_b[0])
    x1 = _oproj(merged, w_o[0].astype(BF16), x2d, gt1)

    h2, ids, wts = _router(x1, norm2_g[0], sh2, sc2, moe_w_rg[0], moe_b_rg[0], moe_w_re[0],
                           moe_b_re[0])
    pos, meta = _moe_positions(ids)
    pos1, pos2 = pos[:, 0], pos[:, 1]
    n_sorted = 2 * l + MOE_EXPERTS * MOE_TILE
    n_tiles = n_sorted // MOE_TILE
    assert n_tiles <= LANES
    xs = _moe_dispatch(h2, pos1, pos2, n_sorted)
    ys = _moe_experts(xs, meta[:n_tiles, 0], meta[0, 1:2], meta[:n_tiles, 2], moe_w_gate[0],
                      moe_w_up[0], moe_w_down[0])
    out = _moe_combine(x1, wts, gt2, final_g, ys, pos1, pos2)
    return out[None]
```

```python
import functools
import math

import jax
import jax.numpy as jnp
from jax import lax
from jax.experimental import pallas as pl
from jax.experimental.pallas import tpu as pltpu

F32 = jnp.float32
BF16 = jnp.bfloat16

EPS = 1e-6
GRID_W = 64
SSD_HEADS = 32
SSD_HEADDIM = 64
SSD_GROUPS = 4
SSD_STATE = 128
SSD_CHUNK = 128
SSD_D_INNER = SSD_HEADS * SSD_HEADDIM
SSD_GN = SSD_GROUPS * SSD_STATE
HY_BANDS = 16
HY_EMB = 33
HY_HIDDEN = 64
HY_DECAY_TARGET = 1e-2
HY_FAST_DECAY_PCT = 0.3
HY_SLOW_DECAY_PCT = 1.5
MOE_GROUPS = 4
MOE_PER_GROUP = 4
MOE_EXPERTS = 16

LANES = 128
DFT_B = 128
PITCH = 136
VMEM_LIMIT = 56 * 1024 * 1024
UNROLL_OUTER = 64
UNROLL_MID = 16
ROW_ALIGN = 16
PROJ_TM = 2048
PROJ_SUB_ROWS = 512
SSD_STEP_CHUNKS = 4
MOE_TILE = 256


def _cparams(sem):
    return pltpu.CompilerParams(dimension_semantics=sem, vmem_limit_bytes=VMEM_LIMIT)


def _dot(a, b):
    return jnp.dot(a, b, preferred_element_type=F32)


def _dot_nt(a, bt):
    return lax.dot_general(a, bt, (((1,), (1,)), ((), ())), preferred_element_type=F32)


def _sigmoid(x):
    return 1.0 / (1.0 + jnp.exp(-x))


def _silu(x):
    return x * _sigmoid(x)


def _split3(v):
    b1 = v.astype(BF16)
    r1 = v - b1.astype(F32)
    b2 = r1.astype(BF16)
    b3 = (r1 - b2.astype(F32)).astype(BF16)
    return b1, b2, b3


def _dot_sel_r(v, sel_b):
    b1, b2, b3 = _split3(v)
    return _dot(b1, sel_b) + _dot(b2, sel_b) + _dot(b3, sel_b)


def _dot_sel_l(sel_b, v):
    b1, b2, b3 = _split3(v)
    return _dot(sel_b, b1) + _dot(sel_b, b2) + _dot(sel_b, b3)


def _dot_mid(a, b):
    a1 = a.astype(BF16)
    a2 = (a - a1.astype(F32)).astype(BF16)
    b1 = b.astype(BF16)
    b2 = (b - b1.astype(F32)).astype(BF16)
    return _dot(a1, b1) + (_dot(a1, b2) + _dot(a2, b1))


def _dot_hi(a, b):
    a1, a2, a3 = _split3(a)
    b1, b2, b3 = _split3(b)
    return (_dot(a1, b1) + (_dot(a1, b2) + _dot(a2, b1))
            + (_dot(a1, b3) + _dot(a2, b2) + _dot(a3, b1)))


def _adaln_kernel(c_ref, w_ref, b_ref, o_ref):
    s = _silu(c_ref[...]).astype(BF16)
    o_ref[...] = _dot(s, w_ref[...].astype(BF16)) + b_ref[...]


def _adaln(cc, w, b, tn=1024):
    m, d = cc.shape
    n = w.shape[1]
    return pl.pallas_call(
        _adaln_kernel,
        out_shape=jax.ShapeDtypeStruct((m, n), F32),
        grid=(n // tn,),
        in_specs=[pl.BlockSpec((m, d), lambda j: (0, 0)),
                  pl.BlockSpec((d, tn), lambda j: (0, j)),
                  pl.BlockSpec((1, tn), lambda j: (0, j))],
        out_specs=pl.BlockSpec((m, tn), lambda j: (0, j)),
        compiler_params=_cparams(("parallel",)),
        name="adaln",
    )(cc, w, b.reshape(1, n))


def _normmod_kernel(x_ref, g_ref, sh_ref, sc_ref, o_ref):
    x = x_ref[...]
    y = x * lax.rsqrt(jnp.mean(x * x, axis=-1, keepdims=True) + EPS) * g_ref[...]
    o_ref[...] = (y * (1.0 + sc_ref[...]) + sh_ref[...]).astype(o_ref.dtype)


def _normmod(x, g, sh, sc, tm):
    m, d = x.shape
    row = pl.BlockSpec((1, d), lambda i: (0, 0))
    return pl.pallas_call(
        _normmod_kernel,
        out_shape=jax.ShapeDtypeStruct((m, d), BF16),
        grid=(m // tm,),
        in_specs=[pl.BlockSpec((tm, d), lambda i: (i, 0)), row, row, row],
        out_specs=pl.BlockSpec((tm, d), lambda i: (i, 0)),
        compiler_params=_cparams(("parallel",)),
        name="normmod",
    )(x, g.reshape(1, d), sh, sc)


def _conv_taps(cw_ref, rows, group):
    n_taps, tn = cw_ref.shape
    half = n_taps // 2
    pos = lax.broadcasted_iota(jnp.int32, (rows, tn), 0) & (group - 1)
    taps = []
    for k in range(n_taps):
        d = k - half
        wk = cw_ref[k:k + 1, :]
        if d != 0:
            wk = jnp.where((pos + d >= 0) & (pos + d < group), wk, 0.0)
        taps.append((d, wk))
    return taps


def _conv_rows(acc, taps):
    rows = acc.shape[0]
    out = None
    for d, wk in taps:
        term = (acc if d == 0 else pltpu.roll(acc, (rows - d) % rows, 0)) * wk
        out = term if out is None else out + term
    return out


def _proj_kernel(a_ref, w_ref, *rest, epi, group):
    o_ref = rest[-1]
    tm = a_ref.shape[0]
    sub = max(group, min(tm, PROJ_SUB_ROWS))
    w = w_ref[...].astype(BF16)
    taps = _conv_taps(rest[0], sub, group) if epi in ("conv", "conv_silu") else None
    for s in range(tm // sub):
        acc = _dot_nt(a_ref[s * sub:(s + 1) * sub, :], w)
        if epi == "none":
            out = acc
        elif epi == "conv":
            out = _conv_rows(acc, taps) + rest[1][...]
        elif epi == "conv_silu":
            out = _silu(_conv_rows(acc, taps) + rest[1][...])
        elif epi == "softplus":
            v = acc + rest[0][...]
            out = jnp.maximum(v, 0.0) + jnp.log(1.0 + jnp.exp(-jnp.abs(v)))
        else:
            raise ValueError(epi)
        if len(o_ref.shape) == 3:
            nb, tn = sub // DFT_B, o_ref.shape[2]
            o_ref[s * nb:(s + 1) * nb, 0:DFT_B, :] = out.reshape(nb, DFT_B, tn).astype(o_ref.dtype)
            o_ref[s * nb:(s + 1) * nb, DFT_B:PITCH, :] = jnp.zeros((nb, PITCH - DFT_B, tn), o_ref.dtype)
        else:
            o_ref[s * sub:(s + 1) * sub, :] = out.astype(o_ref.dtype)


def _proj(a, wt, row0, n, epi, tm, tn, conv_w=None, bias=None, group=GRID_W, out_dtype=F32,
          pitched=False):
    m, k = a.shape
    tm = min(tm, m)
    tn = min(tn, n)
    if pitched:
        out_shape = jax.ShapeDtypeStruct((m // DFT_B, PITCH, n), out_dtype)
        out_spec = pl.BlockSpec((tm // DFT_B, PITCH, tn), lambda i, j: (i, 0, j))
    else:
        out_shape = jax.ShapeDtypeStruct((m, n), out_dtype)
        out_spec = pl.BlockSpec((tm, tn), lambda i, j: (i, j))
    extra, extra_specs = [], []
    if conv_w is not None:
        extra.append(conv_w)
        extra_specs.append(pl.BlockSpec((conv_w.shape[0], tn), lambda i, j: (0, j)))
    if bias is not None:
        extra.append(bias.reshape(1, n))
        extra_specs.append(pl.BlockSpec((1, tn), lambda i, j: (0, j)))
    return pl.pallas_call(
        functools.partial(_proj_kernel, epi=epi, group=group),
        out_shape=out_shape,
        grid=(m // tm, n // tn),
        in_specs=[pl.BlockSpec((tm, k), lambda i, j: (i, 0)),
                  pl.BlockSpec((pl.Element(tn), pl.Element(k)),
                               lambda i, j: (pl.multiple_of(row0 + j * tn, ROW_ALIGN), 0))]
        + extra_specs,
        out_specs=out_spec,
        compiler_params=_cparams(("parallel", "parallel")),
        name="proj_" + epi,
    )(a, wt, *extra)


def _ssd_kernel(xs_ref, b_ref, c_ref, dt_ref, dtT_ref, alog_ref, alogT_ref, h0_ref, *rest,
                reverse, mode):
    if mode == "state":
        hfin_ref, st_ref = rest
    elif mode == "fwd":
        y_ref, st_ref = rest
    else:
        yf_ref, z_ref, dx_ref, ng_ref, y_ref, st_ref = rest
    q = SSD_CHUNK
    n_sub = xs_ref.shape[0] // q
    nh = SSD_HEADS
    rp = SSD_D_INNER // SSD_GROUPS
    step = pl.program_id(0)

    @pl.when(step == 0)
    def _():
        st_ref[...] = h0_ref[...]

    d = 1 if reverse else 0
    a_row = -jnp.exp(alog_ref[d:d + 1, :])
    a_col = -jnp.exp(alogT_ref[:, d:d + 1])
    row = lax.broadcasted_iota(jnp.int32, (q, q), 0)
    col = lax.broadcasted_iota(jnp.int32, (q, q), 1)
    lower = col <= row
    upper = col >= row
    causal = upper if reverse else lower
    tri = jnp.where(causal, 1.0, 0.0).astype(BF16)
    triT = jnp.where(lower if reverse else upper, 1.0, 0.0).astype(BF16)
    hsel = lax.broadcasted_iota(jnp.int32, (nh, SSD_D_INNER), 0)
    lsel = lax.broadcasted_iota(jnp.int32, (nh, SSD_D_INNER), 1) // SSD_HEADDIM
    expand = jnp.where(hsel == lsel, 1.0, 0.0).astype(BF16)
    neg_inf = jnp.float32(-jnp.inf)

    def one_chunk(r0):
        rows = slice(r0, r0 + q)
        dtd = dt_ref[rows, d * nh:(d + 1) * nh]
        dtdT = dtT_ref[d * nh:(d + 1) * nh, rows]
        ad = dtd * a_row
        adT = dtdT * a_col
        acs = _dot_sel_l(tri, ad)
        acsT = _dot_sel_r(adT, triT)
        tot = jnp.sum(ad, axis=0, keepdims=True)
        dte = jnp.exp(tot - acs)
        eacs = jnp.exp(acs)
        cdec = jnp.exp(tot)
        stacked = jnp.concatenate([dtd * dte, eacs, jnp.broadcast_to(cdec, (8, nh))], axis=0)
        s1 = stacked.astype(BF16)
        s2 = (stacked - s1.astype(F32)).astype(BF16)
        exp_all = _dot(s1, expand) + _dot(s2, expand)
        w_x = exp_all[0:q]
        eacs_x = exp_all[q:2 * q]
        cdec_x = exp_all[2 * q:2 * q + 1]

        xs = xs_ref[rows, :]
        xs_b = xs.astype(BF16)
        xdw = (xs * w_x).astype(BF16)
        y_groups = []
        for g in range(SSD_GROUPS):
            bg = b_ref[rows, g * SSD_STATE:(g + 1) * SSD_STATE]
            st_old = st_ref[g]
            s_new = _dot(bg.T.astype(BF16), xdw[:, g * rp:(g + 1) * rp])
            if mode != "state":
                bg_b = bg.astype(BF16)
                cg_b = c_ref[rows, g * SSD_STATE:(g + 1) * SSD_STATE].astype(BF16)
                cb = lax.dot_general(cg_b, bg_b, (((1,), (1,)), ((), ())),
                                     preferred_element_type=F32)
                y_off = _dot(cg_b, st_old.astype(BF16)) * eacs_x[:, g * rp:(g + 1) * rp]
                parts = []
                for r in range(SSD_HEADS // SSD_GROUPS):
                    h = g * (SSD_HEADS // SSD_GROUPS) + r
                    seg = acs[:, h:h + 1] - acsT[h:h + 1, :]
                    dec = jnp.exp(jnp.where(causal, seg, neg_inf))
                    mat = (cb * dec * dtdT[h:h + 1, :]).astype(BF16)
                    parts.append(_dot(mat, xs_b[:, h * SSD_HEADDIM:(h + 1) * SSD_HEADDIM]))
                y_groups.append(jnp.concatenate(parts, axis=1) + y_off)
            st_ref[g] = st_old * cdec_x[:, g * rp:(g + 1) * rp] + s_new

        if mode == "fwd":
            for g in range(SSD_GROUPS):
                y_ref[rows, g * rp:(g + 1) * rp] = y_groups[g]
        elif mode == "bwd":
            for g in range(SSD_GROUPS):
                sl = slice(g * rp, (g + 1) * rp)
                y = yf_ref[rows, sl] + y_groups[g] + dx_ref[:, sl] * xs[:, sl]
                yz = y * _silu(z_ref[rows, sl])
                yz = yz * lax.rsqrt(jnp.mean(yz * yz, axis=-1, keepdims=True) + EPS)
                y_ref[rows, sl] = (yz * ng_ref[:, sl]).astype(y_ref.dtype)

    for sub in (range(n_sub - 1, -1, -1) if reverse else range(n_sub)):
        one_chunk(sub * q)

    if mode == "state":
        @pl.when(step == pl.num_programs(0) - 1)
        def _():
            hfin_ref[...] = st_ref[...]


def _ssd(xbc, dt, dtT, alog, alogT, h0, *, reverse, mode, yf=None, z=None, dx=None, ng=None):
    l = xbc.shape[0]
    q = min(l, SSD_CHUNK * SSD_STEP_CHUNKS)
    nc = l // q
    di = SSD_D_INNER
    nb = di // SSD_GN
    if reverse:
        cidx = lambda c: nc - 1 - c
    else:
        cidx = lambda c: c
    st_shape = (SSD_GROUPS, SSD_STATE, di // SSD_GROUPS)
    full3 = pl.BlockSpec(st_shape, lambda c: (0, 0, 0))
    in_specs = [pl.BlockSpec((q, di), lambda c: (cidx(c), 0)),
                pl.BlockSpec((q, SSD_GN), lambda c: (cidx(c), nb)),
                pl.BlockSpec((q, SSD_GN), lambda c: (cidx(c), nb + 1)),
                pl.BlockSpec((q, dt.shape[1]), lambda c: (cidx(c), 0)),
                pl.BlockSpec((dtT.shape[0], q), lambda c: (0, cidx(c))),
                pl.BlockSpec((2, SSD_HEADS), lambda c: (0, 0)),
                pl.BlockSpec((SSD_HEADS, 2), lambda c: (0, 0)),
                full3]
    args = [xbc, xbc, xbc, dt, dtT, alog, alogT, h0]
    wide = pl.BlockSpec((q, di), lambda c: (cidx(c), 0))
    rowspec = pl.BlockSpec((1, di), lambda c: (0, 0))
    if mode == "state":
        out_shape = jax.ShapeDtypeStruct(st_shape, F32)
        out_specs = full3
    elif mode == "fwd":
        out_shape = jax.ShapeDtypeStruct((l, di), F32)
        out_specs = wide
    else:
        in_specs += [wide, wide, rowspec, rowspec]
        args += [yf, z, dx, ng]
        out_shape = jax.ShapeDtypeStruct((l, di), BF16)
        out_specs = wide
    return pl.pallas_call(
        functools.partial(_ssd_kernel, reverse=reverse, mode=mode),
        out_shape=out_shape,
        grid=(nc,),
        in_specs=in_specs,
        out_specs=out_specs,
        scratch_shapes=[pltpu.VMEM(st_shape, F32)],
        compiler_params=_cparams(("arbitrary",)),
        name="ssd_%s_%s" % (mode, "rev" if reverse else "fwd"),
    )(*args)


def _dft_tables(l):
    n = 2 * l
    hh = l // DFT_B
    ka = jnp.arange(hh, dtype=jnp.int32)
    n1 = jnp.arange(hh, dtype=jnp.int32)
    n2 = jnp.arange(DFT_B, dtype=jnp.int32)
    odd = 2 * ka + 1
    ang_a = ((odd[:, None] * (DFT_B * n1)[None, :]) % (2 * n)).astype(F32) * (math.pi / n)
    ang_b = ((n2[:, None] * odd[None, :]) % (2 * n)).astype(F32) * (math.pi / n)
    ca, sa = jnp.cos(ang_a)[None, :, :], jnp.sin(ang_a)[None, :, :]
    cb, sb = jnp.cos(ang_b)[:, :, None], jnp.sin(ang_b)[:, :, None]
    cs = ca * cb - sa * sb
    sn = sa * cb + ca * sb
    g1 = jnp.concatenate([cs, -sn], axis=1).astype(BF16)
    scale = 2.0 / n
    h2 = jnp.concatenate([jnp.swapaxes(cs, 1, 2), -jnp.swapaxes(sn, 1, 2)], axis=2) * scale
    kb = jnp.arange(DFT_B, dtype=jnp.int32)
    ph2 = (kb[:, None] * n2[None, :]) % DFT_B
    ang2 = ph2.astype(F32) * (2.0 * math.pi / DFT_B)
    fr, fi = jnp.cos(ang2), -jnp.sin(ang2)
    f2 = jnp.concatenate([jnp.concatenate([fr, -fi], axis=1),
                          jnp.concatenate([fi, fr], axis=1)], axis=0).astype(BF16)
    f2i = jnp.concatenate([jnp.concatenate([fr, fi], axis=1),
                           jnp.concatenate([-fi, fr], axis=1)], axis=0).astype(BF16)
    return g1, f2, f2i, h2.astype(BF16)


def _filter_mlp_kernel(bands_ref, w1_ref, b1_ref, f1_ref, w2_ref, b2_ref, f2_ref, o_ref, *, l):
    tm = o_ref.shape[0]
    base = pl.program_id(0) * tm
    idx = (lax.broadcasted_iota(jnp.int32, (tm, LANES), 0) + base).astype(F32)
    lane = lax.broadcasted_iota(jnp.int32, (tm, LANES), 1)
    t = idx * (1.0 / (l - 1))
    w = idx * (2.0 * math.pi / l)
    phase = jnp.where(lane <= HY_BANDS, 0.5 * math.pi, math.pi)
    feats = jnp.where(lane == 0, t,
                      jnp.where(lane < HY_EMB, jnp.sin(bands_ref[...] * w + phase), 0.0))
    h = jnp.sin(f1_ref[...] * (_dot_hi(feats, w1_ref[...]) + b1_ref[...]))
    h = jnp.sin(f2_ref[...] * (_dot_hi(h, w2_ref[...]) + b2_ref[...]))
    o_ref[...] = h.astype(o_ref.dtype)


def _filter_mlp(l, w1, b1, f1, w2, b2, f2, tm=1024):
    tm = min(tm, l)
    bands = jnp.linspace(1e-4, HY_BANDS - 1, HY_BANDS, dtype=F32)
    bands_row = jnp.zeros((1, LANES), F32).at[0, 1:1 + HY_BANDS].set(bands)
    bands_row = bands_row.at[0, 1 + HY_BANDS:HY_EMB].set(bands)
    w1p = jnp.zeros((LANES, HY_HIDDEN), F32).at[:HY_EMB].set(w1)
    full = lambda a: pl.BlockSpec(a.shape, lambda i: (0,) * a.ndim)
    args = [bands_row, w1p, b1.reshape(1, -1), f1.reshape(1, -1), w2, b2.reshape(1, -1),
            f2.reshape(1, -1)]
    return pl.pallas_call(
        functools.partial(_filter_mlp_kernel, l=l),
        out_shape=jax.ShapeDtypeStruct((l, HY_HIDDEN), BF16),
        grid=(l // tm,),
        in_specs=[full(a) for a in args],
        out_specs=pl.BlockSpec((tm, HY_HIDDEN), lambda i: (i, 0)),
        compiler_params=_cparams(("parallel",)),
        name="hyena_filter_mlp",
    )(*args)


def _pq_pitch(hh):
    return 2 * hh + 8


def _fwd_stage1(src_ref, g1_ref, pq_ref, hh):
    pqp = _pq_pitch(hh)

    def body(n2, carry):
        rows = src_ref[pl.ds(n2, hh, stride=PITCH), :]
        p = _dot(g1_ref[n2], rows.astype(BF16))
        pq_ref[pl.ds(pl.multiple_of(n2 * pqp, 8), 2 * hh), :] = p
        return carry
    lax.fori_loop(0, DFT_B, body, 0, unroll=UNROLL_OUTER)


def _load_spectrum_rows(pq_ref, ka, hh):
    pqp = _pq_pitch(hh)
    return jnp.concatenate([pq_ref[pl.ds(ka, DFT_B, stride=pqp), :],
                            pq_ref[pl.ds(hh + ka, DFT_B, stride=pqp), :]], axis=0)


def _filter_spec_kernel(hid_ref, wf_ref, wb_ref, dl_ref, g1_ref, f2_ref, kr_ref, ki_ref,
                        s_ref, d_ref, pq_ref, pq2_ref, *, l):
    hh = l // DFT_B
    t = lax.broadcasted_iota(jnp.int32, (l, LANES), 0).astype(F32) * (1.0 / (l - 1))
    dec = jnp.exp(-t * dl_ref[...])
    hid = hid_ref[...]
    hf = _dot(hid, wf_ref[...].astype(BF16)) * dec
    hb = _dot(hid, wb_ref[...].astype(BF16)) * dec
    first = lax.broadcasted_iota(jnp.int32, (l, LANES), 0) == 0
    hb = jnp.where(first, 0.0, hb)
    inv = 1.0 / (jnp.sum(jnp.abs(hf), axis=0, keepdims=True)
                 + jnp.sum(jnp.abs(hb), axis=0, keepdims=True))
    hs = hf + hb
    hd = hf - hb
    for n1 in range(hh):
        s_ref[n1 * PITCH:n1 * PITCH + DFT_B, :] = hs[n1 * DFT_B:(n1 + 1) * DFT_B]
        d_ref[n1 * PITCH:n1 * PITCH + DFT_B, :] = hd[n1 * DFT_B:(n1 + 1) * DFT_B]
    f2_re = f2_ref[0:DFT_B, :]
    pqp = _pq_pitch(hh)

    def stage1(n2, carry):
        g = g1_ref[n2]
        ps = _dot(g, s_ref[pl.ds(n2, hh, stride=PITCH), :].astype(BF16))
        pd = _dot(g, d_ref[pl.ds(n2, hh, stride=PITCH), :].astype(BF16))
        off = pl.multiple_of(n2 * pqp, 8)
        pq_ref[pl.ds(off, 2 * hh), :] = ps
        pq2_ref[pl.ds(off, 2 * hh), :] = jnp.concatenate([pd[hh:], -pd[:hh]], axis=0)
        return carry
    lax.fori_loop(0, DFT_B, stage1, 0, unroll=UNROLL_OUTER)

    def body(ka, carry):
        pp = jnp.concatenate([_load_spectrum_rows(pq_ref, ka, hh).astype(BF16),
                              _load_spectrum_rows(pq2_ref, ka, hh).astype(BF16)], axis=1)
        x = _dot(f2_re, pp)
        rows = pl.ds(pl.multiple_of(ka * DFT_B, DFT_B), DFT_B)
        kr_ref[rows, :] = (x[:, :LANES] * inv).astype(kr_ref.dtype)
        ki_ref[rows, :] = (x[:, LANES:] * inv).astype(ki_ref.dtype)
        return carry
    lax.fori_loop(0, hh, body, 0, unroll=UNROLL_MID)


def _filter_spec(hid, w3, deltas, g1, f2, l, c):
    hh = l // DFT_B
    nct = c // LANES
    orders = w3.shape[1] // (2 * c)
    spec = pl.BlockSpec((None, l, LANES), lambda o, j: (o, 0, j))
    out_sd = jax.ShapeDtypeStruct((orders, l, c), BF16)
    return pl.pallas_call(
        functools.partial(_filter_spec_kernel, l=l),
        out_shape=(out_sd, out_sd),
        grid=(orders, nct),
        in_specs=[pl.BlockSpec((l, HY_HIDDEN), lambda o, j: (0, 0)),
                  pl.BlockSpec((HY_HIDDEN, LANES), lambda o, j: (0, (2 * o) * nct + j)),
                  pl.BlockSpec((HY_HIDDEN, LANES), lambda o, j: (0, (2 * o + 1) * nct + j)),
                  pl.BlockSpec((1, LANES), lambda o, j: (0, j)),
                  pl.BlockSpec(g1.shape, lambda o, j: (0, 0, 0)),
                  pl.BlockSpec(f2.shape, lambda o, j: (0, 0))],
        out_specs=(spec, spec),
        scratch_shapes=[pltpu.VMEM((hh * PITCH, LANES), F32)] * 2
        + [pltpu.VMEM((DFT_B * _pq_pitch(hh), LANES), F32)] * 2,
        compiler_params=_cparams(("parallel", "parallel")),
        name="hyena_filter_spec",
    )(hid, w3, w3, deltas, g1, f2)


def _long_conv_kernel(u_ref, m_ref, kr_ref, ki_ref, bias_ref, g1_ref, f2_ref, f2i_ref, h2_ref,
                      o_ref, pq_ref, *, l):
    hh = l // DFT_B
    pqp = _pq_pitch(hh)
    _fwd_stage1(u_ref, g1_ref, pq_ref, hh)
    f2 = f2_ref[...]
    f2i = f2i_ref[...]

    def mid(ka, carry):
        koff = pl.multiple_of(ka * DFT_B, DFT_B)
        x = _dot(f2, _load_spectrum_rows(pq_ref, ka, hh).astype(BF16))
        xr, xi = x[:DFT_B], x[DFT_B:]
        kr = kr_ref[pl.ds(koff, DFT_B), :].astype(F32)
        ki = ki_ref[pl.ds(koff, DFT_B), :].astype(F32)
        yy = jnp.concatenate([xr * kr - xi * ki, xr * ki + xi * kr], axis=0).astype(BF16)
        qq = _dot(f2i, yy)
        pq_ref[pl.ds(ka, DFT_B, stride=pqp), :] = qq[:DFT_B]
        pq_ref[pl.ds(hh + ka, DFT_B, stride=pqp), :] = qq[DFT_B:]
        return carry
    lax.fori_loop(0, hh, mid, 0, unroll=UNROLL_MID)

    bias = bias_ref[...]

    def last(n2, carry):
        qq = pq_ref[pl.ds(pl.multiple_of(n2 * pqp, 8), 2 * hh), :].astype(BF16)
        y = _dot(h2_ref[n2], qq)
        u = u_ref[pl.ds(n2, hh, stride=PITCH), :]
        m = m_ref[pl.ds(n2, hh, stride=PITCH), :]
        o_ref[pl.ds(n2, hh, stride=PITCH), :] = (m * (y + bias * u)).astype(o_ref.dtype)
        return carry
    lax.fori_loop(0, DFT_B, last, 0, unroll=UNROLL_OUTER)
    for n1 in range(hh):
        o_ref[n1 * PITCH + DFT_B:(n1 + 1) * PITCH, :] = jnp.zeros((PITCH - DFT_B, LANES), o_ref.dtype)


def _long_conv(u_arr, u_blk, m_arr, m_blk, kr, ki, order, bias, tables, l, c):
    g1, f2, f2i, h2 = tables
    hh = l // DFT_B
    nct = c // LANES
    kspec = pl.BlockSpec((None, l, LANES), lambda j: (order, 0, j))
    return pl.pallas_call(
        functools.partial(_long_conv_kernel, l=l),
        out_shape=jax.ShapeDtypeStruct((hh * PITCH, c), F32),
        grid=(nct,),
        in_specs=[pl.BlockSpec((hh * PITCH, LANES), lambda j: (0, u_blk * nct + j)),
                  pl.BlockSpec((hh * PITCH, LANES), lambda j: (0, m_blk * nct + j)),
                  kspec, kspec,
                  pl.BlockSpec((1, LANES), lambda j: (0, j)),
                  pl.BlockSpec(g1.shape, lambda j: (0, 0, 0)),
                  pl.BlockSpec(f2.shape, lambda j: (0, 0)),
                  pl.BlockSpec(f2i.shape, lambda j: (0, 0)),
                  pl.BlockSpec(h2.shape, lambda j: (0, 0, 0))],
        out_specs=pl.BlockSpec((hh * PITCH, LANES), lambda j: (0, j)),
        scratch_shapes=[pltpu.VMEM((DFT_B * _pq_pitch(hh), LANES), F32)],
        compiler_params=_cparams(("parallel",)),
        name="hyena_long_conv",
    )(u_arr, m_arr, kr, ki, bias.reshape(1, c), g1, f2, f2i, h2)


def _merge_kernel(hx_ref, ys_ref, yh_ref, wg1_ref, wg2_ref, w1_ref, w2_ref, gb1_ref, gb2_ref, o_ref):
    hx = hx_ref[...]
    g1 = _sigmoid(_dot_nt(hx, wg1_ref[...]) + gb1_ref[...])
    g2 = _sigmoid(_dot_nt(hx, wg2_ref[...]) + gb2_ref[...])
    yh = yh_ref[:, 0:DFT_B, :].reshape(hx.shape).astype(BF16)
    out = g1 * _dot(ys_ref[...], w1_ref[...]) + g2 * _dot(yh, w2_ref[...])
    o_ref[...] = out.astype(o_ref.dtype)


def _merge(hx, ys, yh, wgt, row0, w1, w2, gate_b, tm=1024, tn=256):
    m, d = hx.shape
    nt = d // tn
    a_spec = pl.BlockSpec((tm, d), lambda i, j: (i, 0))
    w_spec = pl.BlockSpec((d, tn), lambda i, j: (0, j))
    gb = gate_b.reshape(1, 2 * d)
    return pl.pallas_call(
        _merge_kernel,
        out_shape=jax.ShapeDtypeStruct((m, d), BF16),
        grid=(m // tm, nt),
        in_specs=[a_spec, a_spec,
                  pl.BlockSpec((tm // DFT_B, PITCH, d), lambda i, j: (i, 0, 0)),
                  pl.BlockSpec((pl.Element(tn), pl.Element(d)),
                               lambda i, j: (pl.multiple_of(row0 + j * tn, ROW_ALIGN), 0)),
                  pl.BlockSpec((pl.Element(tn), pl.Element(d)),
                               lambda i, j: (pl.multiple_of(row0 + (nt + j) * tn, ROW_ALIGN), 0)),
                  w_spec, w_spec,
                  pl.BlockSpec((1, tn), lambda i, j: (0, j)),
                  pl.BlockSpec((1, tn), lambda i, j: (0, nt + j))],
        out_specs=pl.BlockSpec((tm, tn), lambda i, j: (i, j)),
        compiler_params=_cparams(("parallel", "parallel")),
        name="merge",
    )(hx, ys, yh, wgt, wgt, w1, w2, gb, gb)


def _oproj_kernel(a_ref, w_ref, x_ref, gt_ref, o_ref):
    o_ref[...] = x_ref[...] + gt_ref[...] * _dot(a_ref[...], w_ref[...])


def _oproj(a, w, x, gt, tm=2048, tn=512):
    m, d = a.shape
    n = w.shape[1]
    tm = min(tm, m)
    return pl.pallas_call(
        _oproj_kernel,
        out_shape=jax.ShapeDtypeStruct((m, n), F32),
        grid=(m // tm, n // tn),
        in_specs=[pl.BlockSpec((tm, d), lambda i, j: (i, 0)),
                  pl.BlockSpec((d, tn), lambda i, j: (0, j)),
                  pl.BlockSpec((tm, tn), lambda i, j: (i, j)),
                  pl.BlockSpec((1, tn), lambda i, j: (0, j))],
        out_specs=pl.BlockSpec((tm, tn), lambda i, j: (i, j)),
        compiler_params=_cparams(("parallel", "parallel")),
        name="oproj",
    )(a, w, x, gt)


def _pack_bf16_pairs(v):
    half = v.shape[1] // 2
    bits = lax.bitcast_convert_type(v.astype(BF16).astype(F32), jnp.uint32)
    return (bits[:, :half] >> 16) | (bits[:, half:] & jnp.uint32(0xFFFF0000))


def _unpack_bf16_pairs(p):
    lo = lax.bitcast_convert_type(p << 16, F32).astype(BF16)
    hi = lax.bitcast_convert_type(p & jnp.uint32(0xFFFF0000), F32).astype(BF16)
    return lo, hi


def _router_kernel(x_ref, g_ref, sh_ref, sc_ref, wr_ref, br_ref, h_ref, ids_ref, wts_ref):
    x = x_ref[...]
    y = x * lax.rsqrt(jnp.mean(x * x, axis=-1, keepdims=True) + EPS) * g_ref[...]
    h = y * (1.0 + sc_ref[...]) + sh_ref[...]
    h_ref[...] = _pack_bf16_pairs(h)
    logits = _dot_mid(h, wr_ref[...]) + br_ref[...]
    tm = x.shape[0]
    lane = lax.broadcasted_iota(jnp.int32, (tm, LANES), 1)
    neg = jnp.float32(-jnp.inf)
    big = jnp.int32(LANES)
    is_grp = (lane >= MOE_EXPERTS) & (lane < MOE_EXPERTS + MOE_GROUPS)
    gl = jnp.where(is_grp, logits, neg)
    gmax = jnp.max(gl, axis=-1, keepdims=True)
    gidx = jnp.min(jnp.where(gl == gmax, lane, big), axis=-1, keepdims=True) - MOE_EXPERTS
    gw = 1.0 / jnp.sum(jnp.where(is_grp, jnp.exp(logits - gmax), 0.0), axis=-1, keepdims=True)
    in_grp = (lane < MOE_EXPERTS) & ((lane // MOE_PER_GROUP) == gidx)
    el = jnp.where(in_grp, logits, neg)
    m1 = jnp.max(el, axis=-1, keepdims=True)
    i1 = jnp.min(jnp.where(el == m1, lane, big), axis=-1, keepdims=True)
    el2 = jnp.where(lane == i1, neg, el)
    m2 = jnp.max(el2, axis=-1, keepdims=True)
    i2 = jnp.min(jnp.where(el2 == m2, lane, big), axis=-1, keepdims=True)
    e21 = jnp.exp(m2 - m1)
    w1 = gw / (1.0 + e21)
    w2 = gw * e21 / (1.0 + e21)
    ids_ref[...] = jnp.where(lane == 0, i1, jnp.where(lane == 1, i2, -1))
    wts_ref[...] = jnp.where(lane == 0, w1, jnp.where(lane == 1, w2, 0.0))


def _router(x1, g, sh, sc, w_rg, b_rg, w_re, b_re, tm=512):
    m, d = x1.shape
    wr = jnp.zeros((d, LANES), F32).at[:, :MOE_EXPERTS].set(w_re)
    wr = wr.at[:, MOE_EXPERTS:MOE_EXPERTS + MOE_GROUPS].set(w_rg)
    br = jnp.zeros((1, LANES), F32).at[0, :MOE_EXPERTS].set(b_re)
    br = br.at[0, MOE_EXPERTS:MOE_EXPERTS + MOE_GROUPS].set(b_rg)
    row = pl.BlockSpec((1, d), lambda i: (0, 0))
    return pl.pallas_call(
        _router_kernel,
        out_shape=(jax.ShapeDtypeStruct((m, d // 2), jnp.uint32),
                   jax.ShapeDtypeStruct((m, LANES), jnp.int32),
                   jax.ShapeDtypeStruct((m, LANES), F32)),
        grid=(m // tm,),
        in_specs=[pl.BlockSpec((tm, d), lambda i: (i, 0)), row, row, row,
                  pl.BlockSpec((d, LANES), lambda i: (0, 0)),
                  pl.BlockSpec((1, LANES), lambda i: (0, 0))],
        out_specs=(pl.BlockSpec((tm, d // 2), lambda i: (i, 0)),
                   pl.BlockSpec((tm, LANES), lambda i: (i, 0)),
                   pl.BlockSpec((tm, LANES), lambda i: (i, 0))),
        compiler_params=_cparams(("parallel",)),
        name="router",
    )(x1, g.reshape(1, d), sh, sc, wr, br)


def _moe_positions_kernel(ids_ref, pos_ref, meta_ref):
    n = ids_ref.shape[0]
    t = MOE_TILE
    lane = lax.broadcasted_iota(jnp.int32, (t, LANES), 1)

    def onehot(k):
        idt = ids_ref[pl.ds(pl.multiple_of(k * t, t), t), :]
        i1, i2 = idt[:, 0:1], idt[:, 1:2]
        return i1, i2, jnp.where((lane == i1) | (lane == i2), 1.0, 0.0)

    def count(k, acc):
        return acc + jnp.sum(onehot(k)[2], axis=0, keepdims=True)
    total = lax.fori_loop(0, n // t, count, jnp.zeros((1, LANES), F32))
    padded = (((total.astype(jnp.int32) + (t - 1)) // t) * t).astype(F32)
    r128 = lax.broadcasted_iota(jnp.int32, (LANES, LANES), 0)
    c128 = lax.broadcasted_iota(jnp.int32, (LANES, LANES), 1)
    before = jnp.where(r128 < c128, 1.0, 0.0).astype(BF16)
    off = _dot_sel_r(jnp.broadcast_to(padded, (8, LANES)), before)[0:1]
    row = lax.broadcasted_iota(jnp.int32, (t, t), 0)
    col = lax.broadcasted_iota(jnp.int32, (t, t), 1)
    tri = jnp.where(col <= row, 1.0, 0.0).astype(BF16)

    def place(k, seen):
        i1, i2, oh = onehot(k)
        base = off + seen + _dot(tri, oh.astype(BF16)) - oh
        p1 = jnp.sum(jnp.where(lane == i1, base, 0.0), axis=1, keepdims=True)
        p2 = jnp.sum(jnp.where(lane == i2, base, 0.0), axis=1, keepdims=True)
        pos_ref[pl.ds(pl.multiple_of(k * t, t), t), :] = jnp.where(
            lane == 0, p1, jnp.where(lane == 1, p2, 0.0)).astype(jnp.int32)
        return seen + jnp.sum(oh, axis=0, keepdims=True)
    lax.fori_loop(0, n // t, place, jnp.zeros((1, LANES), F32))

    ends = off + padded
    start = (r128 * t).astype(F32)
    done = jnp.where((jnp.broadcast_to(ends, (LANES, LANES)) <= start) & (c128 < MOE_EXPERTS), 1.0, 0.0)
    tile_expert = jnp.minimum(jnp.sum(done, axis=1, keepdims=True), MOE_EXPERTS - 1.0)
    used = jnp.sum(jnp.where(c128[0:1] == MOE_EXPERTS - 1, ends, 0.0), axis=1, keepdims=True) / t
    lanef = c128.astype(F32)
    later = (lanef > tile_expert) & (jnp.broadcast_to(padded, (LANES, LANES)) > 0.0) & (c128 < MOE_EXPERTS)
    nxt = jnp.min(jnp.where(later, lanef, float(LANES)), axis=1, keepdims=True)
    nxt = jnp.where(nxt >= float(LANES), -1.0, nxt)
    meta_ref[...] = jnp.where(c128 == 0, tile_expert,
                              jnp.where(c128 == 1, used, jnp.where(c128 == 2, nxt, 0.0))).astype(jnp.int32)


def _moe_positions(ids):
    m = ids.shape[0]
    return pl.pallas_call(
        _moe_positions_kernel,
        out_shape=(jax.ShapeDtypeStruct((m, LANES), jnp.int32),
                   jax.ShapeDtypeStruct((LANES, LANES), jnp.int32)),
        grid=(1,),
        in_specs=[pl.BlockSpec((m, LANES), lambda i: (0, 0))],
        out_specs=(pl.BlockSpec((m, LANES), lambda i: (0, 0)),
                   pl.BlockSpec((LANES, LANES), lambda i: (0, 0))),
        compiler_params=_cparams(("arbitrary",)),
        name="moe_positions",
    )(ids)


def _moe_dispatch_kernel(p1_ref, p2_ref, h_ref, xs_in_ref, xs_ref, sem):
    del xs_in_ref
    t = h_ref.shape[0]
    base = pl.program_id(0) * t

    def issue(r, carry):
        src = h_ref.at[pl.ds(r, 1), :]
        pltpu.make_async_copy(src, xs_ref.at[pl.ds(p1_ref[base + r], 1), :], sem.at[0]).start()
        pltpu.make_async_copy(src, xs_ref.at[pl.ds(p2_ref[base + r], 1), :], sem.at[1]).start()
        return carry
    lax.fori_loop(0, t, issue, 0, unroll=8)
    pltpu.make_async_copy(h_ref, xs_ref.at[pl.ds(0, t), :], sem.at[0]).wait()
    pltpu.make_async_copy(h_ref, xs_ref.at[pl.ds(0, t), :], sem.at[1]).wait()


def _moe_dispatch(hp, pos1, pos2, n_sorted):
    m, w = hp.shape
    t = MOE_TILE
    zeros = jnp.zeros((n_sorted, w), hp.dtype)
    return pl.pallas_call(
        _moe_dispatch_kernel,
        out_shape=jax.ShapeDtypeStruct((n_sorted, w), hp.dtype),
        grid_spec=pltpu.PrefetchScalarGridSpec(
            num_scalar_prefetch=2,
            grid=(m // t,),
            in_specs=[pl.BlockSpec((t, w), lambda i, p1, p2: (i, 0)),
                      pl.BlockSpec(memory_space=pl.ANY)],
            out_specs=pl.BlockSpec(memory_space=pl.ANY),
            scratch_shapes=[pltpu.SemaphoreType.DMA((2,))]),
        input_output_aliases={3: 0},
        compiler_params=_cparams(("arbitrary",)),
        name="moe_dispatch",
    )(pos1, pos2, hp, zeros)


def _moe_experts_kernel(te_ref, nu_ref, nx_ref, x_ref, wg_hbm, wu_hbm, wd_hbm, y_ref,
                        wg_f, wu_f, wd_f, wg_b, wu_b, wd_b, slot_ref, sem):
    i = pl.program_id(0)
    live = i < nu_ref[0]
    new_expert = (i == 0) | (te_ref[i] != te_ref[jnp.maximum(i - 1, 0)])

    def copies(e, s):
        return (pltpu.make_async_copy(wg_hbm.at[e], wg_f.at[s], sem.at[s, 0]),
                pltpu.make_async_copy(wu_hbm.at[e], wu_f.at[s], sem.at[s, 1]),
                pltpu.make_async_copy(wd_hbm.at[e], wd_f.at[s], sem.at[s, 2]))

    @pl.when(i == 0)
    def _():
        slot_ref[0] = 0
        for cp in copies(te_ref[0], 0):
            cp.start()

    @pl.when(live & new_expert)
    def _():
        s = slot_ref[0]
        for cp in copies(te_ref[i], s):
            cp.wait()
        wg_b[...] = wg_f[s].astype(BF16)
        wu_b[...] = wu_f[s].astype(BF16)
        wd_b[...] = wd_f[s].astype(BF16)
        slot_ref[0] = 1 - s

        @pl.when(nx_ref[i] >= 0)
        def _():
            for cp in copies(nx_ref[i], 1 - s):
                cp.start()

    @pl.when(live)
    def _():
        x_lo, x_hi = _unpack_bf16_pairs(x_ref[...])
        half = x_lo.shape[1]
        hg = _dot(x_lo, wg_b[0:half, :]) + _dot(x_hi, wg_b[half:, :])
        hu = _dot(x_lo, wu_b[0:half, :]) + _dot(x_hi, wu_b[half:, :])
        y_ref[...] = _dot((_silu(hg) * hu).astype(BF16), wd_b[...])

    @pl.when(jnp.logical_not(live))
    def _():
        y_ref[...] = jnp.zeros_like(y_ref)


def _moe_experts(xs, tile_expert, n_used, next_expert, wg, wu, wd):
    ns = xs.shape[0]
    ne, d, f = wg.shape
    t = MOE_TILE
    hbm = pl.BlockSpec(memory_space=pl.ANY)
    return pl.pallas_call(
        _moe_experts_kernel,
        out_shape=jax.ShapeDtypeStruct((ns, d), F32),
        grid_spec=pltpu.PrefetchScalarGridSpec(
            num_scalar_prefetch=3,
            grid=(ns // t,),
            in_specs=[pl.BlockSpec((t, d // 2), lambda i, te, nu, nx: (i, 0)), hbm, hbm, hbm],
            out_specs=pl.BlockSpec((t, d), lambda i, te, nu, nx: (i, 0)),
            scratch_shapes=[pltpu.VMEM((2, d, f), F32), pltpu.VMEM((2, d, f), F32),
                            pltpu.VMEM((2, f, d), F32),
                            pltpu.VMEM((d, f), BF16), pltpu.VMEM((d, f), BF16),
                            pltpu.VMEM((f, d), BF16),
                            pltpu.SMEM((1,), jnp.int32), pltpu.SemaphoreType.DMA((2, 3))]),
        compiler_params=_cparams(("arbitrary",)),
        name="moe_experts",
    )(tile_expert, n_used, next_expert, xs, wg, wu, wd)


def _moe_combine_kernel(p1_ref, p2_ref, x_ref, wts_ref, gt_ref, fg_ref, y_ref, o_ref, ya, yb, sem):
    t = x_ref.shape[0]
    i = pl.program_id(0)
    n = pl.num_programs(0)

    def row_copy(src_row, buf, slot, r, which):
        return pltpu.make_async_copy(y_ref.at[pl.ds(src_row, 1), :], buf.at[slot, pl.ds(r, 1), :],
                                     sem.at[which, slot])

    def issue(tile, slot):
        def body(r, carry):
            row_copy(p1_ref[tile * t + r], ya, slot, r, 0).start()
            row_copy(p2_ref[tile * t + r], yb, slot, r, 1).start()
            return carry
        lax.fori_loop(0, t, body, 0, unroll=8)

    @pl.when(i == 0)
    def _():
        issue(0, 0)

    @pl.when(i + 1 < n)
    def _():
        issue(i + 1, (i + 1) % 2)

    slot = i % 2
    pltpu.make_async_copy(y_ref.at[pl.ds(0, t), :], ya.at[slot], sem.at[0, slot]).wait()
    pltpu.make_async_copy(y_ref.at[pl.ds(0, t), :], yb.at[slot], sem.at[1, slot]).wait()
    w = wts_ref[...]
    moe = w[:, 0:1] * ya[slot] + w[:, 1:2] * yb[slot]
    x2 = x_ref[...] + gt_ref[...] * moe
    o_ref[...] = x2 * lax.rsqrt(jnp.mean(x2 * x2, axis=-1, keepdims=True) + EPS) * fg_ref[...]


def _moe_combine(x1, wts, gt, fg, y_sorted, pos1, pos2):
    m, d = x1.shape
    t = MOE_TILE
    row = pl.BlockSpec((1, d), lambda i, p1, p2: (0, 0))
    return pl.pallas_call(
        _moe_combine_kernel,
        out_shape=jax.ShapeDtypeStruct((m, d), F32),
        grid_spec=pltpu.PrefetchScalarGridSpec(
            num_scalar_prefetch=2,
            grid=(m // t,),
            in_specs=[pl.BlockSpec((t, d), lambda i, p1, p2: (i, 0)),
                      pl.BlockSpec((t, LANES), lambda i, p1, p2: (i, 0)),
                      row, row,
                      pl.BlockSpec(memory_space=pl.ANY)],
            out_specs=pl.BlockSpec((t, d), lambda i, p1, p2: (i, 0)),
            scratch_shapes=[pltpu.VMEM((2, t, d), F32), pltpu.VMEM((2, t, d), F32),
                            pltpu.SemaphoreType.DMA((2, 2))]),
        compiler_params=_cparams(("arbitrary",)),
        name="moe_combine",
    )(pos1, pos2, x1, wts, gt, fg.reshape(1, d), y_sorted)


def kernel(x, c, ctx, c_ctx, ada_w, ada_b, norm1_g, w_in, ssd_conv_w, ssd_conv_b, ssd_dt_bias,
           ssd_a_log, ssd_d, ssd_norm_g, w_ssd_out, hy_conv_w, hy_conv_b, hy_f_w1, hy_f_b1,
           hy_f_freq1, hy_f_w2, hy_f_b2, hy_f_freq2, hy_f_w3, hy_bias, w_hy_out, gate_b, w_o,
           norm2_g, moe_w_rg, moe_b_rg, moe_w_re, moe_b_re, moe_w_gate, moe_w_up, moe_w_down,
           final_g):
    bsz, l, d = x.shape
    assert bsz == 1 and ada_w.shape[0] == 1
    lc = ctx.shape[1]
    di = SSD_D_INNER
    xbc_cols = di + 2 * SSD_GN
    col_xbc = di
    col_dt = col_xbc + xbc_cols
    col_hy = col_dt + 2 * SSD_HEADS
    hyw = hy_bias.shape[-1]

    x2d = x[0]
    ctx2d = ctx[0]

    cc = jnp.zeros((8, d), F32).at[0].set(c[0]).at[1].set(c_ctx)
    mods = _adaln(cc, ada_w[0], ada_b[0])
    sh1, sc1, gt1, sh2, sc2, gt2 = [mods[0:1, i * d:(i + 1) * d] for i in range(6)]
    csh1, csc1 = mods[1:2, 0:d], mods[1:2, d:2 * d]

    hx = _normmod(x2d, norm1_g[0], sh1, sc1, tm=512)
    hc = _normmod(ctx2d, norm1_g[0], csh1, csc1, tm=lc)

    w_in_t = w_in[0].T
    col_gate = col_hy + 3 * hyw
    wt_gate = w_in_t[col_gate:].astype(BF16)
    n_dt = 2 * SSD_HEADS
    dt_bias = ssd_dt_bias[0].reshape(-1)

    xbc_c = _proj(hc, w_in_t, col_xbc, xbc_cols, "conv_silu", tm=lc, tn=512, conv_w=ssd_conv_w[0],
                  bias=ssd_conv_b[0], group=lc)
    dt_c = _proj(hc, w_in_t, col_dt, n_dt, "softplus", tm=lc, tn=n_dt, bias=dt_bias)
    alog = ssd_a_log[0]
    alogT = alog.T
    zero_state = jnp.zeros((SSD_GROUPS, SSD_STATE, di // SSD_GROUPS), F32)
    h0f = _ssd(xbc_c, dt_c, dt_c.T, alog, alogT, zero_state, reverse=False, mode="state")
    h0b = _ssd(xbc_c, dt_c, dt_c.T, alog, alogT, zero_state, reverse=True, mode="state")

    z = _proj(hx, w_in_t, 0, di, "none", tm=PROJ_TM, tn=512)
    xbc = _proj(hx, w_in_t, col_xbc, xbc_cols, "conv_silu", tm=PROJ_TM, tn=512, conv_w=ssd_conv_w[0],
                bias=ssd_conv_b[0])
    dt = _proj(hx, w_in_t, col_dt, n_dt, "softplus", tm=PROJ_TM, tn=n_dt, bias=dt_bias)
    u3 = _proj(hx, w_in_t, col_hy, 3 * hyw, "conv", tm=PROJ_TM, tn=512, conv_w=hy_conv_w[0],
               bias=hy_conv_b[0], pitched=True).reshape(l // DFT_B * PITCH, 3 * hyw)

    dtT = dt.T
    yf = _ssd(xbc, dt, dtT, alog, alogT, h0f, reverse=False, mode="fwd")
    dx = jnp.repeat(ssd_d[0], SSD_HEADDIM).reshape(1, di)
    y_ssd = _ssd(xbc, dt, dtT, alog, alogT, h0b, reverse=True, mode="bwd", yf=yf, z=z, dx=dx,
                 ng=ssd_norm_g[0].reshape(1, di))

    tables = _dft_tables(l)
    hid = _filter_mlp(l, hy_f_w1[0], hy_f_b1[0], hy_f_freq1[0], hy_f_w2[0], hy_f_b2[0],
                      hy_f_freq2[0])
    max_decay = math.log(HY_DECAY_TARGET) / HY_FAST_DECAY_PCT
    min_decay = math.log(HY_DECAY_TARGET) / HY_SLOW_DECAY_PCT
    deltas = jnp.abs(jnp.linspace(min_decay, max_decay, hyw, dtype=F32)).reshape(1, hyw)
    kr, ki = _filter_spec(hid, hy_f_w3[0], deltas, tables[0], tables[1], l, hyw)
    z2 = _long_conv(u3, 0, u3, 1, kr, ki, 0, hy_bias[0, 0], tables, l, hyw)
    y_hy = _long_conv(z2, 0, u3, 2, kr, ki, 1, hy_bias[0, 1], tables, l, hyw)

    merged = _merge(hx, y_ssd, y_hy.reshape(l // DFT_B, PITCH, hyw), wt_gate, 0,
                    w_ssd_out[0].astype(BF16), w_hy_out[0].astype(BF16), gate_b[0])
    x1 = _oproj(merged, w_o[0].astype(BF16), x2d, gt1)

    h2, ids, wts = _router(x1, norm2_g[0], sh2, sc2, moe_w_rg[0], moe_b_rg[0], moe_w_re[0],
                           moe_b_re[0])
    pos, meta = _moe_positions(ids)
    pos1, pos2 = pos[:, 0], pos[:, 1]
    n_sorted = 2 * l + MOE_EXPERTS * MOE_TILE
    n_tiles = n_sorted // MOE_TILE
    assert n_tiles <= LANES
    xs = _moe_dispatch(h2, pos1, pos2, n_sorted)
    ys = _moe_experts(xs, meta[:n_tiles, 0], meta[0, 1:2], meta[:n_tiles, 2], moe_w_gate[0],
                      moe_w_up[0], moe_w_down[0])
    out = _moe_combine(x1, wts, gt2, final_g, ys, pos1, pos2)
    return out[None]
```
